```python
import jax, jax.numpy as jnp
from jax import lax
import numpy as np

D_MODEL = 4096
BATCH = 2
SEQ = 8192
DEPTH = 1

CHUNK = 64
Q_BLOCK = 128
N_MEM = 256
EPS = 1e-6

FOX_HEAD_DIM = 128
FOX_WIDTH = D_MODEL // 2
FOX_HEADS = FOX_WIDTH // FOX_HEAD_DIM
FOX_F_BIAS = 4.0
FOX_F_W_SCALE = 0.1

HGRN_EXPAND = 128
HGRN_WIDTH = D_MODEL // 2
HGRN_HEADS = HGRN_WIDTH // HGRN_EXPAND
HGRN_KDIM = HGRN_EXPAND
HGRN_VDIM = HGRN_WIDTH // HGRN_HEADS

MEM_HEADS = 4
MEM_WIDTH = D_MODEL // 2
MEM_HEAD_DIM = MEM_WIDTH // MEM_HEADS

N_BRANCH = 3

PROJ_SIZES = (FOX_WIDTH, FOX_WIDTH, FOX_WIDTH, FOX_HEADS,
              HGRN_WIDTH, HGRN_WIDTH, HGRN_WIDTH, HGRN_WIDTH,
              MEM_WIDTH, N_BRANCH * D_MODEL)
PROJ_TOTAL = sum(PROJ_SIZES)
FOX_F_START = 3 * FOX_WIDTH

N_GROUPS = 8
EXPERTS_PER_GROUP = 8
N_EXPERTS = N_GROUPS * EXPERTS_PER_GROUP
TOP_K = 2
D_EXPERT = D_MODEL // 8
MOE_BLOCK = 128

kernel_name = "chunk_causal_hybrid_fox_hgrn2_memxattn_hmoe"


def rms_norm(x, g):
    xf = x.astype(jnp.float32)
    y = xf * lax.rsqrt(jnp.mean(xf * xf, axis=-1, keepdims=True) + EPS)
    return (y * g.astype(jnp.float32)).astype(x.dtype)


def split_heads(t, n_heads):
    b, s, _ = t.shape
    return t.reshape(b, s, n_heads, -1).transpose(0, 2, 1, 3)


def merge_heads(t):
    b, h, s, d = t.shape
    return t.transpose(0, 2, 1, 3).reshape(b, s, h * d)


def forgetting_attention(q, k, v, f_logit):
    b, h, s, d = q.shape
    c = jnp.cumsum(jax.nn.log_sigmoid(f_logit.astype(jnp.float32)), axis=-1)
    n_blk = s // Q_BLOCK
    q_blocks = q.reshape(b, h, n_blk, Q_BLOCK, d).transpose(2, 0, 1, 3, 4)
    c_blocks = c.reshape(b, h, n_blk, Q_BLOCK).transpose(2, 0, 1, 3)
    k_pos = jnp.arange(s)
    scale = d ** -0.5

    def one_block(args):
        q_blk, c_blk, i = args
        q_pos = i * Q_BLOCK + jnp.arange(Q_BLOCK)
        logits = jnp.einsum('bhqd,bhkd->bhqk', q_blk, k,
                            preferred_element_type=jnp.float32) * scale
        logits = logits + c_blk[..., :, None] - c[..., None, :]
        logits = jnp.where(k_pos[None, :] <= q_pos[:, None], logits, -jnp.inf)
        p = jax.nn.softmax(logits, axis=-1)
        return jnp.einsum('bhqk,bhkd->bhqd', p.astype(v.dtype), v)

    out = lax.map(one_block, (q_blocks, c_blocks, jnp.arange(n_blk)))
    return out.transpose(1, 2, 0, 3, 4).reshape(b, h, s, d)


def hgrn2_recurrence(q, k, v, log_f):
    b, h, s, dk = q.shape
    dv = v.shape[-1]
    n_c = s // CHUNK

    def to_chunks(t):
        return t.astype(jnp.float32).reshape(b, h, n_c, CHUNK, t.shape[-1]).transpose(2, 0, 1, 3, 4)

    causal = jnp.tril(jnp.ones((CHUNK, CHUNK), dtype=bool))

    def step(state, inp):
        qc, kc, vc, gc = inp
        bcum = jnp.cumsum(gc, axis=2)
        inter = jnp.einsum('bhtk,bhkv->bhtv', qc * jnp.exp(bcum), state)
        diff = bcum[:, :, :, None, :] - bcum[:, :, None, :, :]
        decay = jnp.exp(jnp.where(causal[:, :, None], diff, -jnp.inf))
        scores = jnp.einsum('bhtk,bhsk,bhtsk->bhts', qc, kc, decay)
        intra = jnp.einsum('bhts,bhsv->bhtv', scores, vc)
        b_last = bcum[:, :, -1:, :]
        new_state = (jnp.exp(b_last[:, :, 0, :])[..., None] * state
                     + jnp.einsum('bhsk,bhsv->bhkv', kc * jnp.exp(b_last - bcum), vc))
        return new_state, inter + intra

    s0 = jnp.zeros((b, h, dk, dv), jnp.float32)
    _, out = lax.scan(step, s0, (to_chunks(q), to_chunks(k), to_chunks(v), to_chunks(log_f)))
    return out.transpose(1, 2, 0, 3, 4).reshape(b, h, s, dv)


def memory_attention(q, mk, mv):
    d = q.shape[-1]
    logits = jnp.einsum('bhqd,bhmd->bhqm', q, mk, preferred_element_type=jnp.float32) * (d ** -0.5)
    p = jax.nn.softmax(logits, axis=-1)
    return jnp.einsum('bhqm,bhmd->bhqd', p.astype(mv.dtype), mv)


def hierarchical_moe(hn, w_group, b_group, w_router, b_router, w_gate, w_up, w_down):
    n, d = hn.shape
    g_logits = jnp.matmul(hn, w_group, preferred_element_type=jnp.float32) + b_group.astype(jnp.float32)
    g_prob = jax.nn.softmax(g_logits, axis=-1)
    grp = jnp.argmax(g_logits, axis=-1)
    p_grp = jnp.take_along_axis(g_prob, grp[:, None], axis=-1)[:, 0]
    e_logits = jnp.matmul(hn, w_router, preferred_element_type=jnp.float32) + b_router.astype(jnp.float32)
    e_logits = e_logits.reshape(n, N_GROUPS, EXPERTS_PER_GROUP)
    e_in_grp = jnp.take_along_axis(e_logits, grp[:, None, None], axis=1)[:, 0]
    top_val, top_idx = lax.top_k(e_in_grp, TOP_K)
    weights = p_grp[:, None] * jax.nn.softmax(top_val, axis=-1)
    expert_id = grp[:, None] * EXPERTS_PER_GROUP + top_idx

    nk = n * TOP_K
    flat_e = expert_id.reshape(nk).astype(jnp.int32)
    flat_tok = jnp.repeat(jnp.arange(n, dtype=jnp.int32), TOP_K)
    order = jnp.argsort(flat_e)
    sorted_e = flat_e[order]
    counts = jnp.bincount(flat_e, length=N_EXPERTS)
    start = jnp.cumsum(counts) - counts
    padded = (counts + MOE_BLOCK - 1) // MOE_BLOCK * MOE_BLOCK
    pad_end = jnp.cumsum(padded)
    pad_start = pad_end - padded
    dest_sorted = pad_start[sorted_e] + (jnp.arange(nk) - start[sorted_e])
    n_blocks = (nk + N_EXPERTS * (MOE_BLOCK - 1) + MOE_BLOCK - 1) // MOE_BLOCK
    cap = n_blocks * MOE_BLOCK
    tok_buf = jnp.zeros((cap,), jnp.int32).at[dest_sorted].set(flat_tok[order])
    dest = jnp.zeros((nk,), jnp.int32).at[order].set(dest_sorted)
    blk_start = jnp.arange(n_blocks) * MOE_BLOCK
    blk_e = jnp.minimum(jnp.sum(pad_end[None, :] <= blk_start[:, None], axis=1), N_EXPERTS - 1)

    def run_block(args):
        tok_blk, e = args
        xb = hn[tok_blk]
        a = xb @ w_gate[e]
        u = xb @ w_up[e]
        return (jax.nn.silu(a) * u) @ w_down[e]

    y_buf = lax.map(run_block, (tok_buf.reshape(n_blocks, MOE_BLOCK), blk_e)).reshape(cap, d)
    y = y_buf[dest].reshape(n, TOP_K, d)
    return jnp.sum(y * weights[..., None].astype(hn.dtype), axis=1)


def setup_inputs(seed: int = 0) -> dict:
    key = jax.random.key(seed)
    ks = jax.random.split(key, 26)
    f32 = jnp.float32

    def nrm(k, shape, scale):
        return jax.random.normal(k, shape, f32) * scale

    col_scale = jnp.ones((PROJ_TOTAL,), f32).at[FOX_F_START:FOX_F_START + FOX_HEADS].set(FOX_F_W_SCALE)
    return {
        "x": nrm(ks[0], (BATCH, SEQ, D_MODEL), 1.0),
        "mem": nrm(ks[1], (BATCH, N_MEM, D_MODEL), 1.0),
        "norm1_g": 1.0 + nrm(ks[2], (DEPTH, D_MODEL), 0.02),
        "norm2_g": 1.0 + nrm(ks[3], (DEPTH, D_MODEL), 0.02),
        "mem_norm_g": 1.0 + nrm(ks[4], (D_MODEL,), 0.02),
        "w_in": nrm(ks[5], (DEPTH, D_MODEL, PROJ_TOTAL), D_MODEL ** -0.5) * col_scale,
        "b_gate": nrm(ks[6], (DEPTH, N_BRANCH * D_MODEL), 0.02),
        "fox_f_bias": FOX_F_BIAS + nrm(ks[7], (DEPTH, FOX_HEADS), 0.5),
        "fox_q_g": 1.0 + nrm(ks[8], (DEPTH, FOX_HEAD_DIM), 0.02),
        "fox_k_g": 1.0 + nrm(ks[9], (DEPTH, FOX_HEAD_DIM), 0.02),
        "hgrn_lb_logits": nrm(ks[10], (DEPTH + 1, HGRN_WIDTH), 0.1),
        "hgrn_out_g": 1.0 + nrm(ks[11], (DEPTH, HGRN_VDIM), 0.02),
        "mem_q_g": 1.0 + nrm(ks[12], (DEPTH, MEM_HEAD_DIM), 0.02),
        "mem_k_g": 1.0 + nrm(ks[13], (DEPTH, MEM_HEAD_DIM), 0.02),
        "w_mem_kv": nrm(ks[14], (DEPTH, D_MODEL, 2 * MEM_WIDTH), D_MODEL ** -0.5),
        "w_branch_fox": nrm(ks[15], (DEPTH, FOX_WIDTH, D_MODEL), FOX_WIDTH ** -0.5),
        "w_branch_hgrn": nrm(ks[16], (DEPTH, HGRN_WIDTH, D_MODEL), HGRN_WIDTH ** -0.5),
        "w_branch_mem": nrm(ks[17], (DEPTH, MEM_WIDTH, D_MODEL), MEM_WIDTH ** -0.5),
        "w_out": nrm(ks[18], (DEPTH, D_MODEL, D_MODEL), D_MODEL ** -0.5),
        "w_group": nrm(ks[19], (DEPTH, D_MODEL, N_GROUPS), D_MODEL ** -0.5),
        "b_group": nrm(ks[20], (DEPTH, N_GROUPS), 0.01),
        "w_router": nrm(ks[21], (DEPTH, D_MODEL, N_EXPERTS), D_MODEL ** -0.5),
        "b_router": nrm(ks[22], (DEPTH, N_EXPERTS), 0.01),
        "w_exp_gate": nrm(ks[23], (DEPTH, N_EXPERTS, D_MODEL, D_EXPERT), D_MODEL ** -0.5),
        "w_exp_up": nrm(ks[24], (DEPTH, N_EXPERTS, D_MODEL, D_EXPERT), D_MODEL ** -0.5),
        "w_exp_down": nrm(ks[25], (DEPTH, N_EXPERTS, D_EXPERT, D_MODEL), D_EXPERT ** -0.5),
    }


def reference(x, mem, norm1_g, norm2_g, mem_norm_g, w_in, b_gate, fox_f_bias, fox_q_g, fox_k_g,
              hgrn_lb_logits, hgrn_out_g, mem_q_g, mem_k_g, w_mem_kv, w_branch_fox, w_branch_hgrn,
              w_branch_mem, w_out, w_group, b_group, w_router, b_router, w_exp_gate, w_exp_up,
              w_exp_down):
    b, s, d = x.shape
    splits = [int(v) for v in np.cumsum(PROJ_SIZES)[:-1]]
    lb_all = jnp.cumsum(jax.nn.softmax(hgrn_lb_logits.astype(jnp.float32), axis=0), axis=0)
    mem_n = rms_norm(mem, mem_norm_g)
    h = x
    for layer in range(DEPTH):
        hn = rms_norm(h, norm1_g[layer])
        proj = hn @ w_in[layer]
        (fq, fk, fv, ff, hq, hf, hi, hg, mq, gates) = jnp.split(proj, splits, axis=-1)

        q_a = rms_norm(split_heads(fq, FOX_HEADS), fox_q_g[layer])
        k_a = rms_norm(split_heads(fk, FOX_HEADS), fox_k_g[layer])
        v_a = split_heads(fv, FOX_HEADS)
        f_logit = (ff + fox_f_bias[layer].astype(ff.dtype)).transpose(0, 2, 1)
        o_a = merge_heads(forgetting_attention(q_a, k_a, v_a, f_logit))

        lb = lb_all[layer]
        z = hf.astype(jnp.float32)
        log_f = jnp.log(lb + (1.0 - lb) * jax.nn.sigmoid(z))
        k_b = (1.0 - lb) * jax.nn.sigmoid(-z)
        o_b = hgrn2_recurrence(split_heads(hq, HGRN_HEADS), split_heads(k_b, HGRN_HEADS),
                               split_heads(hi, HGRN_HEADS), split_heads(log_f, HGRN_HEADS))
        o_b = rms_norm(o_b.astype(x.dtype), hgrn_out_g[layer])
        o_b = merge_heads(o_b) * jax.nn.silu(hg)

        mkv = mem_n @ w_mem_kv[layer]
        mk, mv = jnp.split(mkv, 2, axis=-1)
        q_c = rms_norm(split_heads(mq, MEM_HEADS), mem_q_g[layer])
        k_c = rms_norm(split_heads(mk, MEM_HEADS), mem_k_g[layer])
        o_c = merge_heads(memory_attention(q_c, k_c, split_heads(mv, MEM_HEADS)))

        g = jax.nn.sigmoid(gates + b_gate[layer].astype(gates.dtype)).reshape(b, s, N_BRANCH, d)
        merged = (g[:, :, 0] * (o_a @ w_branch_fox[layer])
                  + g[:, :, 1] * (o_b @ w_branch_hgrn[layer])
                  + g[:, :, 2] * (o_c @ w_branch_mem[layer]))
        h = h + merged @ w_out[layer]

        hn2 = rms_norm(h, norm2_g[layer]).reshape(b * s, d)
        y = hierarchical_moe(hn2, w_group[layer], b_group[layer], w_router[layer], b_router[layer],
                             w_exp_gate[layer], w_exp_up[layer], w_exp_down[layer])
        h = h + y.reshape(b, s, d)
    return h
```

```python
import functools

import jax
import jax.numpy as jnp
from jax import lax
from jax.experimental import pallas as pl
from jax.experimental.pallas import tpu as pltpu

F32 = jnp.float32
BF16 = jnp.bfloat16

EPS = 1e-6
CHUNK = 64
SUB = 16
HGRN_EXPAND = 128
TOP_K = 2
MOE_BLOCK = 128
LANES = 128
V7X_VMEM_LIMIT_CAP = 60000 * 1024


def _vmem_limit(block_bytes, temp_bytes=0):
    need = int(1.25 * (2 * sum(block_bytes) + temp_bytes)) + (2 << 20)
    return min(max(need, 16 << 20), V7X_VMEM_LIMIT_CAP)


def _nbytes(shape, dtype):
    n = 1
    for s in shape:
        n *= s
    return n * jnp.dtype(dtype).itemsize


def _tile(dim, want):
    t = min(dim, want)
    assert dim % t == 0, (dim, want)
    return t


def _rmsnorm_kernel(x_ref, g_ref, o_ref):
    x = x_ref[...].astype(F32)
    y = x * lax.rsqrt(jnp.mean(x * x, axis=-1, keepdims=True) + EPS)
    o_ref[...] = (y * g_ref[...]).astype(o_ref.dtype)


def rmsnorm_rows(x, g, out_dtype, tm=256):
    m, d = x.shape
    tm = _tile(m, tm)
    return pl.pallas_call(
        _rmsnorm_kernel,
        grid=(m // tm,),
        in_specs=[pl.BlockSpec((tm, d), lambda i: (i, 0)),
                  pl.BlockSpec((1, d), lambda i: (0, 0))],
        out_specs=pl.BlockSpec((tm, d), lambda i: (i, 0)),
        out_shape=jax.ShapeDtypeStruct((m, d), out_dtype),
        compiler_params=pltpu.CompilerParams(
            dimension_semantics=("parallel",),
            vmem_limit_bytes=_vmem_limit([_nbytes((tm, d), x.dtype), _nbytes((tm, d), out_dtype)],
                                         2 * _nbytes((tm, d), F32))),
        name="rmsnorm_rows",
    )(x, g.reshape(1, d).astype(F32))


def _mm_kernel(a_ref, w_ref, *refs, epilogue):
    o_ref = refs[-1]
    acc = jnp.dot(a_ref[...], w_ref[...], preferred_element_type=F32)
    epilogue(acc, o_ref, *refs[:-1])


def _epi_plain(acc, o_ref):
    o_ref[...] = acc.astype(o_ref.dtype)


def _epi_headnorm(acc, o_ref, g_ref, *, head_dim):
    for h in range(acc.shape[1] // head_dim):
        sl = slice(h * head_dim, (h + 1) * head_dim)
        blk = acc[:, sl]
        y = blk * lax.rsqrt(jnp.mean(blk * blk, axis=-1, keepdims=True) + EPS)
        o_ref[:, sl] = (y * g_ref[:, sl]).astype(o_ref.dtype)


def _epi_sigmoid_bias(acc, o_ref, b_ref):
    o_ref[...] = jax.nn.sigmoid(acc + b_ref[...]).astype(o_ref.dtype)


def _epi_residual(acc, o_ref, r_ref):
    o_ref[...] = (r_ref[...] + acc).astype(o_ref.dtype)


def matmul(a, w, out_dtype, epilogue=_epi_plain, row_vec=None, residual=None, tm=1024, tn=1024,
           name="matmul"):
    m, k = a.shape
    n = w.shape[1]
    tm, tn = _tile(m, tm), _tile(n, tn)
    in_specs = [pl.BlockSpec((tm, k), lambda i, j: (i, 0)),
                pl.BlockSpec((k, tn), lambda i, j: (0, j))]
    operands = [a, w]
    blocks = [_nbytes((tm, k), a.dtype), _nbytes((k, tn), w.dtype), _nbytes((tm, tn), out_dtype)]
    if row_vec is not None:
        in_specs.append(pl.BlockSpec((1, tn), lambda i, j: (0, j)))
        operands.append(row_vec.reshape(1, n).astype(F32))
    if residual is not None:
        in_specs.append(pl.BlockSpec((tm, tn), lambda i, j: (i, j)))
        operands.append(residual)
        blocks.append(_nbytes((tm, tn), residual.dtype))
    return pl.pallas_call(
        functools.partial(_mm_kernel, epilogue=epilogue),
        grid=(m // tm, n // tn),
        in_specs=in_specs,
        out_specs=pl.BlockSpec((tm, tn), lambda i, j: (i, j)),
        out_shape=jax.ShapeDtypeStruct((m, n), out_dtype),
        compiler_params=pltpu.CompilerParams(
            dimension_semantics=("parallel", "parallel"),
            vmem_limit_bytes=_vmem_limit(blocks, 2 * _nbytes((tm, tn), F32))),
        name=name,
    )(*operands)


def _fox_cumsum_kernel(f_ref, b_ref, o_ref):
    x = f_ref[0] + b_ref[0]
    a = jnp.minimum(x, 0.0) - jnp.log(1.0 + jnp.exp(-jnp.abs(x)))
    r, l = a.shape
    lane = lax.broadcasted_iota(jnp.int32, (r, l), 1)
    row = lax.broadcasted_iota(jnp.int32, (r, l), 0)
    sh = 1
    while sh < l:
        a = a + jnp.where(lane >= sh, pltpu.roll(a, sh, axis=1), 0.0)
        sh *= 2
    tot = jnp.broadcast_to(a[:, l - 1:l], (r, l))
    t = tot
    sh = 1
    while sh < r:
        t = t + jnp.where(row >= sh, pltpu.roll(t, sh, axis=0), 0.0)
        sh *= 2
    o_ref[0] = -(a + (t - tot))


def fox_neg_cumsum(f_t, bias):
    bh, s = f_t.shape
    r = s // LANES
    out = pl.pallas_call(
        _fox_cumsum_kernel,
        grid=(bh,),
        in_specs=[pl.BlockSpec((1, r, LANES), lambda i: (i, 0, 0)),
                  pl.BlockSpec((1, 1, LANES), lambda i: (i, 0, 0))],
        out_specs=pl.BlockSpec((1, r, LANES), lambda i: (i, 0, 0)),
        out_shape=jax.ShapeDtypeStruct((bh, r, LANES), F32),
        compiler_params=pltpu.CompilerParams(dimension_semantics=("parallel",)),
        name="fox_cumsum",
    )(f_t.reshape(bh, r, LANES), jnp.broadcast_to(bias.reshape(bh, 1, 1), (bh, 1, LANES)).astype(F32))
    return out.reshape(bh, 1, s)


def _fox_attn_kernel(q_ref, k_ref, v_ref, b_ref, o_ref, m_scr, l_scr, acc_scr):
    i = pl.program_id(2)
    j = pl.program_id(3)

    @pl.when(j == 0)
    def _():
        m_scr[...] = jnp.full(m_scr.shape, -jnp.inf, F32)
        l_scr[...] = jnp.zeros(l_scr.shape, F32)
        acc_scr[...] = jnp.zeros(acc_scr.shape, F32)

    def step(masked):
        s = lax.dot_general(q_ref[...], k_ref[...], (((1,), (1,)), ((), ())),
                            preferred_element_type=F32)
        s = s + b_ref[0]
        if masked:
            tq, tk = s.shape
            row = lax.broadcasted_iota(jnp.int32, (tq, tk), 0)
            col = lax.broadcasted_iota(jnp.int32, (tq, tk), 1)
            s = jnp.where(col <= row, s, -jnp.inf)
        m_prev = m_scr[...]
        m_new = jnp.maximum(m_prev, jnp.max(s, axis=1, keepdims=True))
        alpha = jnp.exp(m_prev - m_new)
        p = jnp.exp(s - m_new)
        l_scr[...] = alpha * l_scr[...] + jnp.sum(p, axis=1, keepdims=True)
        acc_scr[...] = alpha * acc_scr[...] + jnp.dot(p.astype(BF16), v_ref[...],
                                                      preferred_element_type=F32)
        m_scr[...] = m_new

    @pl.when(j < i)
    def _():
        step(False)

    @pl.when(j == i)
    def _():
        step(True)
        o_ref[...] = (acc_scr[...] / l_scr[...]).astype(o_ref.dtype)


def fox_attention(qk, vsrc, v_col0, negc, batch, heads, head_dim, tq=512):
    m = qk.shape[0]
    s = m // batch
    tq = _tile(s, tq)
    nq = s // tq
    q_spec = pl.BlockSpec((tq, head_dim), lambda b, h, i, j: (b * nq + i, h))
    k_spec = pl.BlockSpec((tq, head_dim), lambda b, h, i, j: (b * nq + jnp.minimum(i, j), heads + h))
    v_spec = pl.BlockSpec((tq, head_dim), lambda b, h, i, j: (b * nq + jnp.minimum(i, j), v_col0 + h))
    b_spec = pl.BlockSpec((1, 1, tq), lambda b, h, i, j: (b * heads + h, 0, jnp.minimum(i, j)))
    return pl.pallas_call(
        _fox_attn_kernel,
        grid=(batch, heads, nq, nq),
        in_specs=[q_spec, k_spec, v_spec, b_spec],
        out_specs=pl.BlockSpec((tq, head_dim), lambda b, h, i, j: (b * nq + i, h)),
        out_shape=jax.ShapeDtypeStruct((m, heads * head_dim), BF16),
        scratch_shapes=[pltpu.VMEM((tq, 1), F32), pltpu.VMEM((tq, 1), F32),
                        pltpu.VMEM((tq, head_dim), F32)],
        compiler_params=pltpu.CompilerParams(
            dimension_semantics=("parallel", "parallel", "parallel", "arbitrary"),
            vmem_limit_bytes=_vmem_limit([4 * _nbytes((tq, head_dim), BF16)], 6 * _nbytes((tq, tq), F32))),
        name="fox_attention",
    )(qk, qk, vsrc, negc)


def _hgrn_kernel(q_ref, z_ref, v_ref, hg_ref, lb_ref, og_ref, o_ref, st_ref):
    @pl.when(pl.program_id(2) == 0)
    def _():
        st_ref[...] = jnp.zeros(st_ref.shape, F32)

    t, dk = z_ref.shape
    dv = v_ref.shape[1]
    nc, nb, nsub = t // CHUNK, t // SUB, CHUNK // SUB

    q = q_ref[...].astype(F32)
    z = z_ref[...]
    lb = lb_ref[...]
    g = jnp.log(lb + (1.0 - lb) * jax.nn.sigmoid(z))
    kk = (1.0 - lb) * jax.nn.sigmoid(-z)

    row = lax.broadcasted_iota(jnp.int32, (t, dk), 0)
    rin = row % CHUNK
    b = g
    sh = 1
    while sh < CHUNK:
        b = b + jnp.where(rin >= sh, pltpu.roll(b, sh, axis=0), 0.0)
        sh *= 2

    b3 = b.reshape(nc, CHUNK, dk)
    q3 = q.reshape(nc, CHUNK, dk)
    k3 = kk.reshape(nc, CHUNK, dk)
    v3 = v_ref[...].reshape(nc, CHUNK, dv)
    b_last = b3[:, CHUNK - 1:CHUNK, :]
    q_in = (q3 * jnp.exp(b3)).astype(BF16)
    k_out = (k3 * jnp.exp(b_last - b3)).astype(BF16)
    d_last = jnp.exp(b_last)

    st = st_ref[...]
    inter = []
    for c in range(nc):
        inter.append(lax.dot_general(q_in[c], st.astype(BF16), (((1,), (1,)), ((), ())),
                                     preferred_element_type=F32))
        kv_t = lax.dot_general(v3[c], k_out[c], (((0,), (0,)), ((), ())),
                               preferred_element_type=F32)
        st = st * d_last[c] + kv_t
    st_ref[...] = st
    inter = jnp.concatenate(inter, axis=0)

    rsub_c = (rin // SUB).reshape(nc, CHUNK, dk)
    q_parts, k_parts = [], []
    for blk in range(1, nsub):
        b_ref_pt = b3[:, blk * SUB - 1:blk * SUB, :]
        q_hat = q3 * jnp.exp(jnp.minimum(b3 - b_ref_pt, 0.0))
        q_parts.append(jnp.where(rsub_c == blk, q_hat, 0.0).astype(BF16))
        k_parts.append((k3 * jnp.exp(jnp.minimum(b_ref_pt - b3, 0.0))).astype(BF16))
    q_cat = jnp.concatenate(q_parts, axis=2)
    k_cat = jnp.concatenate(k_parts, axis=2)
    s_off = lax.dot_general(q_cat, k_cat, (((2,), (2,)), ((0,), (0,))),
                            preferred_element_type=F32)

    b4 = b.reshape(nb, SUB, dk)
    q4 = q.reshape(nb, SUB, dk)
    k4 = kk.reshape(nb, SUB, dk)
    rsub = (row % SUB).reshape(nb, SUB, dk)
    u_row = lax.broadcasted_iota(jnp.int32, (dk, LANES), 1)
    s_diag = jnp.zeros((t, LANES), F32)
    for j in range(SUB):
        p = q4 * k4[:, j:j + 1, :] * jnp.exp(jnp.minimum(b4 - b4[:, j:j + 1, :], 0.0))
        p = jnp.where(rsub >= j, p, 0.0).reshape(t, dk).astype(BF16)
        place = jnp.where((u_row % SUB == j) & (u_row < CHUNK), 1.0, 0.0).astype(BF16)
        s_diag = s_diag + jnp.dot(p, place, preferred_element_type=F32)
    s_diag = s_diag.reshape(nc, CHUNK, LANES)[:, :, :CHUNK]

    rb = lax.broadcasted_iota(jnp.int32, (nc, CHUNK, CHUNK), 1) // SUB
    cb = lax.broadcasted_iota(jnp.int32, (nc, CHUNK, CHUNK), 2) // SUB
    scores = jnp.where(cb == rb, s_diag, jnp.where(cb < rb, s_off, 0.0)).astype(BF16)
    intra = lax.dot_general(scores, v3, (((2,), (1,)), ((0,), (0,))),
                            preferred_element_type=F32).reshape(t, dv)

    o = inter + intra
    o = o * lax.rsqrt(jnp.mean(o * o, axis=-1, keepdims=True) + EPS) * og_ref[...]
    hg = hg_ref[...].astype(F32)
    o_ref[...] = (o * (hg * jax.nn.sigmoid(hg))).astype(o_ref.dtype)


def hgrn2(src, q_col0, v_col0, hg_col0, z, lb, out_g, batch, heads, rows=512):
    m = z.shape[0]
    s = m // batch
    dk = HGRN_EXPAND
    dv = out_g.shape[-1]
    rows = _tile(s, rows)
    nt = s // rows
    row_blk = lambda b, h, t: b * nt + t
    return pl.pallas_call(
        _hgrn_kernel,
        grid=(batch, heads, nt),
        in_specs=[pl.BlockSpec((rows, dk), lambda b, h, t: (row_blk(b, h, t), q_col0 + h)),
                  pl.BlockSpec((rows, dk), lambda b, h, t: (row_blk(b, h, t), h)),
                  pl.BlockSpec((rows, dv), lambda b, h, t: (row_blk(b, h, t), v_col0 + h)),
                  pl.BlockSpec((rows, dv), lambda b, h, t: (row_blk(b, h, t), hg_col0 + h)),
                  pl.BlockSpec((1, dk), lambda b, h, t: (0, h)),
                  pl.BlockSpec((1, dv), lambda b, h, t: (0, 0))],
        out_specs=pl.BlockSpec((rows, dv), lambda b, h, t: (row_blk(b, h, t), h)),
        out_shape=jax.ShapeDtypeStruct((m, heads * dv), BF16),
        scratch_shapes=[pltpu.VMEM((dv, dk), F32)],
        compiler_params=pltpu.CompilerParams(
            dimension_semantics=("parallel", "parallel", "arbitrary"),
            vmem_limit_bytes=_vmem_limit([6 * _nbytes((rows, dk), F32)], 40 * _nbytes((rows, dk), F32))),
        name="hgrn2",
    )(src, z, src, src, lb.reshape(1, -1).astype(F32), out_g.reshape(1, dv).astype(F32))


def _mem_attn_kernel(q_ref, k_ref, v_ref, o_ref):
    s = lax.dot_general(q_ref[...], k_ref[...], (((1,), (1,)), ((), ())), preferred_element_type=F32)
    m = jnp.max(s, axis=1, keepdims=True)
    p = jnp.exp(s - m)
    l = jnp.sum(p, axis=1, keepdims=True)
    o = jnp.dot(p.astype(BF16), v_ref[...], preferred_element_type=F32)
    o_ref[...] = (o / l).astype(o_ref.dtype)


def mem_attention(q, mk, mv, batch, heads, head_dim, tq=1024):
    m = q.shape[0]
    s = m // batch
    n_mem = mk.shape[0] // batch
    tq = _tile(s, tq)
    nq = s // tq
    return pl.pallas_call(
        _mem_attn_kernel,
        grid=(batch, heads, nq),
        in_specs=[pl.BlockSpec((tq, head_dim), lambda b, h, i: (b * nq + i, h)),
                  pl.BlockSpec((n_mem, head_dim), lambda b, h, i: (b, h)),
                  pl.BlockSpec((n_mem, head_dim), lambda b, h, i: (b, h))],
        out_specs=pl.BlockSpec((tq, head_dim), lambda b, h, i: (b * nq + i, h)),
        out_shape=jax.ShapeDtypeStruct((m, heads * head_dim), BF16),
        compiler_params=pltpu.CompilerParams(
            dimension_semantics=("parallel", "parallel", "parallel"),
            vmem_limit_bytes=_vmem_limit([2 * _nbytes((tq, head_dim), BF16), 2 * _nbytes((n_mem, head_dim), BF16)],
                                         4 * _nbytes((tq, head_dim), F32))),
        name="mem_attention",
    )(q, mk, mv)


def _merge_kernel(a_ref, b_ref, c_ref, wa_ref, wb_ref, wc_ref, ga_ref, gb_ref, gc_ref, o_ref):
    acc = ga_ref[...].astype(F32) * jnp.dot(a_ref[...], wa_ref[...], preferred_element_type=F32)
    acc = acc + gb_ref[...].astype(F32) * jnp.dot(b_ref[...], wb_ref[...], preferred_element_type=F32)
    acc = acc + gc_ref[...].astype(F32) * jnp.dot(c_ref[...], wc_ref[...], preferred_element_type=F32)
    o_ref[...] = acc.astype(o_ref.dtype)


def gated_merge(o_a, o_b, o_c, w_a, w_b, w_c, gates, tm=1024, tn=512):
    m = o_a.shape[0]
    d = w_a.shape[1]
    tm, tn = _tile(m, tm), _tile(d, tn)
    nj = d // tn
    a_spec = lambda arr: pl.BlockSpec((tm, arr.shape[1]), lambda i, j: (i, 0))
    w_spec = lambda arr: pl.BlockSpec((arr.shape[0], tn), lambda i, j: (0, j))
    g_spec = lambda r: pl.BlockSpec((tm, tn), lambda i, j: (i, r * nj + j))
    blocks = ([_nbytes((tm, a.shape[1]), BF16) for a in (o_a, o_b, o_c)]
              + [_nbytes((w.shape[0], tn), BF16) for w in (w_a, w_b, w_c)] + 4 * [_nbytes((tm, tn), BF16)])
    return pl.pallas_call(
        _merge_kernel,
        grid=(m // tm, nj),
        in_specs=[a_spec(o_a), a_spec(o_b), a_spec(o_c), w_spec(w_a), w_spec(w_b), w_spec(w_c),
                  g_spec(0), g_spec(1), g_spec(2)],
        out_specs=pl.BlockSpec((tm, tn), lambda i, j: (i, j)),
        out_shape=jax.ShapeDtypeStruct((m, d), BF16),
        compiler_params=pltpu.CompilerParams(
            dimension_semantics=("parallel", "parallel"),
            vmem_limit_bytes=_vmem_limit(blocks, 3 * _nbytes((tm, tn), F32))),
        name="gated_merge",
    )(o_a, o_b, o_c, w_a, w_b, w_c, gates, gates, gates)


def _router_kernel(h_ref, g_ref, w_ref, b_ref, lg_ref, id_ref, wt_ref, *, n_groups):
    x = h_ref[...]
    hn = x * lax.rsqrt(jnp.mean(x * x, axis=-1, keepdims=True) + EPS) * g_ref[...]
    logits = jnp.dot(hn, w_ref[...], preferred_element_type=F32,
                     precision=lax.Precision.HIGHEST) + b_ref[...]
    lane = lax.broadcasted_iota(jnp.int32, logits.shape, 1).astype(F32)
    lane_grp = lg_ref[...]
    neg = -jnp.inf
    big = float(LANES)

    gl = jnp.where(lane < n_groups, logits, neg)
    gmax = jnp.max(gl, axis=1, keepdims=True)
    grp = jnp.min(jnp.where(gl == gmax, lane, big), axis=1, keepdims=True)
    p_grp = 1.0 / jnp.sum(jnp.exp(gl - gmax), axis=1, keepdims=True)

    el = jnp.where(lane_grp == grp, logits, neg)
    v1 = jnp.max(el, axis=1, keepdims=True)
    i1 = jnp.min(jnp.where(el == v1, lane, big), axis=1, keepdims=True)
    el2 = jnp.where(lane == i1, neg, el)
    v2 = jnp.max(el2, axis=1, keepdims=True)
    i2 = jnp.min(jnp.where(el2 == v2, lane, big), axis=1, keepdims=True)
    e21 = jnp.exp(v2 - v1)
    w1 = p_grp / (1.0 + e21)
    w2 = p_grp * e21 / (1.0 + e21)

    id_ref[...] = jnp.where(lane == 0.0, i1 - n_groups, i2 - n_groups).astype(jnp.int32)
    wt_ref[...] = jnp.where(lane == 0.0, w1, w2)


def moe_router(h, norm_g, w_group, b_group, w_router, b_router, tm=256):
    n, d = h.shape
    n_groups = w_group.shape[1]
    n_experts = w_router.shape[1]
    per_group = n_experts // n_groups
    assert n_groups + n_experts <= LANES
    pad = LANES - n_groups - n_experts
    w = jnp.concatenate([w_group, w_router, jnp.zeros((d, pad), F32)], axis=1).astype(F32)
    bias = jnp.concatenate([b_group, b_router, jnp.zeros((pad,), F32)]).reshape(1, LANES).astype(F32)
    lane = jnp.arange(LANES)
    lane_grp = jnp.where((lane >= n_groups) & (lane < n_groups + n_experts),
                         (lane - n_groups) // per_group, -1).astype(F32).reshape(1, LANES)
    tm = _tile(n, tm)
    ids, wts = pl.pallas_call(
        functools.partial(_router_kernel, n_groups=n_groups),
        grid=(n // tm,),
        in_specs=[pl.BlockSpec((tm, d), lambda i: (i, 0)),
                  pl.BlockSpec((1, d), lambda i: (0, 0)),
                  pl.BlockSpec((d, LANES), lambda i: (0, 0)),
                  pl.BlockSpec((1, LANES), lambda i: (0, 0)),
                  pl.BlockSpec((1, LANES), lambda i: (0, 0))],
        out_specs=[pl.BlockSpec((tm, LANES), lambda i: (i, 0)),
                   pl.BlockSpec((tm, LANES), lambda i: (i, 0))],
        out_shape=[jax.ShapeDtypeStruct((n, LANES), jnp.int32),
                   jax.ShapeDtypeStruct((n, LANES), F32)],
        compiler_params=pltpu.CompilerParams(
            dimension_semantics=("parallel",),
            vmem_limit_bytes=_vmem_limit([_nbytes((tm, d), F32), _nbytes((d, LANES), F32)],
                                         8 * _nbytes((tm, d), F32))),
        name="moe_router",
    )(h, norm_g.reshape(1, d).astype(F32), w, bias, lane_grp)
    return ids[:, :TOP_K], wts[:, :TOP_K]


def _row_copy(src_hbm, row, dst, slot, sem):
    return pltpu.make_async_copy(src_hbm.at[pl.ds(row, 1)], dst.at[pl.ds(slot, 1)], sem)


def _moe_expert_kernel(be_ref, nused_ref, tok_ref, h_hbm, g_ref, wg_ref, wu_ref, wd_ref, y_ref,
                       x_buf, sem):
    del be_ref
    i = pl.program_id(0)
    rows = x_buf.shape[0]

    @pl.when(i < nused_ref[0])
    def _():
        def start(r, c):
            _row_copy(h_hbm, tok_ref[0, 0, r], x_buf, r, sem).start()
            return c

        def wait(r, c):
            _row_copy(h_hbm, tok_ref[0, 0, r], x_buf, r, sem).wait()
            return c

        lax.fori_loop(0, rows, start, 0)
        lax.fori_loop(0, rows, wait, 0)
        x = x_buf[...]
        x = (x * lax.rsqrt(jnp.mean(x * x, axis=-1, keepdims=True) + EPS) * g_ref[...]).astype(BF16)
        a = jnp.dot(x, wg_ref[0], preferred_element_type=F32)
        u = jnp.dot(x, wu_ref[0], preferred_element_type=F32)
        hmid = (a * jax.nn.sigmoid(a) * u).astype(BF16)
        y_ref[...] = jnp.dot(hmid, wd_ref[0], preferred_element_type=F32)

    @pl.when(i >= nused_ref[0])
    def _():
        y_ref[...] = jnp.zeros(y_ref.shape, y_ref.dtype)


def moe_experts(h, norm_g, tok_buf, blk_e, n_used, w_gate, w_up, w_down):
    n, d = h.shape
    n_blocks = blk_e.shape[0]
    de = w_gate.shape[2]
    grid_spec = pltpu.PrefetchScalarGridSpec(
        num_scalar_prefetch=2,
        grid=(n_blocks,),
        in_specs=[pl.BlockSpec((1, 1, MOE_BLOCK), lambda i, be, nu: (i, 0, 0), memory_space=pltpu.SMEM),
                  pl.BlockSpec(memory_space=pl.ANY),
                  pl.BlockSpec((1, d), lambda i, be, nu: (0, 0)),
                  pl.BlockSpec((1, d, de), lambda i, be, nu: (be[i], 0, 0)),
                  pl.BlockSpec((1, d, de), lambda i, be, nu: (be[i], 0, 0)),
                  pl.BlockSpec((1, de, d), lambda i, be, nu: (be[i], 0, 0))],
        out_specs=pl.BlockSpec((MOE_BLOCK, d), lambda i, be, nu: (i, 0)),
        scratch_shapes=[pltpu.VMEM((MOE_BLOCK, d), F32), pltpu.SemaphoreType.DMA(())],
    )
    return pl.pallas_call(
        _moe_expert_kernel,
        grid_spec=grid_spec,
        out_shape=jax.ShapeDtypeStruct((n_blocks * MOE_BLOCK, d), F32),
        compiler_params=pltpu.CompilerParams(
            dimension_semantics=("arbitrary",),
            vmem_limit_bytes=_vmem_limit([3 * _nbytes((d, de), BF16), _nbytes((MOE_BLOCK, d), F32)],
                                         4 * _nbytes((MOE_BLOCK, d), F32))),
        name="moe_experts",
    )(blk_e, n_used, tok_buf.reshape(n_blocks, 1, MOE_BLOCK), h, norm_g.reshape(1, d).astype(F32),
      w_gate, w_up, w_down)


def _combine_kernel(dest_ref, h_ref, w_ref, y_hbm, o_ref, buf0, buf1, sem):
    rows = buf0.shape[0]

    def start(r, c):
        _row_copy(y_hbm, dest_ref[0, 0, TOP_K * r], buf0, r, sem).start()
        _row_copy(y_hbm, dest_ref[0, 0, TOP_K * r + 1], buf1, r, sem).start()
        return c

    def wait(r, c):
        _row_copy(y_hbm, dest_ref[0, 0, TOP_K * r], buf0, r, sem).wait()
        _row_copy(y_hbm, dest_ref[0, 0, TOP_K * r + 1], buf1, r, sem).wait()
        return c

    lax.fori_loop(0, rows, start, 0)
    lax.fori_loop(0, rows, wait, 0)
    w = w_ref[...]
    o_ref[...] = h_ref[...] + (buf0[...] * w[:, 0:1] + buf1[...] * w[:, 1:2])


def moe_combine(h, y_buf, dest, weights, tm=128):
    n, d = h.shape
    tm = _tile(n, tm)
    w_pad = jnp.zeros((n, LANES), F32).at[:, :TOP_K].set(weights)
    return pl.pallas_call(
        _combine_kernel,
        grid=(n // tm,),
        in_specs=[pl.BlockSpec((1, 1, TOP_K * tm), lambda i: (i, 0, 0), memory_space=pltpu.SMEM),
                  pl.BlockSpec((tm, d), lambda i: (i, 0)),
                  pl.BlockSpec((tm, LANES), lambda i: (i, 0)),
                  pl.BlockSpec(memory_space=pl.ANY)],
        out_specs=pl.BlockSpec((tm, d), lambda i: (i, 0)),
        out_shape=jax.ShapeDtypeStruct((n, d), F32),
        scratch_shapes=[pltpu.VMEM((tm, d), F32), pltpu.VMEM((tm, d), F32),
                        pltpu.SemaphoreType.DMA(())],
        compiler_params=pltpu.CompilerParams(
            dimension_semantics=("arbitrary",),
            vmem_limit_bytes=_vmem_limit([2 * _nbytes((tm, d), F32)], 4 * _nbytes((tm, d), F32))),
        name="moe_combine",
    )(dest.reshape(n // tm, 1, TOP_K * tm), h, w_pad, y_buf)


def _moe_dispatch(expert_id, n_experts):
    n = expert_id.shape[0]
    nk = n * TOP_K
    flat_e = expert_id.reshape(nk).astype(jnp.int32)
    flat_tok = jnp.repeat(jnp.arange(n, dtype=jnp.int32), TOP_K)
    order = jnp.argsort(flat_e)
    sorted_e = flat_e[order]
    counts = jnp.bincount(flat_e, length=n_experts)
    start = jnp.cumsum(counts) - counts
    padded = (counts + MOE_BLOCK - 1) // MOE_BLOCK * MOE_BLOCK
    pad_end = jnp.cumsum(padded)
    pad_start = pad_end - padded
    dest_sorted = pad_start[sorted_e] + (jnp.arange(nk) - start[sorted_e])
    n_blocks = (nk + n_experts * (MOE_BLOCK - 1) + MOE_BLOCK - 1) // MOE_BLOCK
    cap = n_blocks * MOE_BLOCK
    tok_buf = jnp.zeros((cap,), jnp.int32).at[dest_sorted].set(flat_tok[order])
    dest = jnp.zeros((nk,), jnp.int32).at[order].set(dest_sorted.astype(jnp.int32))
    blk_start = jnp.arange(n_blocks) * MOE_BLOCK
    blk_e = jnp.minimum(jnp.sum(pad_end[None, :] <= blk_start[:, None], axis=1), n_experts - 1)
    n_used = (pad_end[-1] // MOE_BLOCK).astype(jnp.int32).reshape(1)
    return tok_buf, blk_e.astype(jnp.int32), n_used, dest


def kernel(x, mem, norm1_g, norm2_g, mem_norm_g, w_in, b_gate, fox_f_bias, fox_q_g, fox_k_g,
           hgrn_lb_logits, hgrn_out_g, mem_q_g, mem_k_g, w_mem_kv, w_branch_fox, w_branch_hgrn,
           w_branch_mem, w_out, w_group, b_group, w_router, b_router, w_exp_gate, w_exp_up,
           w_exp_down):
    batch, seq, d = x.shape
    n_mem = mem.shape[1]
    depth = w_in.shape[0]
    fox_heads, fox_hd = fox_f_bias.shape[1], fox_q_g.shape[1]
    fox_w = fox_heads * fox_hd
    hgrn_w = hgrn_lb_logits.shape[1]
    hgrn_heads = hgrn_w // HGRN_EXPAND
    hgrn_dv = hgrn_out_g.shape[1]
    mem_hd = mem_q_g.shape[1]
    mem_w = w_branch_mem.shape[1]
    mem_heads = mem_w // mem_hd
    n_experts = w_router.shape[2]
    m = batch * seq

    sizes = (fox_w, fox_w, fox_w, fox_heads, hgrn_w, hgrn_w, hgrn_w, hgrn_w, mem_w, 3 * d)
    offs = [0]
    for sz in sizes:
        offs.append(offs[-1] + sz)
    (o_fq, o_fk, o_fv, o_ff, o_hq, o_hf, o_hi, o_hg, o_mq, o_gt, o_end) = offs

    lb_all = jnp.cumsum(jax.nn.softmax(hgrn_lb_logits.astype(F32), axis=0), axis=0)
    mem_n = rmsnorm_rows(mem.reshape(batch * n_mem, d), mem_norm_g, BF16)

    h = x.reshape(m, d)
    for layer in range(depth):
        wl = w_in[layer]
        cols = lambda a, b: wl[:, a:b].astype(BF16)
        hn = rmsnorm_rows(h, norm1_g[layer], BF16)

        qk_gain = jnp.concatenate([jnp.tile(fox_q_g[layer] * fox_hd ** -0.5, fox_heads),
                                   jnp.tile(fox_k_g[layer], fox_heads)])
        qk = matmul(hn, cols(o_fq, o_fv), BF16, functools.partial(_epi_headnorm, head_dim=fox_hd),
                    row_vec=qk_gain, name="proj_fox_qk")
        w_plain = jnp.concatenate([wl[:, o_fv:o_ff], wl[:, o_hq:o_hf], wl[:, o_hi:o_mq]], axis=1).astype(BF16)
        plain = matmul(hn, w_plain, BF16, name="proj_plain")
        hd_blk = lambda width: width // HGRN_EXPAND
        c_fv, c_hq = 0, hd_blk(fox_w)
        c_hi = c_hq + hd_blk(hgrn_w)
        c_hg = c_hi + hd_blk(hgrn_w)
        z = matmul(hn, cols(o_hf, o_hi), F32, name="proj_hgrn_f")
        w_ff = jnp.pad(wl[:, o_ff:o_hq], ((0, 0), (0, LANES - fox_heads))).astype(BF16)
        ff = matmul(hn, w_ff, F32, tn=LANES, name="proj_fox_f")
        mq = matmul(hn, cols(o_mq, o_gt), BF16, functools.partial(_epi_headnorm, head_dim=mem_hd),
                    row_vec=jnp.tile(mem_q_g[layer] * mem_hd ** -0.5, mem_heads), tn=mem_hd,
                    name="proj_mem_q")
        gates = matmul(hn, cols(o_gt, o_end), BF16, _epi_sigmoid_bias, row_vec=b_gate[layer],
                       name="proj_gates")

        f_t = ff[:, :fox_heads].reshape(batch, seq, fox_heads).transpose(0, 2, 1).reshape(batch * fox_heads, seq)
        negc = fox_neg_cumsum(f_t, jnp.tile(fox_f_bias[layer], batch))
        o_a = fox_attention(qk, plain, c_fv, negc, batch, fox_heads, fox_hd)

        o_b = hgrn2(plain, c_hq, c_hi, c_hg, z, lb_all[layer], hgrn_out_g[layer], batch, hgrn_heads)

        wkv = w_mem_kv[layer]
        mk = matmul(mem_n, wkv[:, :mem_w].astype(BF16), BF16,
                    functools.partial(_epi_headnorm, head_dim=mem_hd),
                    row_vec=jnp.tile(mem_k_g[layer], mem_heads), tn=mem_hd, name="proj_mem_k")
        mv = matmul(mem_n, wkv[:, mem_w:].astype(BF16), BF16, name="proj_mem_v")
        o_c = mem_attention(mq, mk, mv, batch, mem_heads, mem_hd)

        merged = gated_merge(o_a, o_b, o_c, w_branch_fox[layer].astype(BF16),
                             w_branch_hgrn[layer].astype(BF16), w_branch_mem[layer].astype(BF16), gates)
        h = matmul(merged, w_out[layer].astype(BF16), F32, _epi_residual, residual=h, tn=512,
                   name="proj_out")

        expert_id, weights = moe_router(h, norm2_g[layer], w_group[layer], b_group[layer],
                                        w_router[layer], b_router[layer])
        tok_buf, blk_e, n_used, dest = _moe_dispatch(expert_id, n_experts)
        y_buf = moe_experts(h, norm2_g[layer], tok_buf, blk_e, n_used, w_exp_gate[layer].astype(BF16),
                            w_exp_up[layer].astype(BF16), w_exp_down[layer].astype(BF16))
        h = moe_combine(h, y_buf, dest, weights)
    return h.reshape(batch, seq, d)
```

```python
import functools

import jax
import jax.numpy as jnp
from jax import lax
from jax.experimental import pallas as pl
from jax.experimental.pallas import tpu as pltpu

F32 = jnp.float32
BF16 = jnp.bfloat16

EPS = 1e-6
CHUNK = 64
SUB = 16
HGRN_EXPAND = 128
TOP_K = 2
MOE_BLOCK = 128
FOX_BLOCK = 512
LANES = 128
LOG2E = 1.4426950408889634
V7X_VMEM_LIMIT_CAP = 60000 * 1024


def _vmem_limit(block_bytes, temp_bytes=0):
    need = int(1.25 * (2 * sum(block_bytes) + temp_bytes)) + (2 << 20)
    return min(max(need, 16 << 20), V7X_VMEM_LIMIT_CAP)


def _nbytes(shape, dtype):
    n = 1
    for s in shape:
        n *= s
    return n * jnp.dtype(dtype).itemsize


def _tile(dim, want):
    if dim <= want:
        return dim
    for t in range(want - want % LANES, 0, -LANES):
        if dim % t == 0:
            return t
    raise ValueError((dim, want))


def _rmsnorm_kernel(x_ref, g_ref, o_ref):
    x = x_ref[...].astype(F32)
    y = x * lax.rsqrt(jnp.mean(x * x, axis=-1, keepdims=True) + EPS)
    o_ref[...] = (y * g_ref[...]).astype(o_ref.dtype)


def rmsnorm_rows(x, g, out_dtype, tm=256):
    m, d = x.shape
    tm = _tile(m, tm)
    return pl.pallas_call(
        _rmsnorm_kernel,
        grid=(m // tm,),
        in_specs=[pl.BlockSpec((tm, d), lambda i: (i, 0)),
                  pl.BlockSpec((1, d), lambda i: (0, 0))],
        out_specs=pl.BlockSpec((tm, d), lambda i: (i, 0)),
        out_shape=jax.ShapeDtypeStruct((m, d), out_dtype),
        compiler_params=pltpu.CompilerParams(
            dimension_semantics=("parallel",),
            vmem_limit_bytes=_vmem_limit([_nbytes((tm, d), x.dtype), _nbytes((tm, d), out_dtype)],
                                         2 * _nbytes((tm, d), F32))),
        name="rmsnorm_rows",
    )(x, g.reshape(1, d).astype(F32))


def _mm_kernel(a_ref, w_ref, *refs, epilogue):
    o_ref = refs[-1]
    acc = jnp.dot(a_ref[...], w_ref[...], preferred_element_type=F32)
    epilogue(acc, o_ref, *refs[:-1])


def _epi_plain(acc, o_ref):
    o_ref[...] = acc.astype(o_ref.dtype)


def _epi_headnorm(acc, o_ref, g_ref, *, head_dim):
    for h in range(acc.shape[1] // head_dim):
        sl = slice(h * head_dim, (h + 1) * head_dim)
        blk = acc[:, sl]
        y = blk * lax.rsqrt(jnp.mean(blk * blk, axis=-1, keepdims=True) + EPS)
        o_ref[:, sl] = (y * g_ref[:, sl]).astype(o_ref.dtype)


def _epi_sigmoid_bias(acc, o_ref, b_ref):
    o_ref[...] = jax.nn.sigmoid(acc + b_ref[...]).astype(o_ref.dtype)


def _epi_residual(acc, o_ref, r_ref):
    o_ref[...] = (r_ref[...] + acc).astype(o_ref.dtype)


def matmul(a, w, out_dtype, epilogue=_epi_plain, row_vec=None, residual=None, tm=1024, tn=1024,
           name="matmul"):
    m, k = a.shape
    n = w.shape[1]
    tm, tn = _tile(m, tm), _tile(n, tn)
    in_specs = [pl.BlockSpec((tm, k), lambda i, j: (i, 0)),
                pl.BlockSpec((k, tn), lambda i, j: (0, j))]
    operands = [a, w]
    blocks = [_nbytes((tm, k), a.dtype), _nbytes((k, tn), w.dtype), _nbytes((tm, tn), out_dtype)]
    if row_vec is not None:
        in_specs.append(pl.BlockSpec((1, tn), lambda i, j: (0, j)))
        operands.append(row_vec.reshape(1, n).astype(F32))
    if residual is not None:
        in_specs.append(pl.BlockSpec((tm, tn), lambda i, j: (i, j)))
        operands.append(residual)
        blocks.append(_nbytes((tm, tn), residual.dtype))
    return pl.pallas_call(
        functools.partial(_mm_kernel, epilogue=epilogue),
        grid=(m // tm, n // tn),
        in_specs=in_specs,
        out_specs=pl.BlockSpec((tm, tn), lambda i, j: (i, j)),
        out_shape=jax.ShapeDtypeStruct((m, n), out_dtype),
        compiler_params=pltpu.CompilerParams(
            dimension_semantics=("parallel", "parallel"),
            vmem_limit_bytes=_vmem_limit(blocks, 2 * _nbytes((tm, tn), F32))),
        name=name,
    )(*operands)


def _mm_t_kernel(wt_ref, a_ref, o_ref):
    o_ref[0] = lax.dot_general(wt_ref[...], a_ref[...], (((1,), (1,)), ((), ())),
                               preferred_element_type=F32).astype(o_ref.dtype)


def matmul_t(a, w_t, out_dtype, tm=512, tn=1024, name="matmul_t"):
    m, k = a.shape
    n = w_t.shape[0]
    tm, tn = _tile(m, tm), _tile(n, tn)
    return pl.pallas_call(
        _mm_t_kernel,
        grid=(m // tm, n // tn),
        in_specs=[pl.BlockSpec((tn, k), lambda i, j: (j, 0)),
                  pl.BlockSpec((tm, k), lambda i, j: (i, 0))],
        out_specs=pl.BlockSpec((1, tn, tm), lambda i, j: (i, j, 0)),
        out_shape=jax.ShapeDtypeStruct((m // tm, n, tm), out_dtype),
        compiler_params=pltpu.CompilerParams(
            dimension_semantics=("parallel", "parallel"),
            vmem_limit_bytes=_vmem_limit([_nbytes((tm, k), a.dtype), _nbytes((tn, k), w_t.dtype),
                                          _nbytes((tn, tm), out_dtype)], 2 * _nbytes((tn, tm), F32))),
        name=name,
    )(w_t, a)


N_SPLIT = 3


def _fox_bias_kernel(f_ref, b_ref, e_ref, o_ref, carry):
    @pl.when(pl.program_id(1) == 0)
    def _():
        carry[...] = jnp.zeros(carry.shape, F32)

    x = f_ref[...] + b_ref[...]
    a = jnp.minimum(x, 0.0) - jnp.log(1.0 + jnp.exp(-jnp.abs(x)))
    r = a.shape[0]
    row = lax.broadcasted_iota(jnp.int32, a.shape, 0)
    sh = 1
    while sh < r:
        a = a + jnp.where(row >= sh, pltpu.roll(a, sh, axis=0), 0.0)
        sh *= 2
    c = a + carry[...]
    carry[...] = c[r - 1:r, :]
    rest = c * (-LOG2E)
    pieces = []
    for _ in range(N_SPLIT):
        piece = rest.astype(BF16)
        pieces.append(piece)
        rest = rest - piece.astype(F32)
    o_ref[...] = jnp.dot(jnp.concatenate(pieces, axis=1), e_ref[...],
                         preferred_element_type=F32).astype(o_ref.dtype)


def fox_key_bias(ff, bias, batch, heads, head_dim, rows=1024):
    m = ff.shape[0]
    s = m // batch
    rows = _tile(s, rows)
    nr = s // rows
    src = jnp.arange(N_SPLIT * LANES)
    dst = (src % LANES) * head_dim + src // LANES
    place = ((jnp.arange(heads * head_dim)[None, :] == dst[:, None])
             & ((src % LANES) < heads)[:, None]).astype(BF16)
    bias_row = jnp.zeros((1, LANES), F32).at[0, :heads].set(bias.astype(F32))
    return pl.pallas_call(
        _fox_bias_kernel,
        grid=(batch, nr),
        in_specs=[pl.BlockSpec((rows, LANES), lambda b, t: (b * nr + t, 0)),
                  pl.BlockSpec((1, LANES), lambda b, t: (0, 0)),
                  pl.BlockSpec((N_SPLIT * LANES, heads * head_dim), lambda b, t: (0, 0))],
        out_specs=pl.BlockSpec((rows, heads * head_dim), lambda b, t: (b * nr + t, 0)),
        out_shape=jax.ShapeDtypeStruct((m, heads * head_dim), BF16),
        scratch_shapes=[pltpu.VMEM((1, LANES), F32)],
        compiler_params=pltpu.CompilerParams(
            dimension_semantics=("parallel", "arbitrary"),
            vmem_limit_bytes=_vmem_limit([_nbytes((rows, heads * head_dim), BF16),
                                          _nbytes((N_SPLIT * LANES, heads * head_dim), BF16)],
                                         _nbytes((rows, heads * head_dim), F32) + 16 * _nbytes((rows, LANES), F32))),
        name="fox_key_bias",
    )(ff, bias_row, place)


def _fox_attn_kernel(q_ref, k_ref, kb_ref, vt_ref, o_ref, s_a, s_b, m_scr, l_scr, acc_scr):
    i = pl.program_id(2)
    tq, hd = q_ref.shape
    tk = tq

    q_t = q_ref[...].astype(F32).T.astype(BF16)
    ones_rows = jnp.where(lax.broadcasted_iota(jnp.int32, (hd, tq), 0) < N_SPLIT, 1.0, 0.0).astype(BF16)
    q_aug = jnp.concatenate([q_t, ones_rows], axis=0)

    def scores(j):
        off = pl.multiple_of(j * tk, tk)
        k_aug = jnp.concatenate([k_ref[pl.ds(off, tk), :], kb_ref[pl.ds(off, tk), :]], axis=1)
        return jnp.dot(k_aug, q_aug, preferred_element_type=F32)

    def update(s, j, masked):
        if masked:
            key = lax.broadcasted_iota(jnp.int32, (tk, tq), 0)
            qry = lax.broadcasted_iota(jnp.int32, (tk, tq), 1)
            s = jnp.where(key <= qry, s, -jnp.inf)
        m_prev = m_scr[...]
        m_new = jnp.maximum(m_prev, jnp.max(s, axis=0, keepdims=True))
        alpha = jnp.exp2(m_prev - m_new)
        p = jnp.exp2(s - m_new)
        l_scr[...] = alpha * l_scr[...] + jnp.sum(p, axis=0, keepdims=True)
        acc_scr[...] = alpha * acc_scr[...] + jnp.dot(vt_ref[0, j], p.astype(BF16),
                                                      preferred_element_type=F32)
        m_scr[...] = m_new

    m_scr[...] = jnp.full(m_scr.shape, -jnp.inf, F32)
    l_scr[...] = jnp.zeros(l_scr.shape, F32)
    acc_scr[...] = jnp.zeros(acc_scr.shape, F32)
    s_a[...] = scores(0)

    def pair(t, carry):
        j0 = 2 * t
        s_b[...] = scores(j0 + 1)
        update(s_a[...], j0, False)
        s_a[...] = scores(j0 + 2)
        update(s_b[...], j0 + 1, False)
        return carry

    lax.fori_loop(0, i // 2, pair, 0)

    @pl.when(i % 2 == 0)
    def _():
        update(s_a[...], i, True)

    @pl.when(i % 2 == 1)
    def _():
        s_b[...] = scores(i)
        update(s_a[...], i - 1, False)
        update(s_b[...], i, True)

    o_ref[...] = (acc_scr[...] / l_scr[...]).T.astype(o_ref.dtype)


def fox_attention(qk, kbias, v_t, batch, heads, head_dim):
    m = qk.shape[0]
    s = m // batch
    tq = v_t.shape[2]
    nq = s // tq
    return pl.pallas_call(
        _fox_attn_kernel,
        grid=(batch, heads, nq),
        in_specs=[pl.BlockSpec((tq, head_dim), lambda b, h, i: (b * nq + i, h)),
                  pl.BlockSpec((s, head_dim), lambda b, h, i: (b, heads + h)),
                  pl.BlockSpec((s, head_dim), lambda b, h, i: (b, h)),
                  pl.BlockSpec((1, nq, head_dim, tq), lambda b, h, i: (b, 0, h, 0))],
        out_specs=pl.BlockSpec((tq, head_dim), lambda b, h, i: (b * nq + i, h)),
        out_shape=jax.ShapeDtypeStruct((m, heads * head_dim), BF16),
        scratch_shapes=[pltpu.VMEM((tq, tq), F32), pltpu.VMEM((tq, tq), F32),
                        pltpu.VMEM((1, tq), F32), pltpu.VMEM((1, tq), F32),
                        pltpu.VMEM((head_dim, tq), F32)],
        compiler_params=pltpu.CompilerParams(
            dimension_semantics=("parallel", "parallel", "arbitrary"),
            vmem_limit_bytes=_vmem_limit([3 * _nbytes((s, head_dim), BF16), 2 * _nbytes((tq, head_dim), BF16)],
                                         8 * _nbytes((tq, tq), F32))),
        name="fox_attention",
    )(qk, qk, kbias, v_t.reshape(batch, nq, heads * head_dim, tq))


def _hgrn_kernel(q_ref, z_ref, v_ref, hg_ref, lb_ref, og_ref, o_ref, st_ref):
    @pl.when(pl.program_id(2) == 0)
    def _():
        st_ref[...] = jnp.zeros(st_ref.shape, F32)

    t, dk = z_ref.shape
    dv = v_ref.shape[1]
    nc, nb, nsub = t // CHUNK, t // SUB, CHUNK // SUB

    q = q_ref[...].astype(F32)
    z = z_ref[...]
    lb = lb_ref[...]
    g = jnp.log(lb + (1.0 - lb) * jax.nn.sigmoid(z))
    kk = (1.0 - lb) * jax.nn.sigmoid(-z)

    row = lax.broadcasted_iota(jnp.int32, (t, dk), 0)
    rin = row % CHUNK
    b = g
    sh = 1
    while sh < CHUNK:
        b = b + jnp.where(rin >= sh, pltpu.roll(b, sh, axis=0), 0.0)
        sh *= 2

    b3 = b.reshape(nc, CHUNK, dk)
    q3 = q.reshape(nc, CHUNK, dk)
    k3 = kk.reshape(nc, CHUNK, dk)
    v3 = v_ref[...].reshape(nc, CHUNK, dv)
    b_last = b3[:, CHUNK - 1:CHUNK, :]
    q_in = (q3 * jnp.exp(b3)).astype(BF16)
    k_out = (k3 * jnp.exp(b_last - b3)).astype(BF16)
    d_last = jnp.exp(b_last)

    st = st_ref[...]
    inter = []
    for c in range(nc):
        inter.append(lax.dot_general(q_in[c], st.astype(BF16), (((1,), (1,)), ((), ())),
                                     preferred_element_type=F32))
        kv_t = lax.dot_general(v3[c], k_out[c], (((0,), (0,)), ((), ())),
                               preferred_element_type=F32)
        st = st * d_last[c] + kv_t
    st_ref[...] = st
    inter = jnp.concatenate(inter, axis=0)

    rsub_c = (rin // SUB).reshape(nc, CHUNK, dk)
    q_parts, k_parts = [], []
    for blk in range(1, nsub):
        b_ref_pt = b3[:, blk * SUB - 1:blk * SUB, :]
        q_hat = q3 * jnp.exp(jnp.minimum(b3 - b_ref_pt, 0.0))
        q_parts.append(jnp.where(rsub_c == blk, q_hat, 0.0).astype(BF16))
        k_parts.append((k3 * jnp.exp(jnp.minimum(b_ref_pt - b3, 0.0))).astype(BF16))
    q_cat = jnp.concatenate(q_parts, axis=2)
    k_cat = jnp.concatenate(k_parts, axis=2)
    s_off = lax.dot_general(q_cat, k_cat, (((2,), (2,)), ((0,), (0,))),
                            preferred_element_type=F32)

    b4 = b.reshape(nb, SUB, dk)
    q4 = q.reshape(nb, SUB, dk)
    k4 = kk.reshape(nb, SUB, dk)
    rsub = (row % SUB).reshape(nb, SUB, dk)
    u_row = lax.broadcasted_iota(jnp.int32, (dk, LANES), 1)
    half = SUB // 2
    s_lo = jnp.zeros((t, LANES), F32)
    s_hi = jnp.zeros((t // 2, LANES), F32)
    for j in range(SUB):
        r0 = 0 if j < half else half
        bj, kj = b4[:, j:j + 1, :], k4[:, j:j + 1, :]
        decay = jnp.exp(jnp.where(rsub[:, r0:, :] >= j, b4[:, r0:, :] - bj, -jnp.inf))
        p = (q4[:, r0:, :] * (kj * decay)).reshape(-1, dk).astype(BF16)
        place = jnp.where((u_row % SUB == j) & (u_row < CHUNK), 1.0, 0.0).astype(BF16)
        if j < half:
            s_lo = s_lo + jnp.dot(p, place, preferred_element_type=F32)
        else:
            s_hi = s_hi + jnp.dot(p, place, preferred_element_type=F32)
    s_diag = s_lo.reshape(nb, SUB, LANES) + jnp.concatenate(
        [jnp.zeros((nb, half, LANES), F32), s_hi.reshape(nb, half, LANES)], axis=1)
    s_diag = s_diag.reshape(nc, CHUNK, LANES)[:, :, :CHUNK]

    rb = lax.broadcasted_iota(jnp.int32, (nc, CHUNK, CHUNK), 1) // SUB
    cb = lax.broadcasted_iota(jnp.int32, (nc, CHUNK, CHUNK), 2) // SUB
    scores = jnp.where(cb == rb, s_diag, jnp.where(cb < rb, s_off, 0.0)).astype(BF16)
    intra = lax.dot_general(scores, v3, (((2,), (1,)), ((0,), (0,))),
                            preferred_element_type=F32).reshape(t, dv)

    o = inter + intra
    o = o * lax.rsqrt(jnp.mean(o * o, axis=-1, keepdims=True) + EPS) * og_ref[...]
    hg = hg_ref[...].astype(F32)
    o_ref[...] = (o * (hg * jax.nn.sigmoid(hg))).astype(o_ref.dtype)


def hgrn2(src, q_col0, v_col0, hg_col0, z, lb, out_g, batch, heads, rows=512):
    m = z.shape[0]
    s = m // batch
    dk = HGRN_EXPAND
    dv = out_g.shape[-1]
    rows = _tile(s, rows)
    nt = s // rows
    row_blk = lambda b, h, t: b * nt + t
    return pl.pallas_call(
        _hgrn_kernel,
        grid=(batch, heads, nt),
        in_specs=[pl.BlockSpec((rows, dk), lambda b, h, t: (row_blk(b, h, t), q_col0 + h)),
                  pl.BlockSpec((rows, dk), lambda b, h, t: (row_blk(b, h, t), h)),
                  pl.BlockSpec((rows, dv), lambda b, h, t: (row_blk(b, h, t), v_col0 + h)),
                  pl.BlockSpec((rows, dv), lambda b, h, t: (row_blk(b, h, t), hg_col0 + h)),
                  pl.BlockSpec((1, dk), lambda b, h, t: (0, h)),
                  pl.BlockSpec((1, dv), lambda b, h, t: (0, 0))],
        out_specs=pl.BlockSpec((rows, dv), lambda b, h, t: (row_blk(b, h, t), h)),
        out_shape=jax.ShapeDtypeStruct((m, heads * dv), BF16),
        scratch_shapes=[pltpu.VMEM((dv, dk), F32)],
        compiler_params=pltpu.CompilerParams(
            dimension_semantics=("parallel", "parallel", "arbitrary"),
            vmem_limit_bytes=_vmem_limit([6 * _nbytes((rows, dk), F32)], 40 * _nbytes((rows, dk), F32))),
        name="hgrn2",
    )(src, z, src, src, lb.reshape(1, -1).astype(F32), out_g.reshape(1, dv).astype(F32))


def _mem_attn_kernel(q_ref, k_ref, v_ref, o_ref):
    s = lax.dot_general(q_ref[...], k_ref[...], (((1,), (1,)), ((), ())), preferred_element_type=F32)
    m = jnp.max(s, axis=1, keepdims=True)
    p = jnp.exp(s - m)
    l = jnp.sum(p, axis=1, keepdims=True)
    o = jnp.dot(p.astype(BF16), v_ref[...], preferred_element_type=F32)
    o_ref[...] = (o / l).astype(o_ref.dtype)


def mem_attention(q, mk, mv, batch, heads, head_dim, tq=1024):
    m = q.shape[0]
    s = m // batch
    n_mem = mk.shape[0] // batch
    tq = _tile(s, tq)
    nq = s // tq
    return pl.pallas_call(
        _mem_attn_kernel,
        grid=(batch, heads, nq),
        in_specs=[pl.BlockSpec((tq, head_dim), lambda b, h, i: (b * nq + i, h)),
                  pl.BlockSpec((n_mem, head_dim), lambda b, h, i: (b, h)),
                  pl.BlockSpec((n_mem, head_dim), lambda b, h, i: (b, h))],
        out_specs=pl.BlockSpec((tq, head_dim), lambda b, h, i: (b * nq + i, h)),
        out_shape=jax.ShapeDtypeStruct((m, heads * head_dim), BF16),
        compiler_params=pltpu.CompilerParams(
            dimension_semantics=("parallel", "parallel", "parallel"),
            vmem_limit_bytes=_vmem_limit([2 * _nbytes((tq, head_dim), BF16), 2 * _nbytes((n_mem, head_dim), BF16)],
                                         4 * _nbytes((tq, head_dim), F32))),
        name="mem_attention",
    )(q, mk, mv)


def _merge_kernel(a_ref, b_ref, c_ref, wa_ref, wb_ref, wc_ref, ga_ref, gb_ref, gc_ref, o_ref):
    acc = ga_ref[...].astype(F32) * jnp.dot(a_ref[...], wa_ref[...], preferred_element_type=F32)
    acc = acc + gb_ref[...].astype(F32) * jnp.dot(b_ref[...], wb_ref[...], preferred_element_type=F32)
    acc = acc + gc_ref[...].astype(F32) * jnp.dot(c_ref[...], wc_ref[...], preferred_element_type=F32)
    o_ref[...] = acc.astype(o_ref.dtype)


def gated_merge(o_a, o_b, o_c, w_a, w_b, w_c, gates, tm=1024, tn=512):
    m = o_a.shape[0]
    d = w_a.shape[1]
    tm, tn = _tile(m, tm), _tile(d, tn)
    nj = d // tn
    a_spec = lambda arr: pl.BlockSpec((tm, arr.shape[1]), lambda i, j: (i, 0))
    w_spec = lambda arr: pl.BlockSpec((arr.shape[0], tn), lambda i, j: (0, j))
    g_spec = lambda r: pl.BlockSpec((tm, tn), lambda i, j: (i, r * nj + j))
    blocks = ([_nbytes((tm, a.shape[1]), BF16) for a in (o_a, o_b, o_c)]
              + [_nbytes((w.shape[0], tn), BF16) for w in (w_a, w_b, w_c)] + 4 * [_nbytes((tm, tn), BF16)])
    return pl.pallas_call(
        _merge_kernel,
        grid=(m // tm, nj),
        in_specs=[a_spec(o_a), a_spec(o_b), a_spec(o_c), w_spec(w_a), w_spec(w_b), w_spec(w_c),
                  g_spec(0), g_spec(1), g_spec(2)],
        out_specs=pl.BlockSpec((tm, tn), lambda i, j: (i, j)),
        out_shape=jax.ShapeDtypeStruct((m, d), BF16),
        compiler_params=pltpu.CompilerParams(
            dimension_semantics=("parallel", "parallel"),
            vmem_limit_bytes=_vmem_limit(blocks, 3 * _nbytes((tm, tn), F32))),
        name="gated_merge",
    )(o_a, o_b, o_c, w_a, w_b, w_c, gates, gates, gates)


def _router_kernel(h_ref, g_ref, w_ref, b_ref, lg_ref, id_ref, wt_ref, *, n_groups):
    x = h_ref[...]
    hn = x * lax.rsqrt(jnp.mean(x * x, axis=-1, keepdims=True) + EPS) * g_ref[...]
    logits = jnp.dot(hn, w_ref[...], preferred_element_type=F32,
                     precision=lax.Precision.HIGHEST) + b_ref[...]
    lane = lax.broadcasted_iota(jnp.int32, logits.shape, 1).astype(F32)
    lane_grp = lg_ref[...]
    neg = -jnp.inf
    big = float(LANES)

    gl = jnp.where(lane < n_groups, logits, neg)
    gmax = jnp.max(gl, axis=1, keepdims=True)
    grp = jnp.min(jnp.where(gl == gmax, lane, big), axis=1, keepdims=True)
    p_grp = 1.0 / jnp.sum(jnp.exp(gl - gmax), axis=1, keepdims=True)

    el = jnp.where(lane_grp == grp, logits, neg)
    v1 = jnp.max(el, axis=1, keepdims=True)
    i1 = jnp.min(jnp.where(el == v1, lane, big), axis=1, keepdims=True)
    el2 = jnp.where(lane == i1, neg, el)
    v2 = jnp.max(el2, axis=1, keepdims=True)
    i2 = jnp.min(jnp.where(el2 == v2, lane, big), axis=1, keepdims=True)
    e21 = jnp.exp(v2 - v1)
    w1 = p_grp / (1.0 + e21)
    w2 = p_grp * e21 / (1.0 + e21)

    id_ref[...] = jnp.where(lane == 0.0, i1 - n_groups, i2 - n_groups).astype(jnp.int32)
    wt_ref[...] = jnp.where(lane == 0.0, w1, w2)


def moe_router(h, norm_g, w_group, b_group, w_router, b_router, tm=256):
    n, d = h.shape
    n_groups = w_group.shape[1]
    n_experts = w_router.shape[1]
    per_group = n_experts // n_groups
    assert n_groups + n_experts <= LANES
    pad = LANES - n_groups - n_experts
    w = jnp.concatenate([w_group, w_router, jnp.zeros((d, pad), F32)], axis=1).astype(F32)
    bias = jnp.concatenate([b_group, b_router, jnp.zeros((pad,), F32)]).reshape(1, LANES).astype(F32)
    lane = jnp.arange(LANES)
    lane_grp = jnp.where((lane >= n_groups) & (lane < n_groups + n_experts),
                         (lane - n_groups) // per_group, -1).astype(F32).reshape(1, LANES)
    tm = _tile(n, tm)
    ids, wts = pl.pallas_call(
        functools.partial(_router_kernel, n_groups=n_groups),
        grid=(n // tm,),
        in_specs=[pl.BlockSpec((tm, d), lambda i: (i, 0)),
                  pl.BlockSpec((1, d), lambda i: (0, 0)),
                  pl.BlockSpec((d, LANES), lambda i: (0, 0)),
                  pl.BlockSpec((1, LANES), lambda i: (0, 0)),
                  pl.BlockSpec((1, LANES), lambda i: (0, 0))],
        out_specs=[pl.BlockSpec((tm, LANES), lambda i: (i, 0)),
                   pl.BlockSpec((tm, LANES), lambda i: (i, 0))],
        out_shape=[jax.ShapeDtypeStruct((n, LANES), jnp.int32),
                   jax.ShapeDtypeStruct((n, LANES), F32)],
        compiler_params=pltpu.CompilerParams(
            dimension_semantics=("parallel",),
            vmem_limit_bytes=_vmem_limit([_nbytes((tm, d), F32), _nbytes((d, LANES), F32)],
                                         8 * _nbytes((tm, d), F32))),
        name="moe_router",
    )(h, norm_g.reshape(1, d).astype(F32), w, bias, lane_grp)
    return ids[:, :TOP_K], wts


def _row_copy(src_hbm, row, dst, slot, sem):
    return pltpu.make_async_copy(src_hbm.at[pl.ds(row, 1)], dst.at[pl.ds(slot, 1)], sem)


GATHER_UNROLL = 8


def _gather_rows(src_hbm, ids_ref, n_ids, stride, offset, dst, sem, wait):
    def body(r, c):
        cp = _row_copy(src_hbm, ids_ref[0, 0, stride * r + offset], dst, r, sem)
        if wait:
            cp.wait()
        else:
            cp.start()
        return c

    lax.fori_loop(0, n_ids, body, 0, unroll=GATHER_UNROLL)


def _moe_expert_kernel(be_ref, nused_ref, tok_ref, tok_next_ref, h_hbm, g_ref, wg_ref, wu_ref, wd_ref,
                       y_ref, x_buf, sems):
    del be_ref
    i = pl.program_id(0)
    n_used = nused_ref[0]
    rows = x_buf.shape[1]
    slot = i % 2

    @pl.when((i == 0) & (n_used > 0))
    def _():
        _gather_rows(h_hbm, tok_ref, rows, 1, 0, x_buf.at[0], sems.at[0], wait=False)

    @pl.when(i + 1 < n_used)
    def _():
        _gather_rows(h_hbm, tok_next_ref, rows, 1, 0, x_buf.at[1 - slot], sems.at[1 - slot], wait=False)

    @pl.when(i < n_used)
    def _():
        _gather_rows(h_hbm, tok_ref, rows, 1, 0, x_buf.at[slot], sems.at[slot], wait=True)
        x = x_buf[slot]
        x = (x * lax.rsqrt(jnp.mean(x * x, axis=-1, keepdims=True) + EPS) * g_ref[...]).astype(BF16)
        a = jnp.dot(x, wg_ref[0], preferred_element_type=F32)
        u = jnp.dot(x, wu_ref[0], preferred_element_type=F32)
        hmid = (a * jax.nn.sigmoid(a) * u).astype(BF16)
        y_ref[...] = jnp.dot(hmid, wd_ref[0], preferred_element_type=F32)

    @pl.when(i >= nused_ref[0])
    def _():
        y_ref[...] = jnp.zeros(y_ref.shape, y_ref.dtype)


def moe_experts(h, norm_g, tok_buf, blk_e, n_used, w_gate, w_up, w_down):
    n, d = h.shape
    n_blocks = blk_e.shape[0]
    de = w_gate.shape[2]
    tok_blocks = tok_buf.reshape(n_blocks, 1, MOE_BLOCK)
    grid_spec = pltpu.PrefetchScalarGridSpec(
        num_scalar_prefetch=2,
        grid=(n_blocks,),
        in_specs=[pl.BlockSpec((1, 1, MOE_BLOCK), lambda i, be, nu: (i, 0, 0), memory_space=pltpu.SMEM),
                  pl.BlockSpec((1, 1, MOE_BLOCK), lambda i, be, nu: (jnp.minimum(i + 1, n_blocks - 1), 0, 0),
                               memory_space=pltpu.SMEM),
                  pl.BlockSpec(memory_space=pl.ANY),
                  pl.BlockSpec((1, d), lambda i, be, nu: (0, 0)),
                  pl.BlockSpec((1, d, de), lambda i, be, nu: (be[i], 0, 0)),
                  pl.BlockSpec((1, d, de), lambda i, be, nu: (be[i], 0, 0)),
                  pl.BlockSpec((1, de, d), lambda i, be, nu: (be[i], 0, 0))],
        out_specs=pl.BlockSpec((MOE_BLOCK, d), lambda i, be, nu: (i, 0)),
        scratch_shapes=[pltpu.VMEM((2, MOE_BLOCK, d), F32), pltpu.SemaphoreType.DMA((2,))],
    )
    return pl.pallas_call(
        _moe_expert_kernel,
        grid_spec=grid_spec,
        out_shape=jax.ShapeDtypeStruct((n_blocks * MOE_BLOCK, d), F32),
        compiler_params=pltpu.CompilerParams(
            dimension_semantics=("arbitrary",),
            vmem_limit_bytes=_vmem_limit([3 * _nbytes((d, de), BF16), _nbytes((MOE_BLOCK, d), F32)],
                                         6 * _nbytes((MOE_BLOCK, d), F32))),
        name="moe_experts",
    )(blk_e, n_used, tok_blocks, tok_blocks, h, norm_g.reshape(1, d).astype(F32), w_gate, w_up, w_down)


def _combine_kernel(dest_ref, dest_next_ref, h_ref, w_ref, y_hbm, o_ref, buf, sems):
    i = pl.program_id(0)
    rows = buf.shape[2]
    slot = i % 2

    def gather(ids_ref, s, wait):
        for k in range(TOP_K):
            _gather_rows(y_hbm, ids_ref, rows, TOP_K, k, buf.at[s, k], sems.at[s], wait)

    @pl.when(i == 0)
    def _():
        gather(dest_ref, 0, False)

    @pl.when(i + 1 < pl.num_programs(0))
    def _():
        gather(dest_next_ref, 1 - slot, False)

    gather(dest_ref, slot, True)
    w = w_ref[...]
    o_ref[...] = h_ref[...] + (buf[slot, 0] * w[:, 0:1] + buf[slot, 1] * w[:, 1:2])


def moe_combine(h, y_buf, dest, weights, tm=128):
    n, d = h.shape
    tm = _tile(n, tm)
    nt = n // tm
    dest_blocks = dest.reshape(nt, 1, TOP_K * tm)
    return pl.pallas_call(
        _combine_kernel,
        grid=(nt,),
        in_specs=[pl.BlockSpec((1, 1, TOP_K * tm), lambda i: (i, 0, 0), memory_space=pltpu.SMEM),
                  pl.BlockSpec((1, 1, TOP_K * tm), lambda i: (jnp.minimum(i + 1, nt - 1), 0, 0),
                               memory_space=pltpu.SMEM),
                  pl.BlockSpec((tm, d), lambda i: (i, 0)),
                  pl.BlockSpec((tm, LANES), lambda i: (i, 0)),
                  pl.BlockSpec(memory_space=pl.ANY)],
        out_specs=pl.BlockSpec((tm, d), lambda i: (i, 0)),
        out_shape=jax.ShapeDtypeStruct((n, d), F32),
        scratch_shapes=[pltpu.VMEM((2, TOP_K, tm, d), F32), pltpu.SemaphoreType.DMA((2,))],
        compiler_params=pltpu.CompilerParams(
            dimension_semantics=("arbitrary",),
            vmem_limit_bytes=_vmem_limit([2 * _nbytes((tm, d), F32)], 8 * _nbytes((tm, d), F32))),
        name="moe_combine",
    )(dest_blocks, dest_blocks, h, weights, y_buf)


def _moe_dispatch(expert_id, n_experts):
    n = expert_id.shape[0]
    nk = n * TOP_K
    flat_e = expert_id.reshape(nk).astype(jnp.int32)
    order = jnp.argsort(flat_e).astype(jnp.int32)
    rank = jnp.argsort(order).astype(jnp.int32)
    counts = jnp.bincount(flat_e, length=n_experts).astype(jnp.int32)
    start = jnp.cumsum(counts) - counts
    padded = (counts + MOE_BLOCK - 1) // MOE_BLOCK * MOE_BLOCK
    pad_end = jnp.cumsum(padded)
    pad_start = pad_end - padded
    n_blocks = (nk + n_experts * (MOE_BLOCK - 1) + MOE_BLOCK - 1) // MOE_BLOCK
    blk_start = jnp.arange(n_blocks, dtype=jnp.int32) * MOE_BLOCK
    blk_e = jnp.minimum(jnp.sum(pad_end[None, :] <= blk_start[:, None], axis=1), n_experts - 1).astype(jnp.int32)
    dest = (pad_start[flat_e] + (rank - start[flat_e])).astype(jnp.int32)
    slot_e = jnp.repeat(blk_e, MOE_BLOCK)
    within = jnp.arange(n_blocks * MOE_BLOCK, dtype=jnp.int32) - pad_start[slot_e]
    src = jnp.clip(start[slot_e] + within, 0, nk - 1)
    tok_buf = jnp.where((within >= 0) & (within < counts[slot_e]), order[src] // TOP_K, 0).astype(jnp.int32)
    n_used = (pad_end[-1] // MOE_BLOCK).astype(jnp.int32).reshape(1)
    return tok_buf, blk_e, n_used, dest


def kernel(x, mem, norm1_g, norm2_g, mem_norm_g, w_in, b_gate, fox_f_bias, fox_q_g, fox_k_g,
           hgrn_lb_logits, hgrn_out_g, mem_q_g, mem_k_g, w_mem_kv, w_branch_fox, w_branch_hgrn,
           w_branch_mem, w_out, w_group, b_group, w_router, b_router, w_exp_gate, w_exp_up,
           w_exp_down):
    batch, seq, d = x.shape
    n_mem = mem.shape[1]
    depth = w_in.shape[0]
    fox_heads, fox_hd = fox_f_bias.shape[1], fox_q_g.shape[1]
    fox_w = fox_heads * fox_hd
    hgrn_w = hgrn_lb_logits.shape[1]
    hgrn_heads = hgrn_w // HGRN_EXPAND
    hgrn_dv = hgrn_out_g.shape[1]
    mem_hd = mem_q_g.shape[1]
    mem_w = w_branch_mem.shape[1]
    mem_heads = mem_w // mem_hd
    n_experts = w_router.shape[2]
    m = batch * seq

    sizes = (fox_w, fox_w, fox_w, fox_heads, hgrn_w, hgrn_w, hgrn_w, hgrn_w, mem_w, 3 * d)
    offs = [0]
    for sz in sizes:
        offs.append(offs[-1] + sz)
    (o_fq, o_fk, o_fv, o_ff, o_hq, o_hf, o_hi, o_hg, o_mq, o_gt, o_end) = offs

    lb_all = jnp.cumsum(jax.nn.softmax(hgrn_lb_logits.astype(F32), axis=0), axis=0)
    mem_n = rmsnorm_rows(mem.reshape(batch * n_mem, d), mem_norm_g, BF16)

    h = x.reshape(m, d)
    for layer in range(depth):
        wl = w_in[layer]
        cols = lambda a, b: wl[:, a:b].astype(BF16)
        hn = rmsnorm_rows(h, norm1_g[layer], BF16)

        qk_gain = jnp.concatenate([jnp.tile(fox_q_g[layer] * (LOG2E * fox_hd ** -0.5), fox_heads),
                                   jnp.tile(fox_k_g[layer], fox_heads)])
        qk = matmul(hn, cols(o_fq, o_fv), BF16, functools.partial(_epi_headnorm, head_dim=fox_hd),
                    row_vec=qk_gain, name="proj_fox_qk")
        v_t = matmul_t(hn, wl[:, o_fv:o_ff].T.astype(BF16), BF16, tm=FOX_BLOCK, name="proj_fox_vt")
        w_plain = jnp.concatenate([wl[:, o_hq:o_hf], wl[:, o_hi:o_mq]], axis=1).astype(BF16)
        plain = matmul(hn, w_plain, BF16, name="proj_plain")
        hd_blk = lambda width: width // HGRN_EXPAND
        c_hq = 0
        c_hi = c_hq + hd_blk(hgrn_w)
        c_hg = c_hi + hd_blk(hgrn_w)
        z = matmul(hn, cols(o_hf, o_hi), F32, name="proj_hgrn_f")
        w_ff = jnp.pad(wl[:, o_ff:o_hq], ((0, 0), (0, LANES - fox_heads))).astype(BF16)
        ff = matmul(hn, w_ff, F32, tn=LANES, name="proj_fox_f")
        mq = matmul(hn, cols(o_mq, o_gt), BF16, functools.partial(_epi_headnorm, head_dim=mem_hd),
                    row_vec=jnp.tile(mem_q_g[layer] * mem_hd ** -0.5, mem_heads), tn=mem_hd,
                    name="proj_mem_q")
        gates = matmul(hn, cols(o_gt, o_end), BF16, _epi_sigmoid_bias, row_vec=b_gate[layer],
                       name="proj_gates")

        kbias = fox_key_bias(ff, fox_f_bias[layer], batch, fox_heads, fox_hd)
        o_a = fox_attention(qk, kbias, v_t, batch, fox_heads, fox_hd)

        o_b = hgrn2(plain, c_hq, c_hi, c_hg, z, lb_all[layer], hgrn_out_g[layer], batch, hgrn_heads)

        wkv = w_mem_kv[layer]
        mk = matmul(mem_n, wkv[:, :mem_w].astype(BF16), BF16,
                    functools.partial(_epi_headnorm, head_dim=mem_hd),
                    row_vec=jnp.tile(mem_k_g[layer], mem_heads), tn=mem_hd, name="proj_mem_k")
        mv = matmul(mem_n, wkv[:, mem_w:].astype(BF16), BF16, name="proj_mem_v")
        o_c = mem_attention(mq, mk, mv, batch, mem_heads, mem_hd)

        merged = gated_merge(o_a, o_b, o_c, w_branch_fox[layer].astype(BF16),
                             w_branch_hgrn[layer].astype(BF16), w_branch_mem[layer].astype(BF16), gates)
        h = matmul(merged, w_out[layer].astype(BF16), F32, _epi_residual, residual=h, tn=512,
                   name="proj_out")

        expert_id, weights = moe_router(h, norm2_g[layer], w_group[layer], b_group[layer],
                                        w_router[layer], b_router[layer])
        tok_buf, blk_e, n_used, dest = _moe_dispatch(expert_id, n_experts)
        y_buf = moe_experts(h, norm2_g[layer], tok_buf, blk_e, n_used, w_exp_gate[layer].astype(BF16),
                            w_exp_up[layer].astype(BF16), w_exp_down[layer].astype(BF16))
        h = moe_combine(h, y_buf, dest, weights)
    return h.reshape(batch, seq, d)
```

```python
import functools

import jax
import jax.numpy as jnp
from jax import lax
from jax.experimental import pallas as pl
from jax.experimental.pallas import tpu as pltpu

F32 = jnp.float32
BF16 = jnp.bfloat16

EPS = 1e-6
CHUNK = 64
SUB = 16
HGRN_EXPAND = 128
TOP_K = 2
MOE_BLOCK = 128
FOX_BLOCK = 512
LANES = 128
LOG2E = 1.4426950408889634
V7X_VMEM_LIMIT_CAP = 60000 * 1024


def _vmem_limit(block_bytes, temp_bytes=0):
    need = int(1.25 * (2 * sum(block_bytes) + temp_bytes)) + (2 << 20)
    return min(max(need, 16 << 20), V7X_VMEM_LIMIT_CAP)


def _nbytes(shape, dtype):
    n = 1
    for s in shape:
        n *= s
    return n * jnp.dtype(dtype).itemsize


def _tile(dim, want):
    if dim <= want:
        return dim
    for t in range(want - want % LANES, 0, -LANES):
        if dim % t == 0:
            return t
    raise ValueError((dim, want))


def _rmsnorm_kernel(x_ref, g_ref, o_ref):
    x = x_ref[...].astype(F32)
    y = x * lax.rsqrt(jnp.mean(x * x, axis=-1, keepdims=True) + EPS)
    o_ref[...] = (y * g_ref[...]).astype(o_ref.dtype)


def rmsnorm_rows(x, g, out_dtype, tm=256):
    m, d = x.shape
    tm = _tile(m, tm)
    return pl.pallas_call(
        _rmsnorm_kernel,
        grid=(m // tm,),
        in_specs=[pl.BlockSpec((tm, d), lambda i: (i, 0)),
                  pl.BlockSpec((1, d), lambda i: (0, 0))],
        out_specs=pl.BlockSpec((tm, d), lambda i: (i, 0)),
        out_shape=jax.ShapeDtypeStruct((m, d), out_dtype),
        compiler_params=pltpu.CompilerParams(
            dimension_semantics=("parallel",),
            vmem_limit_bytes=_vmem_limit([_nbytes((tm, d), x.dtype), _nbytes((tm, d), out_dtype)],
                                         2 * _nbytes((tm, d), F32))),
        name="rmsnorm_rows",
    )(x, g.reshape(1, d).astype(F32))


def _mm_kernel(a_ref, w_ref, *refs, epilogue):
    o_ref = refs[-1]
    acc = jnp.dot(a_ref[...], w_ref[...], preferred_element_type=F32)
    epilogue(acc, o_ref, *refs[:-1])


def _epi_plain(acc, o_ref):
    o_ref[...] = acc.astype(o_ref.dtype)


def _epi_headnorm(acc, o_ref, g_ref, *, head_dim):
    for h in range(acc.shape[1] // head_dim):
        sl = slice(h * head_dim, (h + 1) * head_dim)
        blk = acc[:, sl]
        y = blk * lax.rsqrt(jnp.mean(blk * blk, axis=-1, keepdims=True) + EPS)
        o_ref[:, sl] = (y * g_ref[:, sl]).astype(o_ref.dtype)


def _epi_sigmoid_bias(acc, o_ref, b_ref):
    o_ref[...] = jax.nn.sigmoid(acc + b_ref[...]).astype(o_ref.dtype)


def _epi_residual(acc, o_ref, r_ref):
    o_ref[...] = (r_ref[...] + acc).astype(o_ref.dtype)


def matmul(a, w, out_dtype, epilogue=_epi_plain, row_vec=None, residual=None, tm=1024, tn=1024,
           name="matmul"):
    m, k = a.shape
    n = w.shape[1]
    tm, tn = _tile(m, tm), _tile(n, tn)
    in_specs = [pl.BlockSpec((tm, k), lambda i, j: (i, 0)),
                pl.BlockSpec((k, tn), lambda i, j: (0, j))]
    operands = [a, w]
    blocks = [_nbytes((tm, k), a.dtype), _nbytes((k, tn), w.dtype), _nbytes((tm, tn), out_dtype)]
    if row_vec is not None:
        in_specs.append(pl.BlockSpec((1, tn), lambda i, j: (0, j)))
        operands.append(row_vec.reshape(1, n).astype(F32))
    if residual is not None:
        in_specs.append(pl.BlockSpec((tm, tn), lambda i, j: (i, j)))
        operands.append(residual)
        blocks.append(_nbytes((tm, tn), residual.dtype))
    return pl.pallas_call(
        functools.partial(_mm_kernel, epilogue=epilogue),
        grid=(m // tm, n // tn),
        in_specs=in_specs,
        out_specs=pl.BlockSpec((tm, tn), lambda i, j: (i, j)),
        out_shape=jax.ShapeDtypeStruct((m, n), out_dtype),
        compiler_params=pltpu.CompilerParams(
            dimension_semantics=("parallel", "parallel"),
            vmem_limit_bytes=_vmem_limit(blocks, 2 * _nbytes((tm, tn), F32))),
        name=name,
    )(*operands)


def _mm_t_kernel(wt_ref, a_ref, o_ref):
    o_ref[0] = lax.dot_general(wt_ref[...], a_ref[...], (((1,), (1,)), ((), ())),
                               preferred_element_type=F32).astype(o_ref.dtype)


def matmul_t(a, w_t, out_dtype, tm=512, tn=1024, name="matmul_t"):
    m, k = a.shape
    n = w_t.shape[0]
    tm, tn = _tile(m, tm), _tile(n, tn)
    return pl.pallas_call(
        _mm_t_kernel,
        grid=(m // tm, n // tn),
        in_specs=[pl.BlockSpec((tn, k), lambda i, j: (j, 0)),
                  pl.BlockSpec((tm, k), lambda i, j: (i, 0))],
        out_specs=pl.BlockSpec((1, tn, tm), lambda i, j: (i, j, 0)),
        out_shape=jax.ShapeDtypeStruct((m // tm, n, tm), out_dtype),
        compiler_params=pltpu.CompilerParams(
            dimension_semantics=("parallel", "parallel"),
            vmem_limit_bytes=_vmem_limit([_nbytes((tm, k), a.dtype), _nbytes((tn, k), w_t.dtype),
                                          _nbytes((tn, tm), out_dtype)], 2 * _nbytes((tn, tm), F32))),
        name=name,
    )(w_t, a)


N_SPLIT = 3


def _fox_bias_kernel(f_ref, b_ref, e_ref, o_ref, carry):
    @pl.when(pl.program_id(1) == 0)
    def _():
        carry[...] = jnp.zeros(carry.shape, F32)

    x = f_ref[...] + b_ref[...]
    a = jnp.minimum(x, 0.0) - jnp.log(1.0 + jnp.exp(-jnp.abs(x)))
    r = a.shape[0]
    row = lax.broadcasted_iota(jnp.int32, a.shape, 0)
    sh = 1
    while sh < r:
        a = a + jnp.where(row >= sh, pltpu.roll(a, sh, axis=0), 0.0)
        sh *= 2
    c = a + carry[...]
    carry[...] = c[r - 1:r, :]
    rest = c * (-LOG2E)
    pieces = []
    for _ in range(N_SPLIT):
        piece = rest.astype(BF16)
        pieces.append(piece)
        rest = rest - piece.astype(F32)
    o_ref[...] = jnp.dot(jnp.concatenate(pieces, axis=1), e_ref[...],
                         preferred_element_type=F32).astype(o_ref.dtype)


def fox_key_bias(ff, bias, batch, heads, head_dim, rows=1024):
    m = ff.shape[0]
    s = m // batch
    rows = _tile(s, rows)
    nr = s // rows
    src = jnp.arange(N_SPLIT * LANES)
    dst = (src % LANES) * head_dim + src // LANES
    place = ((jnp.arange(heads * head_dim)[None, :] == dst[:, None])
             & ((src % LANES) < heads)[:, None]).astype(BF16)
    bias_row = jnp.zeros((1, LANES), F32).at[0, :heads].set(bias.astype(F32))
    return pl.pallas_call(
        _fox_bias_kernel,
        grid=(batch, nr),
        in_specs=[pl.BlockSpec((rows, LANES), lambda b, t: (b * nr + t, 0)),
                  pl.BlockSpec((1, LANES), lambda b, t: (0, 0)),
                  pl.BlockSpec((N_SPLIT * LANES, heads * head_dim), lambda b, t: (0, 0))],
        out_specs=pl.BlockSpec((rows, heads * head_dim), lambda b, t: (b * nr + t, 0)),
        out_shape=jax.ShapeDtypeStruct((m, heads * head_dim), BF16),
        scratch_shapes=[pltpu.VMEM((1, LANES), F32)],
        compiler_params=pltpu.CompilerParams(
            dimension_semantics=("parallel", "arbitrary"),
            vmem_limit_bytes=_vmem_limit([_nbytes((rows, heads * head_dim), BF16),
                                          _nbytes((N_SPLIT * LANES, heads * head_dim), BF16)],
                                         _nbytes((rows, heads * head_dim), F32) + 16 * _nbytes((rows, LANES), F32))),
        name="fox_key_bias",
    )(ff, bias_row, place)


def _fox_attn_kernel(q_ref, k_ref, kb_ref, vt_ref, o_ref, s_a, s_b, m_scr, l_scr, acc_scr):
    i = pl.program_id(2)
    tq, hd = q_ref.shape
    tk = tq

    q_t = q_ref[...].astype(F32).T.astype(BF16)
    ones_rows = jnp.where(lax.broadcasted_iota(jnp.int32, (hd, tq), 0) < N_SPLIT, 1.0, 0.0).astype(BF16)
    q_aug = jnp.concatenate([q_t, ones_rows], axis=0)

    def scores(j):
        off = pl.multiple_of(j * tk, tk)
        k_aug = jnp.concatenate([k_ref[pl.ds(off, tk), :], kb_ref[pl.ds(off, tk), :]], axis=1)
        return jnp.dot(k_aug, q_aug, preferred_element_type=F32)

    def update(s, j, masked):
        if masked:
            key = lax.broadcasted_iota(jnp.int32, (tk, tq), 0)
            qry = lax.broadcasted_iota(jnp.int32, (tk, tq), 1)
            s = jnp.where(key <= qry, s, -jnp.inf)
        m_prev = m_scr[...]
        m_new = jnp.maximum(m_prev, jnp.max(s, axis=0, keepdims=True))
        alpha = jnp.exp2(m_prev - m_new)
        p = jnp.exp2(s - m_new)
        l_scr[...] = alpha * l_scr[...] + jnp.sum(p, axis=0, keepdims=True)
        acc_scr[...] = alpha * acc_scr[...] + jnp.dot(vt_ref[0, j], p.astype(BF16),
                                                      preferred_element_type=F32)
        m_scr[...] = m_new

    m_scr[...] = jnp.full(m_scr.shape, -jnp.inf, F32)
    l_scr[...] = jnp.zeros(l_scr.shape, F32)
    acc_scr[...] = jnp.zeros(acc_scr.shape, F32)
    s_a[...] = scores(0)

    def pair(t, carry):
        j0 = 2 * t
        s_b[...] = scores(j0 + 1)
        update(s_a[...], j0, False)
        s_a[...] = scores(j0 + 2)
        update(s_b[...], j0 + 1, False)
        return carry

    lax.fori_loop(0, i // 2, pair, 0)

    @pl.when(i % 2 == 0)
    def _():
        update(s_a[...], i, True)

    @pl.when(i % 2 == 1)
    def _():
        s_b[...] = scores(i)
        update(s_a[...], i - 1, False)
        update(s_b[...], i, True)

    o_ref[...] = (acc_scr[...] / l_scr[...]).T.astype(o_ref.dtype)


def fox_attention(qk, kbias, v_t, batch, heads, head_dim):
    m = qk.shape[0]
    s = m // batch
    tq = v_t.shape[2]
    nq = s // tq
    return pl.pallas_call(
        _fox_attn_kernel,
        grid=(batch, heads, nq),
        in_specs=[pl.BlockSpec((tq, head_dim), lambda b, h, i: (b * nq + i, h)),
                  pl.BlockSpec((s, head_dim), lambda b, h, i: (b, heads + h)),
                  pl.BlockSpec((s, head_dim), lambda b, h, i: (b, h)),
                  pl.BlockSpec((1, nq, head_dim, tq), lambda b, h, i: (b, 0, h, 0))],
        out_specs=pl.BlockSpec((tq, head_dim), lambda b, h, i: (b * nq + i, h)),
        out_shape=jax.ShapeDtypeStruct((m, heads * head_dim), BF16),
        scratch_shapes=[pltpu.VMEM((tq, tq), F32), pltpu.VMEM((tq, tq), F32),
                        pltpu.VMEM((1, tq), F32), pltpu.VMEM((1, tq), F32),
                        pltpu.VMEM((head_dim, tq), F32)],
        compiler_params=pltpu.CompilerParams(
            dimension_semantics=("parallel", "parallel", "arbitrary"),
            vmem_limit_bytes=_vmem_limit([3 * _nbytes((s, head_dim), BF16), 2 * _nbytes((tq, head_dim), BF16)],
                                         8 * _nbytes((tq, tq), F32))),
        name="fox_attention",
    )(qk, qk, kbias, v_t.reshape(batch, nq, heads * head_dim, tq))


def _hgrn_kernel(q_ref, z_ref, v_ref, hg_ref, lb_ref, og_ref, o_ref, st_ref):
    @pl.when(pl.program_id(2) == 0)
    def _():
        st_ref[...] = jnp.zeros(st_ref.shape, F32)

    t, dk = z_ref.shape
    dv = v_ref.shape[1]
    nc, nb, nsub = t // CHUNK, t // SUB, CHUNK // SUB

    q = q_ref[...].astype(F32)
    z = z_ref[...]
    lb = lb_ref[...]
    g = jnp.log(lb + (1.0 - lb) * jax.nn.sigmoid(z))
    kk = (1.0 - lb) * jax.nn.sigmoid(-z)

    row = lax.broadcasted_iota(jnp.int32, (t, dk), 0)
    rin = row % CHUNK
    b = g
    sh = 1
    while sh < CHUNK:
        b = b + jnp.where(rin >= sh, pltpu.roll(b, sh, axis=0), 0.0)
        sh *= 2

    b3 = b.reshape(nc, CHUNK, dk)
    q3 = q.reshape(nc, CHUNK, dk)
    k3 = kk.reshape(nc, CHUNK, dk)
    v3 = v_ref[...].reshape(nc, CHUNK, dv)
    b_last = b3[:, CHUNK - 1:CHUNK, :]
    q_in = (q3 * jnp.exp(b3)).astype(BF16)
    k_out = (k3 * jnp.exp(b_last - b3)).astype(BF16)
    d_last = jnp.exp(b_last)

    st = st_ref[...]
    inter = []
    for c in range(nc):
        inter.append(lax.dot_general(q_in[c], st.astype(BF16), (((1,), (1,)), ((), ())),
                                     preferred_element_type=F32))
        kv_t = lax.dot_general(v3[c], k_out[c], (((0,), (0,)), ((), ())),
                               preferred_element_type=F32)
        st = st * d_last[c] + kv_t
    st_ref[...] = st
    inter = jnp.concatenate(inter, axis=0)

    rsub_c = (rin // SUB).reshape(nc, CHUNK, dk)
    q_parts, k_parts = [], []
    for blk in range(1, nsub):
        b_ref_pt = b3[:, blk * SUB - 1:blk * SUB, :]
        q_hat = q3 * jnp.exp(jnp.minimum(b3 - b_ref_pt, 0.0))
        q_parts.append(jnp.where(rsub_c == blk, q_hat, 0.0).astype(BF16))
        k_parts.append((k3 * jnp.exp(jnp.minimum(b_ref_pt - b3, 0.0))).astype(BF16))
    q_cat = jnp.concatenate(q_parts, axis=2)
    k_cat = jnp.concatenate(k_parts, axis=2)
    s_off = lax.dot_general(q_cat, k_cat, (((2,), (2,)), ((0,), (0,))),
                            preferred_element_type=F32)

    b4 = b.reshape(nb, SUB, dk)
    q4 = q.reshape(nb, SUB, dk)
    k4 = kk.reshape(nb, SUB, dk)
    rsub = (row % SUB).reshape(nb, SUB, dk)
    u_row = lax.broadcasted_iota(jnp.int32, (dk, LANES), 1)
    half = SUB // 2
    s_lo = jnp.zeros((t, LANES), F32)
    s_hi = jnp.zeros((t // 2, LANES), F32)
    for j in range(SUB):
        r0 = 0 if j < half else half
        bj, kj = b4[:, j:j + 1, :], k4[:, j:j + 1, :]
        decay = jnp.exp(jnp.where(rsub[:, r0:, :] >= j, b4[:, r0:, :] - bj, -jnp.inf))
        p = (q4[:, r0:, :] * (kj * decay)).reshape(-1, dk).astype(BF16)
        place = jnp.where((u_row % SUB == j) & (u_row < CHUNK), 1.0, 0.0).astype(BF16)
        if j < half:
            s_lo = s_lo + jnp.dot(p, place, preferred_element_type=F32)
        else:
            s_hi = s_hi + jnp.dot(p, place, preferred_element_type=F32)
    s_diag = s_lo.reshape(nb, SUB, LANES) + jnp.concatenate(
        [jnp.zeros((nb, half, LANES), F32), s_hi.reshape(nb, half, LANES)], axis=1)
    s_diag = s_diag.reshape(nc, CHUNK, LANES)[:, :, :CHUNK]

    rb = lax.broadcasted_iota(jnp.int32, (nc, CHUNK, CHUNK), 1) // SUB
    cb = lax.broadcasted_iota(jnp.int32, (nc, CHUNK, CHUNK), 2) // SUB
    scores = jnp.where(cb == rb, s_diag, jnp.where(cb < rb, s_off, 0.0)).astype(BF16)
    intra = lax.dot_general(scores, v3, (((2,), (1,)), ((0,), (0,))),
                            preferred_element_type=F32).reshape(t, dv)

    o = inter + intra
    o = o * lax.rsqrt(jnp.mean(o * o, axis=-1, keepdims=True) + EPS) * og_ref[...]
    hg = hg_ref[...].astype(F32)
    o_ref[...] = (o * (hg * jax.nn.sigmoid(hg))).astype(o_ref.dtype)


def hgrn2(src, q_col0, v_col0, hg_col0, z, lb, out_g, batch, heads, rows=512):
    m = z.shape[0]
    s = m // batch
    dk = HGRN_EXPAND
    dv = out_g.shape[-1]
    rows = _tile(s, rows)
    nt = s // rows
    row_blk = lambda b, h, t: b * nt + t
    return pl.pallas_call(
        _hgrn_kernel,
        grid=(batch, heads, nt),
        in_specs=[pl.BlockSpec((rows, dk), lambda b, h, t: (row_blk(b, h, t), q_col0 + h)),
                  pl.BlockSpec((rows, dk), lambda b, h, t: (row_blk(b, h, t), h)),
                  pl.BlockSpec((rows, dv), lambda b, h, t: (row_blk(b, h, t), v_col0 + h)),
                  pl.BlockSpec((rows, dv), lambda b, h, t: (row_blk(b, h, t), hg_col0 + h)),
                  pl.BlockSpec((1, dk), lambda b, h, t: (0, h)),
                  pl.BlockSpec((1, dv), lambda b, h, t: (0, 0))],
        out_specs=pl.BlockSpec((rows, dv), lambda b, h, t: (row_blk(b, h, t), h)),
        out_shape=jax.ShapeDtypeStruct((m, heads * dv), BF16),
        scratch_shapes=[pltpu.VMEM((dv, dk), F32)],
        compiler_params=pltpu.CompilerParams(
            dimension_semantics=("parallel", "parallel", "arbitrary"),
            vmem_limit_bytes=_vmem_limit([6 * _nbytes((rows, dk), F32)], 40 * _nbytes((rows, dk), F32))),
        name="hgrn2",
    )(src, z, src, src, lb.reshape(1, -1).astype(F32), out_g.reshape(1, dv).astype(F32))


def _mem_attn_kernel(q_ref, k_ref, v_ref, o_ref):
    s = lax.dot_general(q_ref[...], k_ref[...], (((1,), (1,)), ((), ())), preferred_element_type=F32)
    m = jnp.max(s, axis=1, keepdims=True)
    p = jnp.exp(s - m)
    l = jnp.sum(p, axis=1, keepdims=True)
    o = jnp.dot(p.astype(BF16), v_ref[...], preferred_element_type=F32)
    o_ref[...] = (o / l).astype(o_ref.dtype)


def mem_attention(q, mk, mv, batch, heads, head_dim, tq=1024):
    m = q.shape[0]
    s = m // batch
    n_mem = mk.shape[0] // batch
    tq = _tile(s, tq)
    nq = s // tq
    return pl.pallas_call(
        _mem_attn_kernel,
        grid=(batch, heads, nq),
        in_specs=[pl.BlockSpec((tq, head_dim), lambda b, h, i: (b * nq + i, h)),
                  pl.BlockSpec((n_mem, head_dim), lambda b, h, i: (b, h)),
                  pl.BlockSpec((n_mem, head_dim), lambda b, h, i: (b, h))],
        out_specs=pl.BlockSpec((tq, head_dim), lambda b, h, i: (b * nq + i, h)),
        out_shape=jax.ShapeDtypeStruct((m, heads * head_dim), BF16),
        compiler_params=pltpu.CompilerParams(
            dimension_semantics=("parallel", "parallel", "parallel"),
            vmem_limit_bytes=_vmem_limit([2 * _nbytes((tq, head_dim), BF16), 2 * _nbytes((n_mem, head_dim), BF16)],
                                         4 * _nbytes((tq, head_dim), F32))),
        name="mem_attention",
    )(q, mk, mv)


def _merge_kernel(a_ref, b_ref, c_ref, wa_ref, wb_ref, wc_ref, ga_ref, gb_ref, gc_ref, o_ref):
    acc = ga_ref[...].astype(F32) * jnp.dot(a_ref[...], wa_ref[...], preferred_element_type=F32)
    acc = acc + gb_ref[...].astype(F32) * jnp.dot(b_ref[...], wb_ref[...], preferred_element_type=F32)
    acc = acc + gc_ref[...].astype(F32) * jnp.dot(c_ref[...], wc_ref[...], preferred_element_type=F32)
    o_ref[...] = acc.astype(o_ref.dtype)


def gated_merge(o_a, o_b, o_c, w_a, w_b, w_c, gates, tm=1024, tn=512):
    m = o_a.shape[0]
    d = w_a.shape[1]
    tm, tn = _tile(m, tm), _tile(d, tn)
    nj = d // tn
    a_spec = lambda arr: pl.BlockSpec((tm, arr.shape[1]), lambda i, j: (i, 0))
    w_spec = lambda arr: pl.BlockSpec((arr.shape[0], tn), lambda i, j: (0, j))
    g_spec = lambda r: pl.BlockSpec((tm, tn), lambda i, j: (i, r * nj + j))
    blocks = ([_nbytes((tm, a.shape[1]), BF16) for a in (o_a, o_b, o_c)]
              + [_nbytes((w.shape[0], tn), BF16) for w in (w_a, w_b, w_c)] + 4 * [_nbytes((tm, tn), BF16)])
    return pl.pallas_call(
        _merge_kernel,
        grid=(m // tm, nj),
        in_specs=[a_spec(o_a), a_spec(o_b), a_spec(o_c), w_spec(w_a), w_spec(w_b), w_spec(w_c),
                  g_spec(0), g_spec(1), g_spec(2)],
        out_specs=pl.BlockSpec((tm, tn), lambda i, j: (i, j)),
        out_shape=jax.ShapeDtypeStruct((m, d), BF16),
        compiler_params=pltpu.CompilerParams(
            dimension_semantics=("parallel", "parallel"),
            vmem_limit_bytes=_vmem_limit(blocks, 3 * _nbytes((tm, tn), F32))),
        name="gated_merge",
    )(o_a, o_b, o_c, w_a, w_b, w_c, gates, gates, gates)


def _router_kernel(h_ref, g_ref, whi_ref, wlo_ref, b_ref, lg_ref, id_ref, wt_ref, *, n_groups):
    x = h_ref[...]
    hn = x * lax.rsqrt(jnp.mean(x * x, axis=-1, keepdims=True) + EPS) * g_ref[...]
    hn_hi = hn.astype(BF16)
    hn_lo = (hn - hn_hi.astype(F32)).astype(BF16)
    logits = (jnp.dot(hn_hi, whi_ref[...], preferred_element_type=F32)
              + (jnp.dot(hn_hi, wlo_ref[...], preferred_element_type=F32)
                 + jnp.dot(hn_lo, whi_ref[...], preferred_element_type=F32))) + b_ref[...]
    lane = lax.broadcasted_iota(jnp.int32, logits.shape, 1).astype(F32)
    lane_grp = lg_ref[...]
    neg = -jnp.inf
    big = float(LANES)

    gl = jnp.where(lane < n_groups, logits, neg)
    gmax = jnp.max(gl, axis=1, keepdims=True)
    grp = jnp.min(jnp.where(gl == gmax, lane, big), axis=1, keepdims=True)
    p_grp = 1.0 / jnp.sum(jnp.exp(gl - gmax), axis=1, keepdims=True)

    el = jnp.where(lane_grp == grp, logits, neg)
    v1 = jnp.max(el, axis=1, keepdims=True)
    i1 = jnp.min(jnp.where(el == v1, lane, big), axis=1, keepdims=True)
    el2 = jnp.where(lane == i1, neg, el)
    v2 = jnp.max(el2, axis=1, keepdims=True)
    i2 = jnp.min(jnp.where(el2 == v2, lane, big), axis=1, keepdims=True)
    e21 = jnp.exp(v2 - v1)
    w1 = p_grp / (1.0 + e21)
    w2 = p_grp * e21 / (1.0 + e21)

    id_ref[...] = jnp.where(lane == 0.0, i1 - n_groups, i2 - n_groups).astype(jnp.int32)
    wt_ref[...] = jnp.where(lane == 0.0, w1, w2)


def moe_router(h, norm_g, w_group, b_group, w_router, b_router, tm=256):
    n, d = h.shape
    n_groups = w_group.shape[1]
    n_experts = w_router.shape[1]
    per_group = n_experts // n_groups
    assert n_groups + n_experts <= LANES
    pad = LANES - n_groups - n_experts
    w = jnp.concatenate([w_group, w_router, jnp.zeros((d, pad), F32)], axis=1).astype(F32)
    bias = jnp.concatenate([b_group, b_router, jnp.zeros((pad,), F32)]).reshape(1, LANES).astype(F32)
    w_hi = w.astype(BF16)
    w_lo = (w - w_hi.astype(F32)).astype(BF16)
    lane = jnp.arange(LANES)
    lane_grp = jnp.where((lane >= n_groups) & (lane < n_groups + n_experts),
                         (lane - n_groups) // per_group, -1).astype(F32).reshape(1, LANES)
    tm = _tile(n, tm)
    ids, wts = pl.pallas_call(
        functools.partial(_router_kernel, n_groups=n_groups),
        grid=(n // tm,),
        in_specs=[pl.BlockSpec((tm, d), lambda i: (i, 0)),
                  pl.BlockSpec((1, d), lambda i: (0, 0)),
                  pl.BlockSpec((d, LANES), lambda i: (0, 0)),
                  pl.BlockSpec((d, LANES), lambda i: (0, 0)),
                  pl.BlockSpec((1, LANES), lambda i: (0, 0)),
                  pl.BlockSpec((1, LANES), lambda i: (0, 0))],
        out_specs=[pl.BlockSpec((tm, LANES), lambda i: (i, 0)),
                   pl.BlockSpec((tm, LANES), lambda i: (i, 0))],
        out_shape=[jax.ShapeDtypeStruct((n, LANES), jnp.int32),
                   jax.ShapeDtypeStruct((n, LANES), F32)],
        compiler_params=pltpu.CompilerParams(
            dimension_semantics=("parallel",),
            vmem_limit_bytes=_vmem_limit([_nbytes((tm, d), F32), _nbytes((d, LANES), F32)],
                                         8 * _nbytes((tm, d), F32))),
        name="moe_router",
    )(h, norm_g.reshape(1, d).astype(F32), w_hi, w_lo, bias, lane_grp)
    return ids[:, :TOP_K], wts


def _row_copy(src_hbm, row, dst, slot, sem):
    return pltpu.make_async_copy(src_hbm.at[pl.ds(row, 1)], dst.at[pl.ds(slot, 1)], sem)


GATHER_UNROLL = 8


def _gather_rows(src_hbm, ids_ref, n_ids, stride, offset, dst, sem, wait):
    def body(r, c):
        cp = _row_copy(src_hbm, ids_ref[0, 0, stride * r + offset], dst, r, sem)
        if wait:
            cp.wait()
        else:
            cp.start()
        return c

    lax.fori_loop(0, n_ids, body, 0, unroll=GATHER_UNROLL)


W_STAGE_ROWS = 512
W_STAGE_SLOTS = 8
W_CHUNKS_PER_STEP = 8


class _ExpertWeights:
    def __init__(self, hbm, stage, sems, bufs):
        self.hbm, self.stage, self.sems, self.bufs = hbm, stage, sems, bufs
        d, de = hbm[0].shape[1], hbm[0].shape[2]
        self.de = de
        self.sr = stage.shape[1]
        self.slots = stage.shape[0]
        self.ratio = d // de
        self.rb = self.sr // self.ratio
        self.n_a = d // self.sr
        self.n_chunks = 2 * self.n_a + de // self.rb

    def _each_copy(self, idx, e, fn):
        slot = idx % self.slots
        sr, n_a = self.sr, self.n_a
        for mat in range(2):
            @pl.when((idx >= mat * n_a) & (idx < (mat + 1) * n_a))
            def _(mat=mat):
                r0 = pl.multiple_of((idx - mat * n_a) * sr, sr)
                fn(pltpu.make_async_copy(self.hbm[mat].at[e, pl.ds(r0, sr), :], self.stage.at[slot],
                                         self.sems.at[slot]))

        @pl.when(idx >= 2 * n_a)
        def _():
            r0 = pl.multiple_of((idx - 2 * n_a) * self.rb, self.rb)
            for k in range(self.ratio):
                fn(pltpu.make_async_copy(self.hbm[2].at[e, pl.ds(r0, self.rb), pl.ds(k * self.de, self.de)],
                                         self.stage.at[slot, pl.ds(k * self.rb, self.rb), :],
                                         self.sems.at[slot]))

    def start(self, idx, e):
        self._each_copy(idx, e, lambda cp: cp.start())

    def wait(self, idx, e):
        self._each_copy(idx, e, lambda cp: cp.wait())

    def convert(self, idx, dst):
        slot = idx % self.slots
        sr, n_a = self.sr, self.n_a
        for mat in range(2):
            @pl.when((idx >= mat * n_a) & (idx < (mat + 1) * n_a))
            def _(mat=mat):
                r0 = pl.multiple_of((idx - mat * n_a) * sr, sr)
                self.bufs[mat][dst, pl.ds(r0, sr), :] = self.stage[slot].astype(BF16)

        @pl.when(idx >= 2 * n_a)
        def _():
            r0 = pl.multiple_of((idx - 2 * n_a) * self.rb, self.rb)
            for k in range(self.ratio):
                self.bufs[2][dst, pl.ds(r0, self.rb), pl.ds(k * self.de, self.de)] = (
                    self.stage[slot, pl.ds(k * self.rb, self.rb), :].astype(BF16))

    def start_first(self, e):
        for idx in range(min(self.slots, self.n_chunks)):
            self.start(jnp.int32(idx), e)

    def stream(self, e, dst, first, count):
        def body(c, carry):
            idx = first + c
            self.wait(idx, e)
            self.convert(idx, dst)

            @pl.when(idx + self.slots < self.n_chunks)
            def _():
                self.start(idx + self.slots, e)
            return carry

        lax.fori_loop(0, count, body, 0)


def _moe_expert_kernel(be_ref, nused_ref, first_ref, last_ref, nexte_ref, par_ref, tok_ref, tok_next_ref,
                       h_hbm, g_ref, wg_hbm, wu_hbm, wd_hbm, y_ref, x_buf, sems, stage, wsems,
                       wg_buf, wu_buf, wd_buf, done_ref):
    i = pl.program_id(0)
    n_used = nused_ref[0]
    rows = x_buf.shape[1]
    slot = i % 2
    cur = par_ref[i]
    next_e = nexte_ref[i]
    w = _ExpertWeights((wg_hbm, wu_hbm, wd_hbm), stage, wsems, (wg_buf, wu_buf, wd_buf))

    @pl.when((i == 0) & (n_used > 0))
    def _():
        _gather_rows(h_hbm, tok_ref, rows, 1, 0, x_buf.at[0], sems.at[0], wait=False)
        w.start_first(be_ref[0])
        w.stream(be_ref[0], cur, 0, w.n_chunks)

    @pl.when(i + 1 < n_used)
    def _():
        _gather_rows(h_hbm, tok_next_ref, rows, 1, 0, x_buf.at[1 - slot], sems.at[1 - slot], wait=False)

    @pl.when((i < n_used) & (next_e >= 0))
    def _():
        @pl.when(first_ref[i] == 1)
        def _():
            done_ref[0] = 0
            w.start_first(next_e)

        done = done_ref[0]
        left = w.n_chunks - done
        count = jnp.where(last_ref[i] == 1, left, jnp.minimum(W_CHUNKS_PER_STEP, left))
        w.stream(next_e, 1 - cur, done, count)
        done_ref[0] = done + count

    @pl.when(i < n_used)
    def _():
        _gather_rows(h_hbm, tok_ref, rows, 1, 0, x_buf.at[slot], sems.at[slot], wait=True)
        x = x_buf[slot]
        x = (x * lax.rsqrt(jnp.mean(x * x, axis=-1, keepdims=True) + EPS) * g_ref[...]).astype(BF16)
        a = jnp.dot(x, wg_buf[cur], preferred_element_type=F32)
        u = jnp.dot(x, wu_buf[cur], preferred_element_type=F32)
        hmid = (a * jax.nn.sigmoid(a) * u).astype(BF16)
        y_ref[...] = jnp.dot(hmid, wd_buf[cur], preferred_element_type=F32)

    @pl.when(i >= n_used)
    def _():
        y_ref[...] = jnp.zeros(y_ref.shape, y_ref.dtype)


def moe_experts(h, norm_g, tok_buf, sched, w_gate, w_up, w_down):
    n, d = h.shape
    blk_e = sched[0]
    n_blocks = blk_e.shape[0]
    de = w_gate.shape[2]
    sr = min(W_STAGE_ROWS, d)
    assert d % sr == 0 and d % de == 0 and sr % (d // de) == 0 and de % (sr // (d // de)) == 0
    tok_blocks = tok_buf.reshape(n_blocks, 1, MOE_BLOCK)
    n_pre = len(sched)
    grid_spec = pltpu.PrefetchScalarGridSpec(
        num_scalar_prefetch=n_pre,
        grid=(n_blocks,),
        in_specs=[pl.BlockSpec((1, 1, MOE_BLOCK), lambda i, *_: (i, 0, 0), memory_space=pltpu.SMEM),
                  pl.BlockSpec((1, 1, MOE_BLOCK), lambda i, *_: (jnp.minimum(i + 1, n_blocks - 1), 0, 0),
                               memory_space=pltpu.SMEM),
                  pl.BlockSpec(memory_space=pl.ANY),
                  pl.BlockSpec((1, d), lambda i, *_: (0, 0)),
                  pl.BlockSpec(memory_space=pl.ANY),
                  pl.BlockSpec(memory_space=pl.ANY),
                  pl.BlockSpec(memory_space=pl.ANY)],
        out_specs=pl.BlockSpec((MOE_BLOCK, d), lambda i, *_: (i, 0)),
        scratch_shapes=[pltpu.VMEM((2, MOE_BLOCK, d), F32), pltpu.SemaphoreType.DMA((2,)),
                        pltpu.VMEM((W_STAGE_SLOTS, sr, de), F32), pltpu.SemaphoreType.DMA((W_STAGE_SLOTS,)),
                        pltpu.VMEM((2, d, de), BF16), pltpu.VMEM((2, d, de), BF16),
                        pltpu.VMEM((2, de, d), BF16), pltpu.SMEM((1,), jnp.int32)],
    )
    scratch_bytes = (2 * _nbytes((MOE_BLOCK, d), F32) + W_STAGE_SLOTS * _nbytes((sr, de), F32)
                     + 6 * _nbytes((d, de), BF16))
    return pl.pallas_call(
        _moe_expert_kernel,
        grid_spec=grid_spec,
        out_shape=jax.ShapeDtypeStruct((n_blocks * MOE_BLOCK, d), F32),
        compiler_params=pltpu.CompilerParams(
            dimension_semantics=("arbitrary",),
            vmem_limit_bytes=_vmem_limit([_nbytes((MOE_BLOCK, d), F32)],
                                         scratch_bytes + 4 * _nbytes((MOE_BLOCK, d), F32))),
        name="moe_experts",
    )(*sched, tok_blocks, tok_blocks, h, norm_g.reshape(1, d).astype(F32), w_gate, w_up, w_down)


def _combine_kernel(dest_ref, dest_next_ref, h_ref, w_ref, y_hbm, o_ref, buf, sems):
    i = pl.program_id(0)
    rows = buf.shape[2]
    slot = i % 2

    def gather(ids_ref, s, wait):
        for k in range(TOP_K):
            _gather_rows(y_hbm, ids_ref, rows, TOP_K, k, buf.at[s, k], sems.at[s], wait)

    @pl.when(i == 0)
    def _():
        gather(dest_ref, 0, False)

    @pl.when(i + 1 < pl.num_programs(0))
    def _():
        gather(dest_next_ref, 1 - slot, False)

    gather(dest_ref, slot, True)
    w = w_ref[...]
    o_ref[...] = h_ref[...] + (buf[slot, 0] * w[:, 0:1] + buf[slot, 1] * w[:, 1:2])


def moe_combine(h, y_buf, dest, weights, tm=128):
    n, d = h.shape
    tm = _tile(n, tm)
    nt = n // tm
    dest_blocks = dest.reshape(nt, 1, TOP_K * tm)
    return pl.pallas_call(
        _combine_kernel,
        grid=(nt,),
        in_specs=[pl.BlockSpec((1, 1, TOP_K * tm), lambda i: (i, 0, 0), memory_space=pltpu.SMEM),
                  pl.BlockSpec((1, 1, TOP_K * tm), lambda i: (jnp.minimum(i + 1, nt - 1), 0, 0),
                               memory_space=pltpu.SMEM),
                  pl.BlockSpec((tm, d), lambda i: (i, 0)),
                  pl.BlockSpec((tm, LANES), lambda i: (i, 0)),
                  pl.BlockSpec(memory_space=pl.ANY)],
        out_specs=pl.BlockSpec((tm, d), lambda i: (i, 0)),
        out_shape=jax.ShapeDtypeStruct((n, d), F32),
        scratch_shapes=[pltpu.VMEM((2, TOP_K, tm, d), F32), pltpu.SemaphoreType.DMA((2,))],
        compiler_params=pltpu.CompilerParams(
            dimension_semantics=("arbitrary",),
            vmem_limit_bytes=_vmem_limit([2 * _nbytes((tm, d), F32)], 8 * _nbytes((tm, d), F32))),
        name="moe_combine",
    )(dest_blocks, dest_blocks, h, weights, y_buf)


def _moe_dispatch(expert_id, n_experts):
    n = expert_id.shape[0]
    nk = n * TOP_K
    flat_e = expert_id.reshape(nk).astype(jnp.int32)
    order = jnp.argsort(flat_e).astype(jnp.int32)
    rank = jnp.argsort(order).astype(jnp.int32)
    counts = jnp.bincount(flat_e, length=n_experts).astype(jnp.int32)
    start = jnp.cumsum(counts) - counts
    padded = (counts + MOE_BLOCK - 1) // MOE_BLOCK * MOE_BLOCK
    pad_end = jnp.cumsum(padded)
    pad_start = pad_end - padded
    n_blocks = (nk + n_experts * (MOE_BLOCK - 1) + MOE_BLOCK - 1) // MOE_BLOCK
    blk_start = jnp.arange(n_blocks, dtype=jnp.int32) * MOE_BLOCK
    blk_e = jnp.minimum(jnp.sum(pad_end[None, :] <= blk_start[:, None], axis=1), n_experts - 1).astype(jnp.int32)
    dest = (pad_start[flat_e] + (rank - start[flat_e])).astype(jnp.int32)
    slot_e = jnp.repeat(blk_e, MOE_BLOCK)
    within = jnp.arange(n_blocks * MOE_BLOCK, dtype=jnp.int32) - pad_start[slot_e]
    src = jnp.clip(start[slot_e] + within, 0, nk - 1)
    tok_buf = jnp.where((within >= 0) & (within < counts[slot_e]), order[src] // TOP_K, 0).astype(jnp.int32)
    n_used = (pad_end[-1] // MOE_BLOCK).astype(jnp.int32)
    blk = jnp.arange(n_blocks, dtype=jnp.int32)
    used = blk < n_used
    is_first = used & (blk_e != jnp.concatenate([jnp.full((1,), -1, jnp.int32), blk_e[:-1]]))
    is_last = used & ((blk == n_used - 1) | (blk_e != jnp.concatenate([blk_e[1:], jnp.full((1,), -1, jnp.int32)])))
    parity = (jnp.cumsum(is_first.astype(jnp.int32)) - 1) % 2
    live = jnp.where(counts > 0, jnp.arange(n_experts, dtype=jnp.int32), n_experts)
    next_live = jnp.concatenate([lax.cummin(live, reverse=True)[1:], jnp.full((1,), n_experts, jnp.int32)])
    next_e = jnp.where(next_live[blk_e] < n_experts, next_live[blk_e], -1)
    sched = (blk_e, n_used.reshape(1), is_first.astype(jnp.int32), is_last.astype(jnp.int32),
             next_e.astype(jnp.int32), parity.astype(jnp.int32))
    return tok_buf, sched, dest


def kernel(x, mem, norm1_g, norm2_g, mem_norm_g, w_in, b_gate, fox_f_bias, fox_q_g, fox_k_g,
           hgrn_lb_logits, hgrn_out_g, mem_q_g, mem_k_g, w_mem_kv, w_branch_fox, w_branch_hgrn,
           w_branch_mem, w_out, w_group, b_group, w_router, b_router, w_exp_gate, w_exp_up,
           w_exp_down):
    batch, seq, d = x.shape
    n_mem = mem.shape[1]
    depth = w_in.shape[0]
    fox_heads, fox_hd = fox_f_bias.shape[1], fox_q_g.shape[1]
    fox_w = fox_heads * fox_hd
    hgrn_w = hgrn_lb_logits.shape[1]
    hgrn_heads = hgrn_w // HGRN_EXPAND
    hgrn_dv = hgrn_out_g.shape[1]
    mem_hd = mem_q_g.shape[1]
    mem_w = w_branch_mem.shape[1]
    mem_heads = mem_w // mem_hd
    n_experts = w_router.shape[2]
    m = batch * seq

    sizes = (fox_w, fox_w, fox_w, fox_heads, hgrn_w, hgrn_w, hgrn_w, hgrn_w, mem_w, 3 * d)
    offs = [0]
    for sz in sizes:
        offs.append(offs[-1] + sz)
    (o_fq, o_fk, o_fv, o_ff, o_hq, o_hf, o_hi, o_hg, o_mq, o_gt, o_end) = offs

    lb_all = jnp.cumsum(jax.nn.softmax(hgrn_lb_logits.astype(F32), axis=0), axis=0)
    mem_n = rmsnorm_rows(mem.reshape(batch * n_mem, d), mem_norm_g, BF16)

    h = x.reshape(m, d)
    for layer in range(depth):
        wl = w_in[layer]
        cols = lambda a, b: wl[:, a:b].astype(BF16)
        hn = rmsnorm_rows(h, norm1_g[layer], BF16)

        qk_gain = jnp.concatenate([jnp.tile(fox_q_g[layer] * (LOG2E * fox_hd ** -0.5), fox_heads),
                                   jnp.tile(fox_k_g[layer], fox_heads)])
        qk = matmul(hn, cols(o_fq, o_fv), BF16, functools.partial(_epi_headnorm, head_dim=fox_hd),
                    row_vec=qk_gain, name="proj_fox_qk")
        v_t = matmul_t(hn, wl[:, o_fv:o_ff].T.astype(BF16), BF16, tm=FOX_BLOCK, name="proj_fox_vt")
        w_plain = jnp.concatenate([wl[:, o_hq:o_hf], wl[:, o_hi:o_mq]], axis=1).astype(BF16)
        plain = matmul(hn, w_plain, BF16, name="proj_plain")
        hd_blk = lambda width: width // HGRN_EXPAND
        c_hq = 0
        c_hi = c_hq + hd_blk(hgrn_w)
        c_hg = c_hi + hd_blk(hgrn_w)
        z = matmul(hn, cols(o_hf, o_hi), F32, name="proj_hgrn_f")
        w_ff = jnp.pad(wl[:, o_ff:o_hq], ((0, 0), (0, LANES - fox_heads))).astype(BF16)
        ff = matmul(hn, w_ff, F32, tn=LANES, name="proj_fox_f")
        mq = matmul(hn, cols(o_mq, o_gt), BF16, functools.partial(_epi_headnorm, head_dim=mem_hd),
                    row_vec=jnp.tile(mem_q_g[layer] * mem_hd ** -0.5, mem_heads), tn=mem_hd,
                    name="proj_mem_q")
        gates = matmul(hn, cols(o_gt, o_end), BF16, _epi_sigmoid_bias, row_vec=b_gate[layer],
                       name="proj_gates")

        kbias = fox_key_bias(ff, fox_f_bias[layer], batch, fox_heads, fox_hd)
        o_a = fox_attention(qk, kbias, v_t, batch, fox_heads, fox_hd)

        o_b = hgrn2(plain, c_hq, c_hi, c_hg, z, lb_all[layer], hgrn_out_g[layer], batch, hgrn_heads)

        wkv = w_mem_kv[layer]
        mk = matmul(mem_n, wkv[:, :mem_w].astype(BF16), BF16,
                    functools.partial(_epi_headnorm, head_dim=mem_hd),
                    row_vec=jnp.tile(mem_k_g[layer], mem_heads), tn=mem_hd, name="proj_mem_k")
        mv = matmul(mem_n, wkv[:, mem_w:].astype(BF16), BF16, name="proj_mem_v")
        o_c = mem_attention(mq, mk, mv, batch, mem_heads, mem_hd)

        merged = gated_merge(o_a, o_b, o_c, w_branch_fox[layer].astype(BF16),
                             w_branch_hgrn[layer].astype(BF16), w_branch_mem[layer].astype(BF16), gates)
        h = matmul(merged, w_out[layer].astype(BF16), F32, _epi_residual, residual=h, tn=512,
                   name="proj_out")

        expert_id, weights = moe_router(h, norm2_g[layer], w_group[layer], b_group[layer],
                                        w_router[layer], b_router[layer])
        tok_buf, sched, dest = _moe_dispatch(expert_id, n_experts)
        y_buf = moe_experts(h, norm2_g[layer], tok_buf, sched, w_exp_gate[layer].astype(F32),
                            w_exp_up[layer].astype(F32), w_exp_down[layer].astype(F32))
        h = moe_combine(h, y_buf, dest, weights)
    return h.reshape(batch, seq, d)
```

```python
import functools

import jax
import jax.numpy as jnp
from jax import lax
from jax.experimental import pallas as pl
from jax.experimental.pallas import tpu as pltpu

F32 = jnp.float32
BF16 = jnp.bfloat16

EPS = 1e-6
CHUNK = 64
SUB = 16
HGRN_EXPAND = 128
TOP_K = 2
MOE_BLOCK = 128
FOX_BLOCK = 512
FOX_GROUP = 4
LANES = 128
LOG2E = 1.4426950408889634
V7X_VMEM_LIMIT_CAP = 60000 * 1024


def _vmem_limit(block_bytes, temp_bytes=0):
    need = int(1.25 * (2 * sum(block_bytes) + temp_bytes)) + (2 << 20)
    return min(max(need, 16 << 20), V7X_VMEM_LIMIT_CAP)


def _nbytes(shape, dtype):
    n = 1
    for s in shape:
        n *= s
    return n * jnp.dtype(dtype).itemsize


def _tile(dim, want):
    if dim <= want:
        return dim
    for t in range(want - want % LANES, 0, -LANES):
        if dim % t == 0:
            return t
    raise ValueError((dim, want))


def _rmsnorm_kernel(x_ref, g_ref, o_ref):
    x = x_ref[...].astype(F32)
    y = x * lax.rsqrt(jnp.mean(x * x, axis=-1, keepdims=True) + EPS)
    o_ref[...] = (y * g_ref[...]).astype(o_ref.dtype)


def rmsnorm_rows(x, g, out_dtype, tm=256):
    m, d = x.shape
    tm = _tile(m, tm)
    return pl.pallas_call(
        _rmsnorm_kernel,
        grid=(m // tm,),
        in_specs=[pl.BlockSpec((tm, d), lambda i: (i, 0)),
                  pl.BlockSpec((1, d), lambda i: (0, 0))],
        out_specs=pl.BlockSpec((tm, d), lambda i: (i, 0)),
        out_shape=jax.ShapeDtypeStruct((m, d), out_dtype),
        compiler_params=pltpu.CompilerParams(
            dimension_semantics=("parallel",),
            vmem_limit_bytes=_vmem_limit([_nbytes((tm, d), x.dtype), _nbytes((tm, d), out_dtype)],
                                         2 * _nbytes((tm, d), F32))),
        name="rmsnorm_rows",
    )(x, g.reshape(1, d).astype(F32))


def _mm_kernel(a_ref, w_ref, *refs, epilogue):
    o_ref = refs[-1]
    acc = jnp.dot(a_ref[...], w_ref[...], preferred_element_type=F32)
    epilogue(acc, o_ref, *refs[:-1])


def _epi_plain(acc, o_ref):
    o_ref[...] = acc.astype(o_ref.dtype)


def _epi_headnorm(acc, o_ref, g_ref, *, head_dim):
    for h in range(acc.shape[1] // head_dim):
        sl = slice(h * head_dim, (h + 1) * head_dim)
        blk = acc[:, sl]
        y = blk * lax.rsqrt(jnp.mean(blk * blk, axis=-1, keepdims=True) + EPS)
        o_ref[:, sl] = (y * g_ref[:, sl]).astype(o_ref.dtype)


def _epi_sigmoid_bias(acc, o_ref, b_ref):
    o_ref[...] = jax.nn.sigmoid(acc + b_ref[...]).astype(o_ref.dtype)


def _epi_residual(acc, o_ref, r_ref):
    o_ref[...] = (r_ref[...] + acc).astype(o_ref.dtype)


def matmul(a, w, out_dtype, epilogue=_epi_plain, row_vec=None, residual=None, tm=1024, tn=1024,
           name="matmul"):
    m, k = a.shape
    n = w.shape[1]
    tm, tn = _tile(m, tm), _tile(n, tn)
    in_specs = [pl.BlockSpec((tm, k), lambda i, j: (i, 0)),
                pl.BlockSpec((k, tn), lambda i, j: (0, j))]
    operands = [a, w]
    blocks = [_nbytes((tm, k), a.dtype), _nbytes((k, tn), w.dtype), _nbytes((tm, tn), out_dtype)]
    if row_vec is not None:
        in_specs.append(pl.BlockSpec((1, tn), lambda i, j: (0, j)))
        operands.append(row_vec.reshape(1, n).astype(F32))
    if residual is not None:
        in_specs.append(pl.BlockSpec((tm, tn), lambda i, j: (i, j)))
        operands.append(residual)
        blocks.append(_nbytes((tm, tn), residual.dtype))
    return pl.pallas_call(
        functools.partial(_mm_kernel, epilogue=epilogue),
        grid=(m // tm, n // tn),
        in_specs=in_specs,
        out_specs=pl.BlockSpec((tm, tn), lambda i, j: (i, j)),
        out_shape=jax.ShapeDtypeStruct((m, n), out_dtype),
        compiler_params=pltpu.CompilerParams(
            dimension_semantics=("parallel", "parallel"),
            vmem_limit_bytes=_vmem_limit(blocks, 2 * _nbytes((tm, tn), F32))),
        name=name,
    )(*operands)


def _mm_t_kernel(wt_ref, a_ref, o_ref):
    o_ref[0] = lax.dot_general(wt_ref[...], a_ref[...], (((1,), (1,)), ((), ())),
                               preferred_element_type=F32).astype(o_ref.dtype)


def matmul_t(a, w_t, out_dtype, tm=512, tn=1024, name="matmul_t"):
    m, k = a.shape
    n = w_t.shape[0]
    tm, tn = _tile(m, tm), _tile(n, tn)
    return pl.pallas_call(
        _mm_t_kernel,
        grid=(m // tm, n // tn),
        in_specs=[pl.BlockSpec((tn, k), lambda i, j: (j, 0)),
                  pl.BlockSpec((tm, k), lambda i, j: (i, 0))],
        out_specs=pl.BlockSpec((1, tn, tm), lambda i, j: (i, j, 0)),
        out_shape=jax.ShapeDtypeStruct((m // tm, n, tm), out_dtype),
        compiler_params=pltpu.CompilerParams(
            dimension_semantics=("parallel", "parallel"),
            vmem_limit_bytes=_vmem_limit([_nbytes((tm, k), a.dtype), _nbytes((tn, k), w_t.dtype),
                                          _nbytes((tn, tm), out_dtype)], 2 * _nbytes((tn, tm), F32))),
        name=name,
    )(w_t, a)


N_SPLIT = 3


def _fox_bias_kernel(f_ref, b_ref, e_ref, o_ref, carry):
    @pl.when(pl.program_id(1) == 0)
    def _():
        carry[...] = jnp.zeros(carry.shape, F32)

    x = f_ref[...] + b_ref[...]
    a = jnp.minimum(x, 0.0) - jnp.log(1.0 + jnp.exp(-jnp.abs(x)))
    r = a.shape[0]
    row = lax.broadcasted_iota(jnp.int32, a.shape, 0)
    sh = 1
    while sh < r:
        a = a + jnp.where(row >= sh, pltpu.roll(a, sh, axis=0), 0.0)
        sh *= 2
    c = a + carry[...]
    carry[...] = c[r - 1:r, :]
    rest = c * (-LOG2E)
    pieces = []
    for _ in range(N_SPLIT):
        piece = rest.astype(BF16)
        pieces.append(piece)
        rest = rest - piece.astype(F32)
    o_ref[...] = jnp.dot(jnp.concatenate(pieces, axis=1), e_ref[...],
                         preferred_element_type=F32).astype(o_ref.dtype)


def fox_key_bias(ff, bias, batch, heads, head_dim, rows=1024):
    m = ff.shape[0]
    s = m // batch
    rows = _tile(s, rows)
    nr = s // rows
    src = jnp.arange(N_SPLIT * LANES)
    dst = (src % LANES) * head_dim + src // LANES
    place = ((jnp.arange(heads * head_dim)[None, :] == dst[:, None])
             & ((src % LANES) < heads)[:, None]).astype(BF16)
    bias_row = jnp.zeros((1, LANES), F32).at[0, :heads].set(bias.astype(F32))
    return pl.pallas_call(
        _fox_bias_kernel,
        grid=(batch, nr),
        in_specs=[pl.BlockSpec((rows, LANES), lambda b, t: (b * nr + t, 0)),
                  pl.BlockSpec((1, LANES), lambda b, t: (0, 0)),
                  pl.BlockSpec((N_SPLIT * LANES, heads * head_dim), lambda b, t: (0, 0))],
        out_specs=pl.BlockSpec((rows, heads * head_dim), lambda b, t: (b * nr + t, 0)),
        out_shape=jax.ShapeDtypeStruct((m, heads * head_dim), BF16),
        scratch_shapes=[pltpu.VMEM((1, LANES), F32)],
        compiler_params=pltpu.CompilerParams(
            dimension_semantics=("parallel", "arbitrary"),
            vmem_limit_bytes=_vmem_limit([_nbytes((rows, heads * head_dim), BF16),
                                          _nbytes((N_SPLIT * LANES, heads * head_dim), BF16)],
                                         _nbytes((rows, heads * head_dim), F32) + 16 * _nbytes((rows, LANES), F32))),
        name="fox_key_bias",
    )(ff, bias_row, place)


SUM_ROWS = 16


KIND_PLAIN, KIND_DIAG, KIND_PAD = 0, 1, 2


def _fox_attn_kernel(row_ref, col_ref, kind_ref, start_ref, q_ref, k_ref, kb_ref, vt_ref, o_ref,
                     qt_scr, mask_scr, s_scr, m_scr, acc_scr, snap_scr, *, n_steps):
    nq, hd2, t = qt_scr.shape
    hd = hd2 // 2

    causal = (lax.broadcasted_iota(jnp.int32, (t, t), 0) <= lax.broadcasted_iota(jnp.int32, (t, t), 1))
    mask_scr[KIND_PLAIN] = jnp.zeros((t, t), F32)
    mask_scr[KIND_DIAG] = jnp.where(causal, 0.0, -jnp.inf)
    mask_scr[KIND_PAD] = jnp.full((t, t), -jnp.inf, F32)

    for r in range(nq):
        qt_scr[r, :hd, :] = q_ref[r * t:(r + 1) * t, :].astype(F32).T.astype(BF16)
        qt_scr[r, hd:, :] = jnp.where(lax.broadcasted_iota(jnp.int32, (hd, t), 0) < N_SPLIT,
                                      1.0, 0.0).astype(BF16)
    m_scr[...] = jnp.zeros(m_scr.shape, F32)
    acc_scr[...] = jnp.zeros(acc_scr.shape, F32)

    def scores(p):
        off = pl.multiple_of(col_ref[p] * t, t)
        k_aug = jnp.concatenate([k_ref[pl.ds(off, t), :], kb_ref[pl.ds(off, t), :]], axis=1)
        return jnp.dot(k_aug, qt_scr[row_ref[p]], preferred_element_type=F32)

    def update(s, p):
        s = s + mask_scr[kind_ref[p]]
        m_prev = jnp.where(start_ref[p] == 1, -jnp.inf, m_scr[...])
        m_new = jnp.maximum(m_prev, jnp.max(s, axis=0, keepdims=True))
        alpha = jnp.exp2(m_prev - m_new)
        prob = jnp.exp2((s - m_new).astype(BF16))
        vt_aug = jnp.concatenate([vt_ref[0, col_ref[p]], jnp.ones((SUM_ROWS, t), BF16)], axis=0)
        acc = alpha * acc_scr[...] + jnp.dot(vt_aug, prob, preferred_element_type=F32)
        m_scr[...] = m_new
        acc_scr[...] = acc
        return acc

    def write_row(p, acc_ref):
        @pl.when(kind_ref[p] == KIND_DIAG)
        def _():
            out = acc_ref[:hd, :] * (1.0 / acc_ref[hd:hd + 1, :])
            o_ref[pl.ds(pl.multiple_of(row_ref[p] * t, t), t), :] = out.T.astype(o_ref.dtype)

    group = s_scr.shape[0]
    s_scr[0] = scores(0)

    def pair_group(u, carry):
        p = group * u
        for k in range(group):
            s_scr[(k + 1) % group] = scores(p + k + 1)
            acc = update(s_scr[k], p + k)
            if k + 1 < group:
                snap_scr[k] = acc
        for k in range(group):
            write_row(p + k, snap_scr.at[k] if k + 1 < group else acc_scr)
        return carry

    lax.fori_loop(0, n_steps, pair_group, 0)


def fox_attention(qk, kbias, v_t, batch, heads, head_dim):
    m = qk.shape[0]
    s = m // batch
    t = v_t.shape[2]
    nq = s // t
    pairs = [(i, j, KIND_DIAG if i == j else KIND_PLAIN, int(j == 0)) for i in range(nq) for j in range(i + 1)]
    n_steps = -(-len(pairs) // FOX_GROUP)
    pairs += [(nq - 1, nq - 1, KIND_PAD, 0)] * (FOX_GROUP * n_steps + 1 - len(pairs))
    tables = [jnp.asarray(v, jnp.int32) for v in zip(*pairs)]
    grid_spec = pltpu.PrefetchScalarGridSpec(
        num_scalar_prefetch=len(tables),
        grid=(batch, heads),
        in_specs=[pl.BlockSpec((s, head_dim), lambda b, h, *_: (b, h)),
                  pl.BlockSpec((s, head_dim), lambda b, h, *_: (b, heads + h)),
                  pl.BlockSpec((s, head_dim), lambda b, h, *_: (b, h)),
                  pl.BlockSpec((1, nq, head_dim, t), lambda b, h, *_: (b, 0, h, 0))],
        out_specs=pl.BlockSpec((s, head_dim), lambda b, h, *_: (b, h)),
        scratch_shapes=[pltpu.VMEM((nq, 2 * head_dim, t), BF16), pltpu.VMEM((3, t, t), F32),
                        pltpu.VMEM((FOX_GROUP, t, t), F32),
                        pltpu.VMEM((1, t), F32), pltpu.VMEM((head_dim + SUM_ROWS, t), F32),
                        pltpu.VMEM((FOX_GROUP - 1, head_dim + SUM_ROWS, t), F32)],
    )
    return pl.pallas_call(
        functools.partial(_fox_attn_kernel, n_steps=n_steps),
        grid_spec=grid_spec,
        out_shape=jax.ShapeDtypeStruct((m, heads * head_dim), BF16),
        compiler_params=pltpu.CompilerParams(
            dimension_semantics=("parallel", "parallel"),
            vmem_limit_bytes=_vmem_limit([5 * _nbytes((s, head_dim), BF16)],
                                         2 * _nbytes((s, head_dim), BF16) + 10 * _nbytes((t, t), F32))),
        name="fox_attention",
    )(*tables, qk, qk, kbias, v_t.reshape(batch, nq, heads * head_dim, t))


def _hgrn_kernel(q_ref, z_ref, v_ref, hg_ref, lb_ref, og_ref, o_ref, st_ref):
    @pl.when(pl.program_id(2) == 0)
    def _():
        st_ref[...] = jnp.zeros(st_ref.shape, F32)

    t, dk = z_ref.shape
    dv = v_ref.shape[1]
    nc, nb, nsub = t // CHUNK, t // SUB, CHUNK // SUB

    q = q_ref[...].astype(F32)
    z = z_ref[...]
    lb = lb_ref[...]
    g = jnp.log(lb + (1.0 - lb) * jax.nn.sigmoid(z))
    kk = (1.0 - lb) * jax.nn.sigmoid(-z)

    row = lax.broadcasted_iota(jnp.int32, (t, dk), 0)
    rin = row % CHUNK
    b = g
    sh = 1
    while sh < CHUNK:
        b = b + jnp.where(rin >= sh, pltpu.roll(b, sh, axis=0), 0.0)
        sh *= 2

    b3 = b.reshape(nc, CHUNK, dk)
    q3 = q.reshape(nc, CHUNK, dk)
    k3 = kk.reshape(nc, CHUNK, dk)
    v3 = v_ref[...].reshape(nc, CHUNK, dv)
    b_last = b3[:, CHUNK - 1:CHUNK, :]
    q_in = (q3 * jnp.exp(b3)).astype(BF16)
    k_out = (k3 * jnp.exp(b_last - b3)).astype(BF16)
    d_last = jnp.exp(b_last)

    st = st_ref[...]
    inter = []
    for c in range(nc):
        inter.append(lax.dot_general(q_in[c], st.astype(BF16), (((1,), (1,)), ((), ())),
                                     preferred_element_type=F32))
        kv_t = lax.dot_general(v3[c], k_out[c], (((0,), (0,)), ((), ())),
                               preferred_element_type=F32)
        st = st * d_last[c] + kv_t
    st_ref[...] = st
    inter = jnp.concatenate(inter, axis=0)

    rsub_c = (rin // SUB).reshape(nc, CHUNK, dk)
    q_parts, k_parts = [], []
    for blk in range(1, nsub):
        b_ref_pt = b3[:, blk * SUB - 1:blk * SUB, :]
        q_hat = q3 * jnp.exp(jnp.minimum(b3 - b_ref_pt, 0.0))
        q_parts.append(jnp.where(rsub_c == blk, q_hat, 0.0).astype(BF16))
        k_parts.append((k3 * jnp.exp(jnp.minimum(b_ref_pt - b3, 0.0))).astype(BF16))
    q_cat = jnp.concatenate(q_parts, axis=2)
    k_cat = jnp.concatenate(k_parts, axis=2)
    s_off = lax.dot_general(q_cat, k_cat, (((2,), (2,)), ((0,), (0,))),
                            preferred_element_type=F32)

    b4 = b.reshape(nb, SUB, dk)
    q4 = q.reshape(nb, SUB, dk)
    k4 = kk.reshape(nb, SUB, dk)
    rsub = (row % SUB).reshape(nb, SUB, dk)
    u_row = lax.broadcasted_iota(jnp.int32, (dk, LANES), 1)
    half = SUB // 2
    s_lo = jnp.zeros((t, LANES), F32)
    s_hi = jnp.zeros((t // 2, LANES), F32)
    for j in range(SUB):
        r0 = 0 if j < half else half
        bj, kj = b4[:, j:j + 1, :], k4[:, j:j + 1, :]
        decay = jnp.exp(jnp.where(rsub[:, r0:, :] >= j, b4[:, r0:, :] - bj, -jnp.inf))
        p = (q4[:, r0:, :] * (kj * decay)).reshape(-1, dk).astype(BF16)
        place = jnp.where((u_row % SUB == j) & (u_row < CHUNK), 1.0, 0.0).astype(BF16)
        if j < half:
            s_lo = s_lo + jnp.dot(p, place, preferred_element_type=F32)
        else:
            s_hi = s_hi + jnp.dot(p, place, preferred_element_type=F32)
    s_diag = s_lo.reshape(nb, SUB, LANES) + jnp.concatenate(
        [jnp.zeros((nb, half, LANES), F32), s_hi.reshape(nb, half, LANES)], axis=1)
    s_diag = s_diag.reshape(nc, CHUNK, LANES)[:, :, :CHUNK]

    rb = lax.broadcasted_iota(jnp.int32, (nc, CHUNK, CHUNK), 1) // SUB
    cb = lax.broadcasted_iota(jnp.int32, (nc, CHUNK, CHUNK), 2) // SUB
    scores = jnp.where(cb == rb, s_diag, jnp.where(cb < rb, s_off, 0.0)).astype(BF16)
    intra = lax.dot_general(scores, v3, (((2,), (1,)), ((0,), (0,))),
                            preferred_element_type=F32).reshape(t, dv)

    o = inter + intra
    o = o * lax.rsqrt(jnp.mean(o * o, axis=-1, keepdims=True) + EPS) * og_ref[...]
    hg = hg_ref[...].astype(F32)
    o_ref[...] = (o * (hg * jax.nn.sigmoid(hg))).astype(o_ref.dtype)


def hgrn2(src, q_col0, v_col0, hg_col0, z, lb, out_g, batch, heads, rows=512):
    m = z.shape[0]
    s = m // batch
    dk = HGRN_EXPAND
    dv = out_g.shape[-1]
    rows = _tile(s, rows)
    nt = s // rows
    row_blk = lambda b, h, t: b * nt + t
    return pl.pallas_call(
        _hgrn_kernel,
        grid=(batch, heads, nt),
        in_specs=[pl.BlockSpec((rows, dk), lambda b, h, t: (row_blk(b, h, t), q_col0 + h)),
                  pl.BlockSpec((rows, dk), lambda b, h, t: (row_blk(b, h, t), h)),
                  pl.BlockSpec((rows, dv), lambda b, h, t: (row_blk(b, h, t), v_col0 + h)),
                  pl.BlockSpec((rows, dv), lambda b, h, t: (row_blk(b, h, t), hg_col0 + h)),
                  pl.BlockSpec((1, dk), lambda b, h, t: (0, h)),
                  pl.BlockSpec((1, dv), lambda b, h, t: (0, 0))],
        out_specs=pl.BlockSpec((rows, dv), lambda b, h, t: (row_blk(b, h, t), h)),
        out_shape=jax.ShapeDtypeStruct((m, heads * dv), BF16),
        scratch_shapes=[pltpu.VMEM((dv, dk), F32)],
        compiler_params=pltpu.CompilerParams(
            dimension_semantics=("parallel", "parallel", "arbitrary"),
            vmem_limit_bytes=_vmem_limit([6 * _nbytes((rows, dk), F32)], 40 * _nbytes((rows, dk), F32))),
        name="hgrn2",
    )(src, z, src, src, lb.reshape(1, -1).astype(F32), out_g.reshape(1, dv).astype(F32))


def _mem_attn_kernel(q_ref, k_ref, v_ref, o_ref):
    s = lax.dot_general(q_ref[...], k_ref[...], (((1,), (1,)), ((), ())), preferred_element_type=F32)
    m = jnp.max(s, axis=1, keepdims=True)
    p = jnp.exp(s - m)
    l = jnp.sum(p, axis=1, keepdims=True)
    o = jnp.dot(p.astype(BF16), v_ref[...], preferred_element_type=F32)
    o_ref[...] = (o / l).astype(o_ref.dtype)


def mem_attention(q, mk, mv, batch, heads, head_dim, tq=1024):
    m = q.shape[0]
    s = m // batch
    n_mem = mk.shape[0] // batch
    tq = _tile(s, tq)
    nq = s // tq
    return pl.pallas_call(
        _mem_attn_kernel,
        grid=(batch, heads, nq),
        in_specs=[pl.BlockSpec((tq, head_dim), lambda b, h, i: (b * nq + i, h)),
                  pl.BlockSpec((n_mem, head_dim), lambda b, h, i: (b, h)),
                  pl.BlockSpec((n_mem, head_dim), lambda b, h, i: (b, h))],
        out_specs=pl.BlockSpec((tq, head_dim), lambda b, h, i: (b * nq + i, h)),
        out_shape=jax.ShapeDtypeStruct((m, heads * head_dim), BF16),
        compiler_params=pltpu.CompilerParams(
            dimension_semantics=("parallel", "parallel", "parallel"),
            vmem_limit_bytes=_vmem_limit([2 * _nbytes((tq, head_dim), BF16), 2 * _nbytes((n_mem, head_dim), BF16)],
                                         4 * _nbytes((tq, head_dim), F32))),
        name="mem_attention",
    )(q, mk, mv)


def _merge_kernel(a_ref, b_ref, c_ref, wa_ref, wb_ref, wc_ref, ga_ref, gb_ref, gc_ref, o_ref):
    acc = ga_ref[...].astype(F32) * jnp.dot(a_ref[...], wa_ref[...], preferred_element_type=F32)
    acc = acc + gb_ref[...].astype(F32) * jnp.dot(b_ref[...], wb_ref[...], preferred_element_type=F32)
    acc = acc + gc_ref[...].astype(F32) * jnp.dot(c_ref[...], wc_ref[...], preferred_element_type=F32)
    o_ref[...] = acc.astype(o_ref.dtype)


def gated_merge(o_a, o_b, o_c, w_a, w_b, w_c, gates, tm=1024, tn=512):
    m = o_a.shape[0]
    d = w_a.shape[1]
    tm, tn = _tile(m, tm), _tile(d, tn)
    nj = d // tn
    a_spec = lambda arr: pl.BlockSpec((tm, arr.shape[1]), lambda i, j: (i, 0))
    w_spec = lambda arr: pl.BlockSpec((arr.shape[0], tn), lambda i, j: (0, j))
    g_spec = lambda r: pl.BlockSpec((tm, tn), lambda i, j: (i, r * nj + j))
    blocks = ([_nbytes((tm, a.shape[1]), BF16) for a in (o_a, o_b, o_c)]
              + [_nbytes((w.shape[0], tn), BF16) for w in (w_a, w_b, w_c)] + 4 * [_nbytes((tm, tn), BF16)])
    return pl.pallas_call(
        _merge_kernel,
        grid=(m // tm, nj),
        in_specs=[a_spec(o_a), a_spec(o_b), a_spec(o_c), w_spec(w_a), w_spec(w_b), w_spec(w_c),
                  g_spec(0), g_spec(1), g_spec(2)],
        out_specs=pl.BlockSpec((tm, tn), lambda i, j: (i, j)),
        out_shape=jax.ShapeDtypeStruct((m, d), BF16),
        compiler_params=pltpu.CompilerParams(
            dimension_semantics=("parallel", "parallel"),
            vmem_limit_bytes=_vmem_limit(blocks, 3 * _nbytes((tm, tn), F32))),
        name="gated_merge",
    )(o_a, o_b, o_c, w_a, w_b, w_c, gates, gates, gates)


def _router_kernel(h_ref, g_ref, whi_ref, wlo_ref, b_ref, lg_ref, id_ref, wt_ref, *, n_groups):
    x = h_ref[...]
    hn = x * lax.rsqrt(jnp.mean(x * x, axis=-1, keepdims=True) + EPS) * g_ref[...]
    hn_hi = hn.astype(BF16)
    hn_lo = (hn - hn_hi.astype(F32)).astype(BF16)
    logits = (jnp.dot(hn_hi, whi_ref[...], preferred_element_type=F32)
              + (jnp.dot(hn_hi, wlo_ref[...], preferred_element_type=F32)
                 + jnp.dot(hn_lo, whi_ref[...], preferred_element_type=F32))) + b_ref[...]
    lane = lax.broadcasted_iota(jnp.int32, logits.shape, 1).astype(F32)
    lane_grp = lg_ref[...]
    neg = -jnp.inf
    big = float(LANES)

    gl = jnp.where(lane < n_groups, logits, neg)
    gmax = jnp.max(gl, axis=1, keepdims=True)
    grp = jnp.min(jnp.where(gl == gmax, lane, big), axis=1, keepdims=True)
    p_grp = 1.0 / jnp.sum(jnp.exp(gl - gmax), axis=1, keepdims=True)

    el = jnp.where(lane_grp == grp, logits, neg)
    v1 = jnp.max(el, axis=1, keepdims=True)
    i1 = jnp.min(jnp.where(el == v1, lane, big), axis=1, keepdims=True)
    el2 = jnp.where(lane == i1, neg, el)
    v2 = jnp.max(el2, axis=1, keepdims=True)
    i2 = jnp.min(jnp.where(el2 == v2, lane, big), axis=1, keepdims=True)
    e21 = jnp.exp(v2 - v1)
    w1 = p_grp / (1.0 + e21)
    w2 = p_grp * e21 / (1.0 + e21)

    id_ref[...] = jnp.where(lane == 0.0, i1 - n_groups, i2 - n_groups).astype(jnp.int32)
    wt_ref[...] = jnp.where(lane == 0.0, w1, w2)


def moe_router(h, norm_g, w_group, b_group, w_router, b_router, tm=256):
    n, d = h.shape
    n_groups = w_group.shape[1]
    n_experts = w_router.shape[1]
    per_group = n_experts // n_groups
    assert n_groups + n_experts <= LANES
    pad = LANES - n_groups - n_experts
    w = jnp.concatenate([w_group, w_router, jnp.zeros((d, pad), F32)], axis=1).astype(F32)
    bias = jnp.concatenate([b_group, b_router, jnp.zeros((pad,), F32)]).reshape(1, LANES).astype(F32)
    w_hi = w.astype(BF16)
    w_lo = (w - w_hi.astype(F32)).astype(BF16)
    lane = jnp.arange(LANES)
    lane_grp = jnp.where((lane >= n_groups) & (lane < n_groups + n_experts),
                         (lane - n_groups) // per_group, -1).astype(F32).reshape(1, LANES)
    tm = _tile(n, tm)
    ids, wts = pl.pallas_call(
        functools.partial(_router_kernel, n_groups=n_groups),
        grid=(n // tm,),
        in_specs=[pl.BlockSpec((tm, d), lambda i: (i, 0)),
                  pl.BlockSpec((1, d), lambda i: (0, 0)),
                  pl.BlockSpec((d, LANES), lambda i: (0, 0)),
                  pl.BlockSpec((d, LANES), lambda i: (0, 0)),
                  pl.BlockSpec((1, LANES), lambda i: (0, 0)),
                  pl.BlockSpec((1, LANES), lambda i: (0, 0))],
        out_specs=[pl.BlockSpec((tm, LANES), lambda i: (i, 0)),
                   pl.BlockSpec((tm, LANES), lambda i: (i, 0))],
        out_shape=[jax.ShapeDtypeStruct((n, LANES), jnp.int32),
                   jax.ShapeDtypeStruct((n, LANES), F32)],
        compiler_params=pltpu.CompilerParams(
            dimension_semantics=("parallel",),
            vmem_limit_bytes=_vmem_limit([_nbytes((tm, d), F32), _nbytes((d, LANES), F32)],
                                         8 * _nbytes((tm, d), F32))),
        name="moe_router",
    )(h, norm_g.reshape(1, d).astype(F32), w_hi, w_lo, bias, lane_grp)
    return ids[:, :TOP_K], wts


def _row_copy(src_hbm, row, dst, slot, sem):
    return pltpu.make_async_copy(src_hbm.at[pl.ds(row, 1)], dst.at[pl.ds(slot, 1)], sem)


GATHER_UNROLL = 8


def _gather_rows(src_hbm, ids_ref, n_ids, stride, offset, dst, sem, wait):
    def body(r, c):
        cp = _row_copy(src_hbm, ids_ref[0, 0, stride * r + offset], dst, r, sem)
        if wait:
            cp.wait()
        else:
            cp.start()
        return c

    lax.fori_loop(0, n_ids, body, 0, unroll=GATHER_UNROLL)


W_STAGE_ROWS = 512
W_STAGE_SLOTS = 8


class _ExpertWeights:
    def __init__(self, hbm, stage, sems, bufs):
        self.hbm, self.stage, self.sems, self.bufs = hbm, stage, sems, bufs
        d, de = hbm[0].shape[1], hbm[0].shape[2]
        self.de = de
        self.sr = stage.shape[1]
        self.slots = stage.shape[0]
        self.ratio = d // de
        self.rb = self.sr // self.ratio
        self.n_a = d // self.sr
        self.n_chunks = 2 * self.n_a + de // self.rb

    def _each_copy(self, idx, e, fn):
        slot = idx % self.slots
        sr, n_a = self.sr, self.n_a
        for mat in range(2):
            @pl.when((idx >= mat * n_a) & (idx < (mat + 1) * n_a))
            def _(mat=mat):
                r0 = pl.multiple_of((idx - mat * n_a) * sr, sr)
                fn(pltpu.make_async_copy(self.hbm[mat].at[e, pl.ds(r0, sr), :], self.stage.at[slot],
                                         self.sems.at[slot]))

        @pl.when(idx >= 2 * n_a)
        def _():
            r0 = pl.multiple_of((idx - 2 * n_a) * self.rb, self.rb)
            for k in range(self.ratio):
                fn(pltpu.make_async_copy(self.hbm[2].at[e, pl.ds(r0, self.rb), pl.ds(k * self.de, self.de)],
                                         self.stage.at[slot, pl.ds(k * self.rb, self.rb), :],
                                         self.sems.at[slot]))

    def start(self, idx, e):
        self._each_copy(idx, e, lambda cp: cp.start())

    def wait(self, idx, e):
        self._each_copy(idx, e, lambda cp: cp.wait())

    def convert(self, idx, dst):
        slot = idx % self.slots
        sr, n_a = self.sr, self.n_a
        for mat in range(2):
            @pl.when((idx >= mat * n_a) & (idx < (mat + 1) * n_a))
            def _(mat=mat):
                r0 = pl.multiple_of((idx - mat * n_a) * sr, sr)
                self.bufs[mat][dst, pl.ds(r0, sr), :] = self.stage[slot].astype(BF16)

        @pl.when(idx >= 2 * n_a)
        def _():
            r0 = pl.multiple_of((idx - 2 * n_a) * self.rb, self.rb)
            for k in range(self.ratio):
                self.bufs[2][dst, pl.ds(r0, self.rb), pl.ds(k * self.de, self.de)] = (
                    self.stage[slot, pl.ds(k * self.rb, self.rb), :].astype(BF16))

    def start_first(self, e):
        for idx in range(min(self.slots, self.n_chunks)):
            self.start(jnp.int32(idx), e)

    def stream(self, e, dst, first, count):
        def body(c, carry):
            idx = first + c
            self.wait(idx, e)
            self.convert(idx, dst)

            @pl.when(idx + self.slots < self.n_chunks)
            def _():
                self.start(idx + self.slots, e)
            return carry

        lax.fori_loop(0, count, body, 0)


def _moe_expert_kernel(be_ref, nused_ref, pos_ref, nexte_ref, par_ref, c0_ref, cn_ref, tok_ref, tok_next_ref,
                       h_hbm, g_ref, wg_hbm, wu_hbm, wd_hbm, y_ref, x_buf, sems, stage, wsems,
                       wg_buf, wu_buf, wd_buf):
    i = pl.program_id(0)
    n_used = nused_ref[0]
    rows = x_buf.shape[1]
    slot = i % 2
    cur = par_ref[i]
    next_e = nexte_ref[i]
    w = _ExpertWeights((wg_hbm, wu_hbm, wd_hbm), stage, wsems, (wg_buf, wu_buf, wd_buf))

    @pl.when((i == 0) & (n_used > 0))
    def _():
        _gather_rows(h_hbm, tok_ref, rows, 1, 0, x_buf.at[0], sems.at[0], wait=False)
        w.start_first(be_ref[0])
        w.stream(be_ref[0], cur, 0, w.n_chunks)

    @pl.when(i + 1 < n_used)
    def _():
        _gather_rows(h_hbm, tok_next_ref, rows, 1, 0, x_buf.at[1 - slot], sems.at[1 - slot], wait=False)

    @pl.when((i < n_used) & (next_e >= 0))
    def _():
        @pl.when(pos_ref[i] == 0)
        def _():
            w.start_first(next_e)

        w.stream(next_e, 1 - cur, c0_ref[i], cn_ref[i])

    @pl.when(i < n_used)
    def _():
        _gather_rows(h_hbm, tok_ref, rows, 1, 0, x_buf.at[slot], sems.at[slot], wait=True)
        x = x_buf[slot]
        x = (x * lax.rsqrt(jnp.mean(x * x, axis=-1, keepdims=True) + EPS) * g_ref[...]).astype(BF16)
        a = jnp.dot(x, wg_buf[cur], preferred_element_type=F32)
        u = jnp.dot(x, wu_buf[cur], preferred_element_type=F32)
        hmid = (a * jax.nn.sigmoid(a) * u).astype(BF16)
        y_ref[...] = jnp.dot(hmid, wd_buf[cur], preferred_element_type=F32)

    @pl.when(i >= n_used)
    def _():
        y_ref[...] = jnp.zeros(y_ref.shape, y_ref.dtype)


def moe_experts(h, norm_g, tok_buf, sched, w_gate, w_up, w_down):
    n, d = h.shape
    blk_e, n_used, run_pos, run_len, next_e, parity = sched
    n_blocks = blk_e.shape[0]
    de = w_gate.shape[2]
    sr = min(W_STAGE_ROWS, d)
    assert d % sr == 0 and d % de == 0 and sr % (d // de) == 0 and de % (sr // (d // de)) == 0
    n_chunks = 2 * (d // sr) + de // (sr // (d // de))
    spread = jnp.maximum(run_len - 1, 1)
    k = jnp.maximum(run_pos - 1, 0)
    base, extra = n_chunks // spread, n_chunks % spread
    single = run_len == 1
    chunk0 = jnp.where(single, 0, k * base + jnp.minimum(k, extra)).astype(jnp.int32)
    chunk_n = jnp.where(single, n_chunks,
                        jnp.where(run_pos == 0, 0, base + (k < extra))).astype(jnp.int32)
    sched = (blk_e, n_used, run_pos, next_e, parity, chunk0, chunk_n)
    tok_blocks = tok_buf.reshape(n_blocks, 1, MOE_BLOCK)
    n_pre = len(sched)
    grid_spec = pltpu.PrefetchScalarGridSpec(
        num_scalar_prefetch=n_pre,
        grid=(n_blocks,),
        in_specs=[pl.BlockSpec((1, 1, MOE_BLOCK), lambda i, *_: (i, 0, 0), memory_space=pltpu.SMEM),
                  pl.BlockSpec((1, 1, MOE_BLOCK), lambda i, *_: (jnp.minimum(i + 1, n_blocks - 1), 0, 0),
                               memory_space=pltpu.SMEM),
                  pl.BlockSpec(memory_space=pl.ANY),
                  pl.BlockSpec((1, d), lambda i, *_: (0, 0)),
                  pl.BlockSpec(memory_space=pl.ANY),
                  pl.BlockSpec(memory_space=pl.ANY),
                  pl.BlockSpec(memory_space=pl.ANY)],
        out_specs=pl.BlockSpec((MOE_BLOCK, d), lambda i, *_: (i, 0)),
        scratch_shapes=[pltpu.VMEM((2, MOE_BLOCK, d), F32), pltpu.SemaphoreType.DMA((2,)),
                        pltpu.VMEM((W_STAGE_SLOTS, sr, de), F32), pltpu.SemaphoreType.DMA((W_STAGE_SLOTS,)),
                        pltpu.VMEM((2, d, de), BF16), pltpu.VMEM((2, d, de), BF16),
                        pltpu.VMEM((2, de, d), BF16)],
    )
    scratch_bytes = (2 * _nbytes((MOE_BLOCK, d), F32) + W_STAGE_SLOTS * _nbytes((sr, de), F32)
                     + 6 * _nbytes((d, de), BF16))
    return pl.pallas_call(
        _moe_expert_kernel,
        grid_spec=grid_spec,
        out_shape=jax.ShapeDtypeStruct((n_blocks * MOE_BLOCK, d), F32),
        compiler_params=pltpu.CompilerParams(
            dimension_semantics=("arbitrary",),
            vmem_limit_bytes=_vmem_limit([_nbytes((MOE_BLOCK, d), F32)],
                                         scratch_bytes + 4 * _nbytes((MOE_BLOCK, d), F32))),
        name="moe_experts",
    )(*sched, tok_blocks, tok_blocks, h, norm_g.reshape(1, d).astype(F32), w_gate, w_up, w_down)


def _combine_kernel(dest_ref, dest_next_ref, h_ref, w_ref, y_hbm, o_ref, buf, sems):
    i = pl.program_id(0)
    rows = buf.shape[2]
    slot = i % 2

    def gather(ids_ref, s, wait):
        for k in range(TOP_K):
            _gather_rows(y_hbm, ids_ref, rows, TOP_K, k, buf.at[s, k], sems.at[s], wait)

    @pl.when(i == 0)
    def _():
        gather(dest_ref, 0, False)

    @pl.when(i + 1 < pl.num_programs(0))
    def _():
        gather(dest_next_ref, 1 - slot, False)

    gather(dest_ref, slot, True)
    w = w_ref[...]
    o_ref[...] = h_ref[...] + (buf[slot, 0] * w[:, 0:1] + buf[slot, 1] * w[:, 1:2])


def moe_combine(h, y_buf, dest, weights, tm=128):
    n, d = h.shape
    tm = _tile(n, tm)
    nt = n // tm
    dest_blocks = dest.reshape(nt, 1, TOP_K * tm)
    return pl.pallas_call(
        _combine_kernel,
        grid=(nt,),
        in_specs=[pl.BlockSpec((1, 1, TOP_K * tm), lambda i: (i, 0, 0), memory_space=pltpu.SMEM),
                  pl.BlockSpec((1, 1, TOP_K * tm), lambda i: (jnp.minimum(i + 1, nt - 1), 0, 0),
                               memory_space=pltpu.SMEM),
                  pl.BlockSpec((tm, d), lambda i: (i, 0)),
                  pl.BlockSpec((tm, LANES), lambda i: (i, 0)),
                  pl.BlockSpec(memory_space=pl.ANY)],
        out_specs=pl.BlockSpec((tm, d), lambda i: (i, 0)),
        out_shape=jax.ShapeDtypeStruct((n, d), F32),
        scratch_shapes=[pltpu.VMEM((2, TOP_K, tm, d), F32), pltpu.SemaphoreType.DMA((2,))],
        compiler_params=pltpu.CompilerParams(
            dimension_semantics=("arbitrary",),
            vmem_limit_bytes=_vmem_limit([2 * _nbytes((tm, d), F32)], 8 * _nbytes((tm, d), F32))),
        name="moe_combine",
    )(dest_blocks, dest_blocks, h, weights, y_buf)


def _moe_dispatch(expert_id, n_experts):
    n = expert_id.shape[0]
    nk = n * TOP_K
    flat_e = expert_id.reshape(nk).astype(jnp.int32)
    order = jnp.argsort(flat_e).astype(jnp.int32)
    rank = jnp.argsort(order).astype(jnp.int32)
    counts = jnp.bincount(flat_e, length=n_experts).astype(jnp.int32)
    start = jnp.cumsum(counts) - counts
    padded = (counts + MOE_BLOCK - 1) // MOE_BLOCK * MOE_BLOCK
    pad_end = jnp.cumsum(padded)
    pad_start = pad_end - padded
    n_blocks = (nk + n_experts * (MOE_BLOCK - 1) + MOE_BLOCK - 1) // MOE_BLOCK
    blk_start = jnp.arange(n_blocks, dtype=jnp.int32) * MOE_BLOCK
    blk_e = jnp.minimum(jnp.sum(pad_end[None, :] <= blk_start[:, None], axis=1), n_experts - 1).astype(jnp.int32)
    dest = (pad_start[flat_e] + (rank - start[flat_e])).astype(jnp.int32)
    slot_e = jnp.repeat(blk_e, MOE_BLOCK)
    within = jnp.arange(n_blocks * MOE_BLOCK, dtype=jnp.int32) - pad_start[slot_e]
    src = jnp.clip(start[slot_e] + within, 0, nk - 1)
    tok_buf = jnp.where((within >= 0) & (within < counts[slot_e]), order[src] // TOP_K, 0).astype(jnp.int32)
    n_used = (pad_end[-1] // MOE_BLOCK).astype(jnp.int32)
    blk = jnp.arange(n_blocks, dtype=jnp.int32)
    used = blk < n_used
    is_first = used & (blk_e != jnp.concatenate([jnp.full((1,), -1, jnp.int32), blk_e[:-1]]))
    parity = (jnp.cumsum(is_first.astype(jnp.int32)) - 1) % 2
    run_pos = blk - pad_start[blk_e] // MOE_BLOCK
    run_len = padded[blk_e] // MOE_BLOCK
    live = jnp.where(counts > 0, jnp.arange(n_experts, dtype=jnp.int32), n_experts)
    next_live = jnp.concatenate([lax.cummin(live, reverse=True)[1:], jnp.full((1,), n_experts, jnp.int32)])
    next_e = jnp.where(next_live[blk_e] < n_experts, next_live[blk_e], -1)
    sched = (blk_e, n_used.reshape(1), run_pos.astype(jnp.int32), run_len.astype(jnp.int32),
             next_e.astype(jnp.int32), parity.astype(jnp.int32))
    return tok_buf, sched, dest


def kernel(x, mem, norm1_g, norm2_g, mem_norm_g, w_in, b_gate, fox_f_bias, fox_q_g, fox_k_g,
           hgrn_lb_logits, hgrn_out_g, mem_q_g, mem_k_g, w_mem_kv, w_branch_fox, w_branch_hgrn,
           w_branch_mem, w_out, w_group, b_group, w_router, b_router, w_exp_gate, w_exp_up,
           w_exp_down):
    batch, seq, d = x.shape
    n_mem = mem.shape[1]
    depth = w_in.shape[0]
    fox_heads, fox_hd = fox_f_bias.shape[1], fox_q_g.shape[1]
    fox_w = fox_heads * fox_hd
    hgrn_w = hgrn_lb_logits.shape[1]
    hgrn_heads = hgrn_w // HGRN_EXPAND
    hgrn_dv = hgrn_out_g.shape[1]
    mem_hd = mem_q_g.shape[1]
    mem_w = w_branch_mem.shape[1]
    mem_heads = mem_w // mem_hd
    n_experts = w_router.shape[2]
    m = batch * seq

    sizes = (fox_w, fox_w, fox_w, fox_heads, hgrn_w, hgrn_w, hgrn_w, hgrn_w, mem_w, 3 * d)
    offs = [0]
    for sz in sizes:
        offs.append(offs[-1] + sz)
    (o_fq, o_fk, o_fv, o_ff, o_hq, o_hf, o_hi, o_hg, o_mq, o_gt, o_end) = offs

    lb_all = jnp.cumsum(jax.nn.softmax(hgrn_lb_logits.astype(F32), axis=0), axis=0)
    mem_n = rmsnorm_rows(mem.reshape(batch * n_mem, d), mem_norm_g, BF16)

    h = x.reshape(m, d)
    for layer in range(depth):
        wl = w_in[layer]
        cols = lambda a, b: wl[:, a:b].astype(BF16)
        hn = rmsnorm_rows(h, norm1_g[layer], BF16)

        qk_gain = jnp.concatenate([jnp.tile(fox_q_g[layer] * (LOG2E * fox_hd ** -0.5), fox_heads),
                                   jnp.tile(fox_k_g[layer], fox_heads)])
        qk = matmul(hn, cols(o_fq, o_fv), BF16, functools.partial(_epi_headnorm, head_dim=fox_hd),
                    row_vec=qk_gain, name="proj_fox_qk")
        v_t = matmul_t(hn, wl[:, o_fv:o_ff].T.astype(BF16), BF16, tm=FOX_BLOCK, name="proj_fox_vt")
        w_plain = jnp.concatenate([wl[:, o_hq:o_hf], wl[:, o_hi:o_mq]], axis=1).astype(BF16)
        plain = matmul(hn, w_plain, BF16, name="proj_plain")
        hd_blk = lambda width: width // HGRN_EXPAND
        c_hq = 0
        c_hi = c_hq + hd_blk(hgrn_w)
        c_hg = c_hi + hd_blk(hgrn_w)
        z = matmul(hn, cols(o_hf, o_hi), F32, name="proj_hgrn_f")
        w_ff = jnp.pad(wl[:, o_ff:o_hq], ((0, 0), (0, LANES - fox_heads))).astype(BF16)
        ff = matmul(hn, w_ff, F32, tn=LANES, name="proj_fox_f")
        mq = matmul(hn, cols(o_mq, o_gt), BF16, functools.partial(_epi_headnorm, head_dim=mem_hd),
                    row_vec=jnp.tile(mem_q_g[layer] * mem_hd ** -0.5, mem_heads), tn=mem_hd,
                    name="proj_mem_q")
        gates = matmul(hn, cols(o_gt, o_end), BF16, _epi_sigmoid_bias, row_vec=b_gate[layer],
                       name="proj_gates")

        kbias = fox_key_bias(ff, fox_f_bias[layer], batch, fox_heads, fox_hd)
        o_a = fox_attention(qk, kbias, v_t, batch, fox_heads, fox_hd)

        o_b = hgrn2(plain, c_hq, c_hi, c_hg, z, lb_all[layer], hgrn_out_g[layer], batch, hgrn_heads)

        wkv = w_mem_kv[layer]
        mk = matmul(mem_n, wkv[:, :mem_w].astype(BF16), BF16,
                    functools.partial(_epi_headnorm, head_dim=mem_hd),
                    row_vec=jnp.tile(mem_k_g[layer], mem_heads), tn=mem_hd, name="proj_mem_k")
        mv = matmul(mem_n, wkv[:, mem_w:].astype(BF16), BF16, name="proj_mem_v")
        o_c = mem_attention(mq, mk, mv, batch, mem_heads, mem_hd)

        merged = gated_merge(o_a, o_b, o_c, w_branch_fox[layer].astype(BF16),
                             w_branch_hgrn[layer].astype(BF16), w_branch_mem[layer].astype(BF16), gates)
        h = matmul(merged, w_out[layer].astype(BF16), F32, _epi_residual, residual=h, tn=512,
                   name="proj_out")

        expert_id, weights = moe_router(h, norm2_g[layer], w_group[layer], b_group[layer],
                                        w_router[layer], b_router[layer])
        tok_buf, sched, dest = _moe_dispatch(expert_id, n_experts)
        y_buf = moe_experts(h, norm2_g[layer], tok_buf, sched, w_exp_gate[layer].astype(F32),
                            w_exp_up[layer].astype(F32), w_exp_down[layer].astype(F32))
        h = moe_combine(h, y_buf, dest, weights)
    return h.reshape(batch, seq, d)
```

```python
import functools

import jax
import jax.numpy as jnp
from jax import lax
from jax.experimental import pallas as pl
from jax.experimental.pallas import tpu as pltpu

F32 = jnp.float32
BF16 = jnp.bfloat16

EPS = 1e-6
CHUNK = 64
SUB = 16
HGRN_EXPAND = 128
TOP_K = 2
MOE_BLOCK = 128
FOX_BLOCK = 512
FOX_GROUP = 4
LANES = 128
LOG2E = 1.4426950408889634
V7X_VMEM_LIMIT_CAP = 60000 * 1024


def _vmem_limit(block_bytes, temp_bytes=0):
    need = int(1.25 * (2 * sum(block_bytes) + temp_bytes)) + (2 << 20)
    return min(max(need, 16 << 20), V7X_VMEM_LIMIT_CAP)


def _nbytes(shape, dtype):
    n = 1
    for s in shape:
        n *= s
    return n * jnp.dtype(dtype).itemsize


def _tile(dim, want):
    if dim <= want:
        return dim
    for t in range(want - want % LANES, 0, -LANES):
        if dim % t == 0:
            return t
    raise ValueError((dim, want))


def _rmsnorm_kernel(x_ref, g_ref, o_ref):
    x = x_ref[...].astype(F32)
    y = x * lax.rsqrt(jnp.mean(x * x, axis=-1, keepdims=True) + EPS)
    o_ref[...] = (y * g_ref[...]).astype(o_ref.dtype)


def rmsnorm_rows(x, g, out_dtype, tm=256):
    m, d = x.shape
    tm = _tile(m, tm)
    return pl.pallas_call(
        _rmsnorm_kernel,
        grid=(m // tm,),
        in_specs=[pl.BlockSpec((tm, d), lambda i: (i, 0)),
                  pl.BlockSpec((1, d), lambda i: (0, 0))],
        out_specs=pl.BlockSpec((tm, d), lambda i: (i, 0)),
        out_shape=jax.ShapeDtypeStruct((m, d), out_dtype),
        compiler_params=pltpu.CompilerParams(
            dimension_semantics=("parallel",),
            vmem_limit_bytes=_vmem_limit([_nbytes((tm, d), x.dtype), _nbytes((tm, d), out_dtype)],
                                         2 * _nbytes((tm, d), F32))),
        name="rmsnorm_rows",
    )(x, g.reshape(1, d).astype(F32))


def _mm_kernel(a_ref, w_ref, *refs, epilogue):
    o_ref = refs[-1]
    acc = jnp.dot(a_ref[...], w_ref[...], preferred_element_type=F32)
    epilogue(acc, o_ref, *refs[:-1])


def _epi_plain(acc, o_ref):
    o_ref[...] = acc.astype(o_ref.dtype)


def _epi_headnorm(acc, o_ref, g_ref, *, head_dim):
    for h in range(acc.shape[1] // head_dim):
        sl = slice(h * head_dim, (h + 1) * head_dim)
        blk = acc[:, sl]
        y = blk * lax.rsqrt(jnp.mean(blk * blk, axis=-1, keepdims=True) + EPS)
        o_ref[:, sl] = (y * g_ref[:, sl]).astype(o_ref.dtype)


def _epi_sigmoid_bias(acc, o_ref, b_ref):
    o_ref[...] = jax.nn.sigmoid(acc + b_ref[...]).astype(o_ref.dtype)


def _epi_residual(acc, o_ref, r_ref):
    o_ref[...] = (r_ref[...] + acc).astype(o_ref.dtype)


def matmul(a, w, out_dtype, epilogue=_epi_plain, row_vec=None, residual=None, tm=1024, tn=1024,
           name="matmul"):
    m, k = a.shape
    n = w.shape[1]
    tm, tn = _tile(m, tm), _tile(n, tn)
    in_specs = [pl.BlockSpec((tm, k), lambda i, j: (i, 0)),
                pl.BlockSpec((k, tn), lambda i, j: (0, j))]
    operands = [a, w]
    blocks = [_nbytes((tm, k), a.dtype), _nbytes((k, tn), w.dtype), _nbytes((tm, tn), out_dtype)]
    if row_vec is not None:
        in_specs.append(pl.BlockSpec((1, tn), lambda i, j: (0, j)))
        operands.append(row_vec.reshape(1, n).astype(F32))
    if residual is not None:
        in_specs.append(pl.BlockSpec((tm, tn), lambda i, j: (i, j)))
        operands.append(residual)
        blocks.append(_nbytes((tm, tn), residual.dtype))
    return pl.pallas_call(
        functools.partial(_mm_kernel, epilogue=epilogue),
        grid=(m // tm, n // tn),
        in_specs=in_specs,
        out_specs=pl.BlockSpec((tm, tn), lambda i, j: (i, j)),
        out_shape=jax.ShapeDtypeStruct((m, n), out_dtype),
        compiler_params=pltpu.CompilerParams(
            dimension_semantics=("parallel", "parallel"),
            vmem_limit_bytes=_vmem_limit(blocks, 2 * _nbytes((tm, tn), F32))),
        name=name,
    )(*operands)


def _mm_t_kernel(wt_ref, a_ref, o_ref):
    o_ref[0] = lax.dot_general(wt_ref[...], a_ref[...], (((1,), (1,)), ((), ())),
                               preferred_element_type=F32).astype(o_ref.dtype)


def matmul_t(a, w_t, out_dtype, tm=512, tn=1024, name="matmul_t"):
    m, k = a.shape
    n = w_t.shape[0]
    tm, tn = _tile(m, tm), _tile(n, tn)
    return pl.pallas_call(
        _mm_t_kernel,
        grid=(m // tm, n // tn),
        in_specs=[pl.BlockSpec((tn, k), lambda i, j: (j, 0)),
                  pl.BlockSpec((tm, k), lambda i, j: (i, 0))],
        out_specs=pl.BlockSpec((1, tn, tm), lambda i, j: (i, j, 0)),
        out_shape=jax.ShapeDtypeStruct((m // tm, n, tm), out_dtype),
        compiler_params=pltpu.CompilerParams(
            dimension_semantics=("parallel", "parallel"),
            vmem_limit_bytes=_vmem_limit([_nbytes((tm, k), a.dtype), _nbytes((tn, k), w_t.dtype),
                                          _nbytes((tn, tm), out_dtype)], 2 * _nbytes((tn, tm), F32))),
        name=name,
    )(w_t, a)


def _w_in_split_kernel(w_ref, qk_ref, vt_ref, ff_ref, plain_ref, hf_ref, mq_ref, gates_ref, *,
                       o_fv, o_ff, n_ff, moves):
    r, p = w_ref.shape
    qk_ref[...] = w_ref[:, :o_fv].astype(BF16)
    vt_ref[...] = w_ref[:, o_fv:o_ff].T.astype(BF16)
    lane = lax.broadcasted_iota(jnp.int32, (r, LANES), 1)
    ff_ref[...] = jnp.where(lane < n_ff, w_ref[:, o_ff:o_ff + LANES], 0.0).astype(BF16)
    sel = (lax.broadcasted_iota(jnp.int32, (2 * LANES, LANES), 0)
           == lax.broadcasted_iota(jnp.int32, (2 * LANES, LANES), 1) + n_ff).astype(BF16)
    outs = (plain_ref, hf_ref, mq_ref, gates_ref)
    for src, width, which, dst in moves:
        for k in range(width // LANES):
            a = o_ff + src + k * LANES
            if a + 2 * LANES <= p:
                slab = w_ref[:, a:a + 2 * LANES]
            else:
                slab = jnp.concatenate([w_ref[:, a:a + LANES], w_ref[:, a + LANES:p],
                                        jnp.zeros((r, a + 2 * LANES - p), F32)], axis=1)
            tile = jnp.dot(slab.astype(BF16), sel, preferred_element_type=F32)
            outs[which][:, dst + k * LANES:dst + (k + 1) * LANES] = tile.astype(BF16)


def split_w_in(wl, o_fv, o_ff, n_ff, widths, rows=128):
    d, p = wl.shape
    hgrn_w, mem_w, gates_w = widths
    assert o_ff % LANES == 0 and 0 < n_ff < LANES and p == o_ff + n_ff + 4 * hgrn_w + mem_w + gates_w
    assert all(w % LANES == 0 for w in widths)
    rows = _tile(d, rows)
    moves = ((0, hgrn_w, 0, 0), (hgrn_w, hgrn_w, 1, 0), (2 * hgrn_w, hgrn_w, 0, hgrn_w),
             (3 * hgrn_w, hgrn_w, 0, 2 * hgrn_w), (4 * hgrn_w, mem_w, 2, 0), (4 * hgrn_w + mem_w, gates_w, 3, 0))
    out_cols = (o_fv, None, LANES, 3 * hgrn_w, hgrn_w, mem_w, gates_w)
    out_shape, out_specs = [], []
    for c in out_cols:
        if c is None:
            out_shape.append(jax.ShapeDtypeStruct((o_ff - o_fv, d), BF16))
            out_specs.append(pl.BlockSpec((o_ff - o_fv, rows), lambda i: (0, i)))
        else:
            out_shape.append(jax.ShapeDtypeStruct((d, c), BF16))
            out_specs.append(pl.BlockSpec((rows, c), lambda i: (i, 0)))
    return pl.pallas_call(
        functools.partial(_w_in_split_kernel, o_fv=o_fv, o_ff=o_ff, n_ff=n_ff, moves=moves),
        grid=(d // rows,),
        in_specs=[pl.BlockSpec((rows, p), lambda i: (i, 0))],
        out_specs=out_specs,
        out_shape=out_shape,
        compiler_params=pltpu.CompilerParams(
            dimension_semantics=("parallel",),
            vmem_limit_bytes=_vmem_limit([_nbytes((rows, p), F32), _nbytes((rows, p + LANES), BF16)],
                                         _nbytes((rows, o_ff - o_fv), F32))),
        name="split_w_in",
    )(wl)


N_SPLIT = 3


def _fox_bias_kernel(f_ref, b_ref, e_ref, o_ref, carry):
    @pl.when(pl.program_id(1) == 0)
    def _():
        carry[...] = jnp.zeros(carry.shape, F32)

    x = f_ref[...] + b_ref[...]
    a = jnp.minimum(x, 0.0) - jnp.log(1.0 + jnp.exp(-jnp.abs(x)))
    r = a.shape[0]
    row = lax.broadcasted_iota(jnp.int32, a.shape, 0)
    sh = 1
    while sh < r:
        a = a + jnp.where(row >= sh, pltpu.roll(a, sh, axis=0), 0.0)
        sh *= 2
    c = a + carry[...]
    carry[...] = c[r - 1:r, :]
    rest = c * (-LOG2E)
    pieces = []
    for _ in range(N_SPLIT):
        piece = rest.astype(BF16)
        pieces.append(piece)
        rest = rest - piece.astype(F32)
    o_ref[...] = jnp.dot(jnp.concatenate(pieces, axis=1), e_ref[...],
                         preferred_element_type=F32).astype(o_ref.dtype)


def fox_key_bias(ff, bias, batch, heads, head_dim, rows=1024):
    m = ff.shape[0]
    s = m // batch
    rows = _tile(s, rows)
    nr = s // rows
    src = jnp.arange(N_SPLIT * LANES)
    dst = (src % LANES) * head_dim + src // LANES
    place = ((jnp.arange(heads * head_dim)[None, :] == dst[:, None])
             & ((src % LANES) < heads)[:, None]).astype(BF16)
    bias_row = jnp.zeros((1, LANES), F32).at[0, :heads].set(bias.astype(F32))
    return pl.pallas_call(
        _fox_bias_kernel,
        grid=(batch, nr),
        in_specs=[pl.BlockSpec((rows, LANES), lambda b, t: (b * nr + t, 0)),
                  pl.BlockSpec((1, LANES), lambda b, t: (0, 0)),
                  pl.BlockSpec((N_SPLIT * LANES, heads * head_dim), lambda b, t: (0, 0))],
        out_specs=pl.BlockSpec((rows, heads * head_dim), lambda b, t: (b * nr + t, 0)),
        out_shape=jax.ShapeDtypeStruct((m, heads * head_dim), BF16),
        scratch_shapes=[pltpu.VMEM((1, LANES), F32)],
        compiler_params=pltpu.CompilerParams(
            dimension_semantics=("parallel", "arbitrary"),
            vmem_limit_bytes=_vmem_limit([_nbytes((rows, heads * head_dim), BF16),
                                          _nbytes((N_SPLIT * LANES, heads * head_dim), BF16)],
                                         _nbytes((rows, heads * head_dim), F32) + 16 * _nbytes((rows, LANES), F32))),
        name="fox_key_bias",
    )(ff, bias_row, place)


SUM_ROWS = 16


KIND_PLAIN, KIND_DIAG, KIND_PAD = 0, 1, 2


def _fox_attn_kernel(row_ref, col_ref, kind_ref, start_ref, q_ref, k_ref, kb_ref, vt_ref, o_ref,
                     qt_scr, mask_scr, s_scr, m_scr, acc_scr, snap_scr, *, n_steps):
    nq, hd2, t = qt_scr.shape
    hd = hd2 // 2

    causal = (lax.broadcasted_iota(jnp.int32, (t, t), 0) <= lax.broadcasted_iota(jnp.int32, (t, t), 1))
    mask_scr[KIND_PLAIN] = jnp.zeros((t, t), F32)
    mask_scr[KIND_DIAG] = jnp.where(causal, 0.0, -jnp.inf)
    mask_scr[KIND_PAD] = jnp.full((t, t), -jnp.inf, F32)

    for r in range(nq):
        qt_scr[r, :hd, :] = q_ref[r * t:(r + 1) * t, :].astype(F32).T.astype(BF16)
        qt_scr[r, hd:, :] = jnp.where(lax.broadcasted_iota(jnp.int32, (hd, t), 0) < N_SPLIT,
                                      1.0, 0.0).astype(BF16)
    m_scr[...] = jnp.zeros(m_scr.shape, F32)
    acc_scr[...] = jnp.zeros(acc_scr.shape, F32)

    def scores(p):
        off = pl.multiple_of(col_ref[p] * t, t)
        k_aug = jnp.concatenate([k_ref[pl.ds(off, t), :], kb_ref[pl.ds(off, t), :]], axis=1)
        return jnp.dot(k_aug, qt_scr[row_ref[p]], preferred_element_type=F32)

    def update(s, p):
        s = s + mask_scr[kind_ref[p]]
        m_prev = jnp.where(start_ref[p] == 1, -jnp.inf, m_scr[...])
        m_new = jnp.maximum(m_prev, jnp.max(s, axis=0, keepdims=True))
        alpha = jnp.exp2(m_prev - m_new)
        prob = jnp.exp2((s - m_new).astype(BF16))
        vt_aug = jnp.concatenate([vt_ref[0, col_ref[p]], jnp.ones((SUM_ROWS, t), BF16)], axis=0)
        acc = alpha * acc_scr[...] + jnp.dot(vt_aug, prob, preferred_element_type=F32)
        m_scr[...] = m_new
        acc_scr[...] = acc
        return acc

    def write_row(p, acc_ref):
        @pl.when(kind_ref[p] == KIND_DIAG)
        def _():
            out = acc_ref[:hd, :] * (1.0 / acc_ref[hd:hd + 1, :])
            o_ref[pl.ds(pl.multiple_of(row_ref[p] * t, t), t), :] = out.T.astype(o_ref.dtype)

    group = s_scr.shape[0]
    s_scr[0] = scores(0)

    def pair_group(u, carry):
        p = group * u
        for k in range(group):
            s_scr[(k + 1) % group] = scores(p + k + 1)
            acc = update(s_scr[k], p + k)
            if k + 1 < group:
                snap_scr[k] = acc
        for k in range(group):
            write_row(p + k, snap_scr.at[k] if k + 1 < group else acc_scr)
        return carry

    lax.fori_loop(0, n_steps, pair_group, 0)


def fox_attention(qk, kbias, v_t, batch, heads, head_dim):
    m = qk.shape[0]
    s = m // batch
    t = v_t.shape[2]
    nq = s // t
    pairs = [(i, j, KIND_DIAG if i == j else KIND_PLAIN, int(j == 0)) for i in range(nq) for j in range(i + 1)]
    n_steps = -(-len(pairs) // FOX_GROUP)
    pairs += [(nq - 1, nq - 1, KIND_PAD, 0)] * (FOX_GROUP * n_steps + 1 - len(pairs))
    tables = [jnp.asarray(v, jnp.int32) for v in zip(*pairs)]
    grid_spec = pltpu.PrefetchScalarGridSpec(
        num_scalar_prefetch=len(tables),
        grid=(batch, heads),
        in_specs=[pl.BlockSpec((s, head_dim), lambda b, h, *_: (b, h)),
                  pl.BlockSpec((s, head_dim), lambda b, h, *_: (b, heads + h)),
                  pl.BlockSpec((s, head_dim), lambda b, h, *_: (b, h)),
                  pl.BlockSpec((1, nq, head_dim, t), lambda b, h, *_: (b, 0, h, 0))],
        out_specs=pl.BlockSpec((s, head_dim), lambda b, h, *_: (b, h)),
        scratch_shapes=[pltpu.VMEM((nq, 2 * head_dim, t), BF16), pltpu.VMEM((3, t, t), F32),
                        pltpu.VMEM((FOX_GROUP, t, t), F32),
                        pltpu.VMEM((1, t), F32), pltpu.VMEM((head_dim + SUM_ROWS, t), F32),
                        pltpu.VMEM((FOX_GROUP - 1, head_dim + SUM_ROWS, t), F32)],
    )
    return pl.pallas_call(
        functools.partial(_fox_attn_kernel, n_steps=n_steps),
        grid_spec=grid_spec,
        out_shape=jax.ShapeDtypeStruct((m, heads * head_dim), BF16),
        compiler_params=pltpu.CompilerParams(
            dimension_semantics=("parallel", "parallel"),
            vmem_limit_bytes=_vmem_limit([5 * _nbytes((s, head_dim), BF16)],
                                         2 * _nbytes((s, head_dim), BF16) + 10 * _nbytes((t, t), F32))),
        name="fox_attention",
    )(*tables, qk, qk, kbias, v_t.reshape(batch, nq, heads * head_dim, t))


def _hgrn_kernel(q_ref, z_ref, v_ref, hg_ref, lb_ref, og_ref, o_ref, st_ref):
    @pl.when(pl.program_id(2) == 0)
    def _():
        st_ref[...] = jnp.zeros(st_ref.shape, F32)

    t, dk = z_ref.shape
    dv = v_ref.shape[1]
    nc, nb, nsub = t // CHUNK, t // SUB, CHUNK // SUB

    q = q_ref[...].astype(F32)
    z = z_ref[...]
    lb = lb_ref[...]
    g = jnp.log(lb + (1.0 - lb) * jax.nn.sigmoid(z))
    kk = (1.0 - lb) * jax.nn.sigmoid(-z)

    row = lax.broadcasted_iota(jnp.int32, (t, dk), 0)
    rin = row % CHUNK
    b = g
    sh = 1
    while sh < CHUNK:
        b = b + jnp.where(rin >= sh, pltpu.roll(b, sh, axis=0), 0.0)
        sh *= 2

    b3 = b.reshape(nc, CHUNK, dk)
    q3 = q.reshape(nc, CHUNK, dk)
    k3 = kk.reshape(nc, CHUNK, dk)
    v3 = v_ref[...].reshape(nc, CHUNK, dv)
    b_last = b3[:, CHUNK - 1:CHUNK, :]
    q_in = (q3 * jnp.exp(b3)).astype(BF16)
    k_out = (k3 * jnp.exp(b_last - b3)).astype(BF16)
    d_last = jnp.exp(b_last)

    st = st_ref[...]
    inter = []
    for c in range(nc):
        inter.append(lax.dot_general(q_in[c], st.astype(BF16), (((1,), (1,)), ((), ())),
                                     preferred_element_type=F32))
        kv_t = lax.dot_general(v3[c], k_out[c], (((0,), (0,)), ((), ())),
                               preferred_element_type=F32)
        st = st * d_last[c] + kv_t
    st_ref[...] = st
    inter = jnp.concatenate(inter, axis=0)

    rsub_c = (rin // SUB).reshape(nc, CHUNK, dk)
    q_parts, k_parts = [], []
    for blk in range(1, nsub):
        b_ref_pt = b3[:, blk * SUB - 1:blk * SUB, :]
        q_hat = q3 * jnp.exp(jnp.minimum(b3 - b_ref_pt, 0.0))
        q_parts.append(jnp.where(rsub_c == blk, q_hat, 0.0).astype(BF16))
        k_parts.append((k3 * jnp.exp(jnp.minimum(b_ref_pt - b3, 0.0))).astype(BF16))
    q_cat = jnp.concatenate(q_parts, axis=2)
    k_cat = jnp.concatenate(k_parts, axis=2)
    s_off = lax.dot_general(q_cat, k_cat, (((2,), (2,)), ((0,), (0,))),
                            preferred_element_type=F32)

    b4 = b.reshape(nb, SUB, dk)
    q4 = q.reshape(nb, SUB, dk)
    k4 = kk.reshape(nb, SUB, dk)
    rsub = (row % SUB).reshape(nb, SUB, dk)
    u_row = lax.broadcasted_iota(jnp.int32, (dk, LANES), 1)
    half = SUB // 2
    s_lo = jnp.zeros((t, LANES), F32)
    s_hi = jnp.zeros((t // 2, LANES), F32)
    for j in range(SUB):
        r0 = 0 if j < half else half
        bj, kj = b4[:, j:j + 1, :], k4[:, j:j + 1, :]
        decay = jnp.exp(jnp.where(rsub[:, r0:, :] >= j, b4[:, r0:, :] - bj, -jnp.inf))
        p = (q4[:, r0:, :] * (kj * decay)).reshape(-1, dk).astype(BF16)
        place = jnp.where((u_row % SUB == j) & (u_row < CHUNK), 1.0, 0.0).astype(BF16)
        if j < half:
            s_lo = s_lo + jnp.dot(p, place, preferred_element_type=F32)
        else:
            s_hi = s_hi + jnp.dot(p, place, preferred_element_type=F32)
    s_diag = s_lo.reshape(nb, SUB, LANES) + jnp.concatenate(
        [jnp.zeros((nb, half, LANES), F32), s_hi.reshape(nb, half, LANES)], axis=1)
    s_diag = s_diag.reshape(nc, CHUNK, LANES)[:, :, :CHUNK]

    rb = lax.broadcasted_iota(jnp.int32, (nc, CHUNK, CHUNK), 1) // SUB
    cb = lax.broadcasted_iota(jnp.int32, (nc, CHUNK, CHUNK), 2) // SUB
    scores = jnp.where(cb == rb, s_diag, jnp.where(cb < rb, s_off, 0.0)).astype(BF16)
    intra = lax.dot_general(scores, v3, (((2,), (1,)), ((0,), (0,))),
                            preferred_element_type=F32).reshape(t, dv)

    o = inter + intra
    o = o * lax.rsqrt(jnp.mean(o * o, axis=-1, keepdims=True) + EPS) * og_ref[...]
    hg = hg_ref[...].astype(F32)
    o_ref[...] = (o * (hg * jax.nn.sigmoid(hg))).astype(o_ref.dtype)


def hgrn2(src, q_col0, v_col0, hg_col0, z, lb, out_g, batch, heads, rows=512):
    m = z.shape[0]
    s = m // batch
    dk = HGRN_EXPAND
    dv = out_g.shape[-1]
    rows = _tile(s, rows)
    nt = s // rows
    row_blk = lambda b, h, t: b * nt + t
    return pl.pallas_call(
        _hgrn_kernel,
        grid=(batch, heads, nt),
        in_specs=[pl.BlockSpec((rows, dk), lambda b, h, t: (row_blk(b, h, t), q_col0 + h)),
                  pl.BlockSpec((rows, dk), lambda b, h, t: (row_blk(b, h, t), h)),
                  pl.BlockSpec((rows, dv), lambda b, h, t: (row_blk(b, h, t), v_col0 + h)),
                  pl.BlockSpec((rows, dv), lambda b, h, t: (row_blk(b, h, t), hg_col0 + h)),
                  pl.BlockSpec((1, dk), lambda b, h, t: (0, h)),
                  pl.BlockSpec((1, dv), lambda b, h, t: (0, 0))],
        out_specs=pl.BlockSpec((rows, dv), lambda b, h, t: (row_blk(b, h, t), h)),
        out_shape=jax.ShapeDtypeStruct((m, heads * dv), BF16),
        scratch_shapes=[pltpu.VMEM((dv, dk), F32)],
        compiler_params=pltpu.CompilerParams(
            dimension_semantics=("parallel", "parallel", "arbitrary"),
            vmem_limit_bytes=_vmem_limit([6 * _nbytes((rows, dk), F32)], 40 * _nbytes((rows, dk), F32))),
        name="hgrn2",
    )(src, z, src, src, lb.reshape(1, -1).astype(F32), out_g.reshape(1, dv).astype(F32))


def _mem_attn_kernel(q_ref, k_ref, v_ref, o_ref):
    s = lax.dot_general(q_ref[...], k_ref[...], (((1,), (1,)), ((), ())), preferred_element_type=F32)
    m = jnp.max(s, axis=1, keepdims=True)
    p = jnp.exp(s - m)
    l = jnp.sum(p, axis=1, keepdims=True)
    o = jnp.dot(p.astype(BF16), v_ref[...], preferred_element_type=F32)
    o_ref[...] = (o / l).astype(o_ref.dtype)


def mem_attention(q, mk, mv, batch, heads, head_dim, tq=1024):
    m = q.shape[0]
    s = m // batch
    n_mem = mk.shape[0] // batch
    tq = _tile(s, tq)
    nq = s // tq
    return pl.pallas_call(
        _mem_attn_kernel,
        grid=(batch, heads, nq),
        in_specs=[pl.BlockSpec((tq, head_dim), lambda b, h, i: (b * nq + i, h)),
                  pl.BlockSpec((n_mem, head_dim), lambda b, h, i: (b, h)),
                  pl.BlockSpec((n_mem, head_dim), lambda b, h, i: (b, h))],
        out_specs=pl.BlockSpec((tq, head_dim), lambda b, h, i: (b * nq + i, h)),
        out_shape=jax.ShapeDtypeStruct((m, heads * head_dim), BF16),
        compiler_params=pltpu.CompilerParams(
            dimension_semantics=("parallel", "parallel", "parallel"),
            vmem_limit_bytes=_vmem_limit([2 * _nbytes((tq, head_dim), BF16), 2 * _nbytes((n_mem, head_dim), BF16)],
                                         4 * _nbytes((tq, head_dim), F32))),
        name="mem_attention",
    )(q, mk, mv)


def _merge_kernel(a_ref, b_ref, c_ref, wa_ref, wb_ref, wc_ref, ga_ref, gb_ref, gc_ref, o_ref):
    acc = ga_ref[...].astype(F32) * jnp.dot(a_ref[...], wa_ref[...], preferred_element_type=F32)
    acc = acc + gb_ref[...].astype(F32) * jnp.dot(b_ref[...], wb_ref[...], preferred_element_type=F32)
    acc = acc + gc_ref[...].astype(F32) * jnp.dot(c_ref[...], wc_ref[...], preferred_element_type=F32)
    o_ref[...] = acc.astype(o_ref.dtype)


def gated_merge(o_a, o_b, o_c, w_a, w_b, w_c, gates, tm=1024, tn=512):
    m = o_a.shape[0]
    d = w_a.shape[1]
    tm, tn = _tile(m, tm), _tile(d, tn)
    nj = d // tn
    a_spec = lambda arr: pl.BlockSpec((tm, arr.shape[1]), lambda i, j: (i, 0))
    w_spec = lambda arr: pl.BlockSpec((arr.shape[0], tn), lambda i, j: (0, j))
    g_spec = lambda r: pl.BlockSpec((tm, tn), lambda i, j: (i, r * nj + j))
    blocks = ([_nbytes((tm, a.shape[1]), BF16) for a in (o_a, o_b, o_c)]
              + [_nbytes((w.shape[0], tn), BF16) for w in (w_a, w_b, w_c)] + 4 * [_nbytes((tm, tn), BF16)])
    return pl.pallas_call(
        _merge_kernel,
        grid=(m // tm, nj),
        in_specs=[a_spec(o_a), a_spec(o_b), a_spec(o_c), w_spec(w_a), w_spec(w_b), w_spec(w_c),
                  g_spec(0), g_spec(1), g_spec(2)],
        out_specs=pl.BlockSpec((tm, tn), lambda i, j: (i, j)),
        out_shape=jax.ShapeDtypeStruct((m, d), BF16),
        compiler_params=pltpu.CompilerParams(
            dimension_semantics=("parallel", "parallel"),
            vmem_limit_bytes=_vmem_limit(blocks, 3 * _nbytes((tm, tn), F32))),
        name="gated_merge",
    )(o_a, o_b, o_c, w_a, w_b, w_c, gates, gates, gates)


def _router_kernel(h_ref, g_ref, whi_ref, wlo_ref, b_ref, lg_ref, id_ref, wt_ref, *, n_groups):
    x = h_ref[...]
    hn = x * lax.rsqrt(jnp.mean(x * x, axis=-1, keepdims=True) + EPS) * g_ref[...]
    hn_hi = hn.astype(BF16)
    hn_lo = (hn - hn_hi.astype(F32)).astype(BF16)
    logits = (jnp.dot(hn_hi, whi_ref[...], preferred_element_type=F32)
              + (jnp.dot(hn_hi, wlo_ref[...], preferred_element_type=F32)
                 + jnp.dot(hn_lo, whi_ref[...], preferred_element_type=F32))) + b_ref[...]
    lane = lax.broadcasted_iota(jnp.int32, logits.shape, 1).astype(F32)
    lane_grp = lg_ref[...]
    neg = -jnp.inf
    big = float(LANES)

    gl = jnp.where(lane < n_groups, logits, neg)
    gmax = jnp.max(gl, axis=1, keepdims=True)
    grp = jnp.min(jnp.where(gl == gmax, lane, big), axis=1, keepdims=True)
    p_grp = 1.0 / jnp.sum(jnp.exp(gl - gmax), axis=1, keepdims=True)

    el = jnp.where(lane_grp == grp, logits, neg)
    v1 = jnp.max(el, axis=1, keepdims=True)
    i1 = jnp.min(jnp.where(el == v1, lane, big), axis=1, keepdims=True)
    el2 = jnp.where(lane == i1, neg, el)
    v2 = jnp.max(el2, axis=1, keepdims=True)
    i2 = jnp.min(jnp.where(el2 == v2, lane, big), axis=1, keepdims=True)
    e21 = jnp.exp(v2 - v1)
    w1 = p_grp / (1.0 + e21)
    w2 = p_grp * e21 / (1.0 + e21)

    id_ref[...] = jnp.where(lane == 0.0, i1 - n_groups, i2 - n_groups).astype(jnp.int32)
    wt_ref[...] = jnp.where(lane == 0.0, w1, w2)


def moe_router(h, norm_g, w_group, b_group, w_router, b_router, tm=256):
    n, d = h.shape
    n_groups = w_group.shape[1]
    n_experts = w_router.shape[1]
    per_group = n_experts // n_groups
    assert n_groups + n_experts <= LANES
    pad = LANES - n_groups - n_experts
    w = jnp.concatenate([w_group, w_router, jnp.zeros((d, pad), F32)], axis=1).astype(F32)
    bias = jnp.concatenate([b_group, b_router, jnp.zeros((pad,), F32)]).reshape(1, LANES).astype(F32)
    w_hi = w.astype(BF16)
    w_lo = (w - w_hi.astype(F32)).astype(BF16)
    lane = jnp.arange(LANES)
    lane_grp = jnp.where((lane >= n_groups) & (lane < n_groups + n_experts),
                         (lane - n_groups) // per_group, -1).astype(F32).reshape(1, LANES)
    tm = _tile(n, tm)
    ids, wts = pl.pallas_call(
        functools.partial(_router_kernel, n_groups=n_groups),
        grid=(n // tm,),
        in_specs=[pl.BlockSpec((tm, d), lambda i: (i, 0)),
                  pl.BlockSpec((1, d), lambda i: (0, 0)),
                  pl.BlockSpec((d, LANES), lambda i: (0, 0)),
                  pl.BlockSpec((d, LANES), lambda i: (0, 0)),
                  pl.BlockSpec((1, LANES), lambda i: (0, 0)),
                  pl.BlockSpec((1, LANES), lambda i: (0, 0))],
        out_specs=[pl.BlockSpec((tm, LANES), lambda i: (i, 0)),
                   pl.BlockSpec((tm, LANES), lambda i: (i, 0))],
        out_shape=[jax.ShapeDtypeStruct((n, LANES), jnp.int32),
                   jax.ShapeDtypeStruct((n, LANES), F32)],
        compiler_params=pltpu.CompilerParams(
            dimension_semantics=("parallel",),
            vmem_limit_bytes=_vmem_limit([_nbytes((tm, d), F32), _nbytes((d, LANES), F32)],
                                         8 * _nbytes((tm, d), F32))),
        name="moe_router",
    )(h, norm_g.reshape(1, d).astype(F32), w_hi, w_lo, bias, lane_grp)
    return ids[:, :TOP_K], wts


def _row_copy(src_hbm, row, dst, slot, sem):
    return pltpu.make_async_copy(src_hbm.at[pl.ds(row, 1)], dst.at[pl.ds(slot, 1)], sem)


GATHER_UNROLL = 8


def _gather_rows(src_hbm, ids_ref, n_ids, stride, offset, dst, sem, wait):
    def body(r, c):
        cp = _row_copy(src_hbm, ids_ref[0, 0, stride * r + offset], dst, r, sem)
        if wait:
            cp.wait()
        else:
            cp.start()
        return c

    lax.fori_loop(0, n_ids, body, 0, unroll=GATHER_UNROLL)


W_STAGE_ROWS = 512
W_STAGE_SLOTS = 8


class _ExpertWeights:
    def __init__(self, hbm, stage, sems, bufs):
        self.hbm, self.stage, self.sems, self.bufs = hbm, stage, sems, bufs
        d, de = hbm[0].shape[1], hbm[0].shape[2]
        self.de = de
        self.sr = stage.shape[1]
        self.slots = stage.shape[0]
        self.ratio = d // de
        self.rb = self.sr // self.ratio
        self.n_a = d // self.sr
        self.n_chunks = 2 * self.n_a + de // self.rb

    def _each_copy(self, idx, e, fn):
        slot = idx % self.slots
        sr, n_a = self.sr, self.n_a
        for mat in range(2):
            @pl.when((idx >= mat * n_a) & (idx < (mat + 1) * n_a))
            def _(mat=mat):
                r0 = pl.multiple_of((idx - mat * n_a) * sr, sr)
                fn(pltpu.make_async_copy(self.hbm[mat].at[e, pl.ds(r0, sr), :], self.stage.at[slot],
                                         self.sems.at[slot]))

        @pl.when(idx >= 2 * n_a)
        def _():
            r0 = pl.multiple_of((idx - 2 * n_a) * self.rb, self.rb)
            for k in range(self.ratio):
                fn(pltpu.make_async_copy(self.hbm[2].at[e, pl.ds(r0, self.rb), pl.ds(k * self.de, self.de)],
                                         self.stage.at[slot, pl.ds(k * self.rb, self.rb), :],
                                         self.sems.at[slot]))

    def start(self, idx, e):
        self._each_copy(idx, e, lambda cp: cp.start())

    def wait(self, idx, e):
        self._each_copy(idx, e, lambda cp: cp.wait())

    def convert(self, idx, dst):
        slot = idx % self.slots
        sr, n_a = self.sr, self.n_a
        for mat in range(2):
            @pl.when((idx >= mat * n_a) & (idx < (mat + 1) * n_a))
            def _(mat=mat):
                r0 = pl.multiple_of((idx - mat * n_a) * sr, sr)
                self.bufs[mat][dst, pl.ds(r0, sr), :] = self.stage[slot].astype(BF16)

        @pl.when(idx >= 2 * n_a)
        def _():
            r0 = pl.multiple_of((idx - 2 * n_a) * self.rb, self.rb)
            for k in range(self.ratio):
                self.bufs[2][dst, pl.ds(r0, self.rb), pl.ds(k * self.de, self.de)] = (
                    self.stage[slot, pl.ds(k * self.rb, self.rb), :].astype(BF16))

    def start_first(self, e):
        for idx in range(min(self.slots, self.n_chunks)):
            self.start(jnp.int32(idx), e)

    def stream(self, e, dst, first, count):
        def body(c, carry):
            idx = first + c
            self.wait(idx, e)
            self.convert(idx, dst)

            @pl.when(idx + self.slots < self.n_chunks)
            def _():
                self.start(idx + self.slots, e)
            return carry

        lax.fori_loop(0, count, body, 0)


def _moe_expert_kernel(be_ref, nused_ref, pos_ref, nexte_ref, par_ref, c0_ref, cn_ref, tok_ref, tok_next_ref,
                       h_hbm, g_ref, wg_hbm, wu_hbm, wd_hbm, y_ref, x_buf, sems, stage, wsems,
                       wg_buf, wu_buf, wd_buf):
    i = pl.program_id(0)
    n_used = nused_ref[0]
    rows = x_buf.shape[1]
    slot = i % 2
    cur = par_ref[i]
    next_e = nexte_ref[i]
    w = _ExpertWeights((wg_hbm, wu_hbm, wd_hbm), stage, wsems, (wg_buf, wu_buf, wd_buf))

    @pl.when((i == 0) & (n_used > 0))
    def _():
        _gather_rows(h_hbm, tok_ref, rows, 1, 0, x_buf.at[0], sems.at[0], wait=False)
        w.start_first(be_ref[0])
        w.stream(be_ref[0], cur, 0, w.n_chunks)

    @pl.when(i + 1 < n_used)
    def _():
        _gather_rows(h_hbm, tok_next_ref, rows, 1, 0, x_buf.at[1 - slot], sems.at[1 - slot], wait=False)

    @pl.when((i < n_used) & (next_e >= 0))
    def _():
        @pl.when(pos_ref[i] == 0)
        def _():
            w.start_first(next_e)

        w.stream(next_e, 1 - cur, c0_ref[i], cn_ref[i])

    @pl.when(i < n_used)
    def _():
        _gather_rows(h_hbm, tok_ref, rows, 1, 0, x_buf.at[slot], sems.at[slot], wait=True)
        x = x_buf[slot]
        x = (x * lax.rsqrt(jnp.mean(x * x, axis=-1, keepdims=True) + EPS) * g_ref[...]).astype(BF16)
        a = jnp.dot(x, wg_buf[cur], preferred_element_type=F32)
        u = jnp.dot(x, wu_buf[cur], preferred_element_type=F32)
        hmid = (a * jax.nn.sigmoid(a) * u).astype(BF16)
        y_ref[...] = jnp.dot(hmid, wd_buf[cur], preferred_element_type=F32)

    @pl.when(i >= n_used)
    def _():
        y_ref[...] = jnp.zeros(y_ref.shape, y_ref.dtype)


def moe_experts(h, norm_g, tok_buf, sched, w_gate, w_up, w_down):
    n, d = h.shape
    blk_e, n_used, run_pos, run_len, next_e, parity = sched
    n_blocks = blk_e.shape[0]
    de = w_gate.shape[2]
    sr = min(W_STAGE_ROWS, d)
    assert d % sr == 0 and d % de == 0 and sr % (d // de) == 0 and de % (sr // (d // de)) == 0
    n_chunks = 2 * (d // sr) + de // (sr // (d // de))
    spread = jnp.maximum(run_len - 1, 1)
    k = jnp.maximum(run_pos - 1, 0)
    base, extra = n_chunks // spread, n_chunks % spread
    single = run_len == 1
    chunk0 = jnp.where(single, 0, k * base + jnp.minimum(k, extra)).astype(jnp.int32)
    chunk_n = jnp.where(single, n_chunks,
                        jnp.where(run_pos == 0, 0, base + (k < extra))).astype(jnp.int32)
    sched = (blk_e, n_used, run_pos, next_e, parity, chunk0, chunk_n)
    tok_blocks = tok_buf.reshape(n_blocks, 1, MOE_BLOCK)
    n_pre = len(sched)
    grid_spec = pltpu.PrefetchScalarGridSpec(
        num_scalar_prefetch=n_pre,
        grid=(n_blocks,),
        in_specs=[pl.BlockSpec((1, 1, MOE_BLOCK), lambda i, *_: (i, 0, 0), memory_space=pltpu.SMEM),
                  pl.BlockSpec((1, 1, MOE_BLOCK), lambda i, *_: (jnp.minimum(i + 1, n_blocks - 1), 0, 0),
                               memory_space=pltpu.SMEM),
                  pl.BlockSpec(memory_space=pl.ANY),
                  pl.BlockSpec((1, d), lambda i, *_: (0, 0)),
                  pl.BlockSpec(memory_space=pl.ANY),
                  pl.BlockSpec(memory_space=pl.ANY),
                  pl.BlockSpec(memory_space=pl.ANY)],
        out_specs=pl.BlockSpec((MOE_BLOCK, d), lambda i, *_: (i, 0)),
        scratch_shapes=[pltpu.VMEM((2, MOE_BLOCK, d), F32), pltpu.SemaphoreType.DMA((2,)),
                        pltpu.VMEM((W_STAGE_SLOTS, sr, de), F32), pltpu.SemaphoreType.DMA((W_STAGE_SLOTS,)),
                        pltpu.VMEM((2, d, de), BF16), pltpu.VMEM((2, d, de), BF16),
                        pltpu.VMEM((2, de, d), BF16)],
    )
    scratch_bytes = (2 * _nbytes((MOE_BLOCK, d), F32) + W_STAGE_SLOTS * _nbytes((sr, de), F32)
                     + 6 * _nbytes((d, de), BF16))
    return pl.pallas_call(
        _moe_expert_kernel,
        grid_spec=grid_spec,
        out_shape=jax.ShapeDtypeStruct((n_blocks * MOE_BLOCK, d), F32),
        compiler_params=pltpu.CompilerParams(
            dimension_semantics=("arbitrary",),
            vmem_limit_bytes=_vmem_limit([_nbytes((MOE_BLOCK, d), F32)],
                                         scratch_bytes + 4 * _nbytes((MOE_BLOCK, d), F32))),
        name="moe_experts",
    )(*sched, tok_blocks, tok_blocks, h, norm_g.reshape(1, d).astype(F32), w_gate, w_up, w_down)


def _combine_kernel(dest_ref, dest_next_ref, h_ref, w_ref, y_hbm, o_ref, buf, sems):
    i = pl.program_id(0)
    rows = buf.shape[2]
    slot = i % 2

    def gather(ids_ref, s, wait):
        for k in range(TOP_K):
            _gather_rows(y_hbm, ids_ref, rows, TOP_K, k, buf.at[s, k], sems.at[s], wait)

    @pl.when(i == 0)
    def _():
        gather(dest_ref, 0, False)

    @pl.when(i + 1 < pl.num_programs(0))
    def _():
        gather(dest_next_ref, 1 - slot, False)

    gather(dest_ref, slot, True)
    w = w_ref[...]
    o_ref[...] = h_ref[...] + (buf[slot, 0] * w[:, 0:1] + buf[slot, 1] * w[:, 1:2])


def moe_combine(h, y_buf, dest, weights, tm=128):
    n, d = h.shape
    tm = _tile(n, tm)
    nt = n // tm
    dest_blocks = dest.reshape(nt, 1, TOP_K * tm)
    return pl.pallas_call(
        _combine_kernel,
        grid=(nt,),
        in_specs=[pl.BlockSpec((1, 1, TOP_K * tm), lambda i: (i, 0, 0), memory_space=pltpu.SMEM),
                  pl.BlockSpec((1, 1, TOP_K * tm), lambda i: (jnp.minimum(i + 1, nt - 1), 0, 0),
                               memory_space=pltpu.SMEM),
                  pl.BlockSpec((tm, d), lambda i: (i, 0)),
                  pl.BlockSpec((tm, LANES), lambda i: (i, 0)),
                  pl.BlockSpec(memory_space=pl.ANY)],
        out_specs=pl.BlockSpec((tm, d), lambda i: (i, 0)),
        out_shape=jax.ShapeDtypeStruct((n, d), F32),
        scratch_shapes=[pltpu.VMEM((2, TOP_K, tm, d), F32), pltpu.SemaphoreType.DMA((2,))],
        compiler_params=pltpu.CompilerParams(
            dimension_semantics=("arbitrary",),
            vmem_limit_bytes=_vmem_limit([2 * _nbytes((tm, d), F32)], 8 * _nbytes((tm, d), F32))),
        name="moe_combine",
    )(dest_blocks, dest_blocks, h, weights, y_buf)


def _moe_dispatch(expert_id, n_experts):
    n = expert_id.shape[0]
    nk = n * TOP_K
    experts = jnp.arange(n_experts, dtype=jnp.int32)

    def lookup(table, idx):
        return jnp.sum(jnp.where(idx[:, None] == experts[None, :], table[None, :], 0), axis=1)

    flat_e = expert_id.reshape(nk).astype(jnp.int32)
    order = jnp.argsort(flat_e).astype(jnp.int32)
    rank = jnp.argsort(order).astype(jnp.int32)
    counts = jnp.sum((flat_e[:, None] == experts[None, :]).astype(jnp.int32), axis=0)
    start = jnp.cumsum(counts) - counts
    padded = (counts + MOE_BLOCK - 1) // MOE_BLOCK * MOE_BLOCK
    pad_end = jnp.cumsum(padded)
    pad_start = pad_end - padded
    n_blocks = (nk + n_experts * (MOE_BLOCK - 1) + MOE_BLOCK - 1) // MOE_BLOCK
    blk = jnp.arange(n_blocks, dtype=jnp.int32)
    blk_e = jnp.minimum(jnp.sum(pad_end[None, :] <= (blk * MOE_BLOCK)[:, None], axis=1),
                        n_experts - 1).astype(jnp.int32)
    dest = (rank + lookup(pad_start - start, flat_e)).astype(jnp.int32)
    blk_pad_start, blk_start, blk_count = (lookup(tbl, blk_e) for tbl in (pad_start, start, counts))
    within = (blk * MOE_BLOCK - blk_pad_start)[:, None] + jnp.arange(MOE_BLOCK, dtype=jnp.int32)[None, :]
    src = jnp.clip(blk_start[:, None] + within, 0, nk - 1)
    tok_buf = jnp.where((within >= 0) & (within < blk_count[:, None]), order[src] // TOP_K, 0)
    tok_buf = tok_buf.reshape(n_blocks * MOE_BLOCK).astype(jnp.int32)
    n_used = (pad_end[-1] // MOE_BLOCK).astype(jnp.int32)
    used = blk < n_used
    is_first = used & (blk_e != jnp.concatenate([jnp.full((1,), -1, jnp.int32), blk_e[:-1]]))
    parity = (jnp.cumsum(is_first.astype(jnp.int32)) - 1) % 2
    run_pos = blk - blk_pad_start // MOE_BLOCK
    run_len = lookup(padded, blk_e) // MOE_BLOCK
    live = jnp.where(counts > 0, experts, n_experts)
    next_live = jnp.concatenate([lax.cummin(live, reverse=True)[1:], jnp.full((1,), n_experts, jnp.int32)])
    blk_next = lookup(next_live, blk_e)
    next_e = jnp.where(blk_next < n_experts, blk_next, -1)
    sched = (blk_e, n_used.reshape(1), run_pos.astype(jnp.int32), run_len.astype(jnp.int32),
             next_e.astype(jnp.int32), parity.astype(jnp.int32))
    return tok_buf, sched, dest


def kernel(x, mem, norm1_g, norm2_g, mem_norm_g, w_in, b_gate, fox_f_bias, fox_q_g, fox_k_g,
           hgrn_lb_logits, hgrn_out_g, mem_q_g, mem_k_g, w_mem_kv, w_branch_fox, w_branch_hgrn,
           w_branch_mem, w_out, w_group, b_group, w_router, b_router, w_exp_gate, w_exp_up,
           w_exp_down):
    batch, seq, d = x.shape
    n_mem = mem.shape[1]
    depth = w_in.shape[0]
    fox_heads, fox_hd = fox_f_bias.shape[1], fox_q_g.shape[1]
    fox_w = fox_heads * fox_hd
    hgrn_w = hgrn_lb_logits.shape[1]
    hgrn_heads = hgrn_w // HGRN_EXPAND
    hgrn_dv = hgrn_out_g.shape[1]
    mem_hd = mem_q_g.shape[1]
    mem_w = w_branch_mem.shape[1]
    mem_heads = mem_w // mem_hd
    n_experts = w_router.shape[2]
    m = batch * seq

    sizes = (fox_w, fox_w, fox_w, fox_heads, hgrn_w, hgrn_w, hgrn_w, hgrn_w, mem_w, 3 * d)
    offs = [0]
    for sz in sizes:
        offs.append(offs[-1] + sz)
    (o_fq, o_fk, o_fv, o_ff, o_hq, o_hf, o_hi, o_hg, o_mq, o_gt, o_end) = offs

    lb_all = jnp.cumsum(jax.nn.softmax(hgrn_lb_logits.astype(F32), axis=0), axis=0)
    mem_n = rmsnorm_rows(mem.reshape(batch * n_mem, d), mem_norm_g, BF16)

    h = x.reshape(m, d)
    for layer in range(depth):
        w_qk, w_vt, w_ff, w_plain, w_hf, w_mq, w_gates = split_w_in(
            w_in[layer].astype(F32), o_fv, o_ff, fox_heads, (hgrn_w, mem_w, 3 * d))
        hn = rmsnorm_rows(h, norm1_g[layer], BF16)

        qk_gain = jnp.concatenate([jnp.tile(fox_q_g[layer] * (LOG2E * fox_hd ** -0.5), fox_heads),
                                   jnp.tile(fox_k_g[layer], fox_heads)])
        qk = matmul(hn, w_qk, BF16, functools.partial(_epi_headnorm, head_dim=fox_hd),
                    row_vec=qk_gain, name="proj_fox_qk")
        v_t = matmul_t(hn, w_vt, BF16, tm=FOX_BLOCK, name="proj_fox_vt")
        plain = matmul(hn, w_plain, BF16, name="proj_plain")
        hd_blk = lambda width: width // HGRN_EXPAND
        c_hq = 0
        c_hi = c_hq + hd_blk(hgrn_w)
        c_hg = c_hi + hd_blk(hgrn_w)
        z = matmul(hn, w_hf, F32, name="proj_hgrn_f")
        ff = matmul(hn, w_ff, F32, tn=LANES, name="proj_fox_f")
        mq = matmul(hn, w_mq, BF16, functools.partial(_epi_headnorm, head_dim=mem_hd),
                    row_vec=jnp.tile(mem_q_g[layer] * mem_hd ** -0.5, mem_heads), tn=mem_hd,
                    name="proj_mem_q")
        gates = matmul(hn, w_gates, BF16, _epi_sigmoid_bias, row_vec=b_gate[layer], name="proj_gates")

        kbias = fox_key_bias(ff, fox_f_bias[layer], batch, fox_heads, fox_hd)
        o_a = fox_attention(qk, kbias, v_t, batch, fox_heads, fox_hd)

        o_b = hgrn2(plain, c_hq, c_hi, c_hg, z, lb_all[layer], hgrn_out_g[layer], batch, hgrn_heads)

        wkv = w_mem_kv[layer]
        mk = matmul(mem_n, wkv[:, :mem_w].astype(BF16), BF16,
                    functools.partial(_epi_headnorm, head_dim=mem_hd),
                    row_vec=jnp.tile(mem_k_g[layer], mem_heads), tn=mem_hd, name="proj_mem_k")
        mv = matmul(mem_n, wkv[:, mem_w:].astype(BF16), BF16, name="proj_mem_v")
        o_c = mem_attention(mq, mk, mv, batch, mem_heads, mem_hd)

        merged = gated_merge(o_a, o_b, o_c, w_branch_fox[layer].astype(BF16),
                             w_branch_hgrn[layer].astype(BF16), w_branch_mem[layer].astype(BF16), gates)
        h = matmul(merged, w_out[layer].astype(BF16), F32, _epi_residual, residual=h, tn=512,
                   name="proj_out")

        expert_id, weights = moe_router(h, norm2_g[layer], w_group[layer], b_group[layer],
                                        w_router[layer], b_router[layer])
        tok_buf, sched, dest = _moe_dispatch(expert_id, n_experts)
        y_buf = moe_experts(h, norm2_g[layer], tok_buf, sched, w_exp_gate[layer].astype(F32),
                            w_exp_up[layer].astype(F32), w_exp_down[layer].astype(F32))
        h = moe_combine(h, y_buf, dest, weights)
    return h.reshape(batch, seq, d)
```

```python
import functools

import jax
import jax.numpy as jnp
from jax import lax
from jax.experimental import pallas as pl
from jax.experimental.pallas import tpu as pltpu

F32 = jnp.float32
BF16 = jnp.bfloat16

EPS = 1e-6
CHUNK = 64
SUB = 16
HGRN_EXPAND = 128
TOP_K = 2
MOE_BLOCK = 128
FOX_BLOCK = 512
FOX_GROUP = 4
LANES = 128
LOG2E = 1.4426950408889634
V7X_VMEM_LIMIT_CAP = 60000 * 1024


def _vmem_limit(block_bytes, temp_bytes=0):
    need = int(1.25 * (2 * sum(block_bytes) + temp_bytes)) + (2 << 20)
    return min(max(need, 16 << 20), V7X_VMEM_LIMIT_CAP)


def _nbytes(shape, dtype):
    n = 1
    for s in shape:
        n *= s
    return n * jnp.dtype(dtype).itemsize


def _tile(dim, want):
    if dim <= want:
        return dim
    for t in range(want - want % LANES, 0, -LANES):
        if dim % t == 0:
            return t
    raise ValueError((dim, want))


def _rmsnorm_kernel(x_ref, g_ref, o_ref):
    x = x_ref[...].astype(F32)
    y = x * lax.rsqrt(jnp.mean(x * x, axis=-1, keepdims=True) + EPS)
    o_ref[...] = (y * g_ref[...]).astype(o_ref.dtype)


def rmsnorm_rows(x, g, out_dtype, tm=256):
    m, d = x.shape
    tm = _tile(m, tm)
    return pl.pallas_call(
        _rmsnorm_kernel,
        grid=(m // tm,),
        in_specs=[pl.BlockSpec((tm, d), lambda i: (i, 0)),
                  pl.BlockSpec((1, d), lambda i: (0, 0))],
        out_specs=pl.BlockSpec((tm, d), lambda i: (i, 0)),
        out_shape=jax.ShapeDtypeStruct((m, d), out_dtype),
        compiler_params=pltpu.CompilerParams(
            dimension_semantics=("parallel",),
            vmem_limit_bytes=_vmem_limit([_nbytes((tm, d), x.dtype), _nbytes((tm, d), out_dtype)],
                                         2 * _nbytes((tm, d), F32))),
        name="rmsnorm_rows",
    )(x, g.reshape(1, d).astype(F32))


def _mm_kernel(a_ref, w_ref, *refs, epilogue, w_is_nk):
    o_ref = refs[-1]
    contract = (((1,), (1 if w_is_nk else 0,)), ((), ()))
    acc = lax.dot_general(a_ref[...], w_ref[...], contract, preferred_element_type=F32)
    epilogue(acc, o_ref, *refs[:-1])


def _epi_plain(acc, o_ref):
    o_ref[...] = acc.astype(o_ref.dtype)


def _epi_headnorm(acc, o_ref, g_ref, *, head_dim):
    for h in range(acc.shape[1] // head_dim):
        sl = slice(h * head_dim, (h + 1) * head_dim)
        blk = acc[:, sl]
        y = blk * lax.rsqrt(jnp.mean(blk * blk, axis=-1, keepdims=True) + EPS)
        o_ref[:, sl] = (y * g_ref[:, sl]).astype(o_ref.dtype)


def _epi_sigmoid_bias(acc, o_ref, b_ref):
    o_ref[...] = jax.nn.sigmoid(acc + b_ref[...]).astype(o_ref.dtype)


def _epi_residual(acc, o_ref, r_ref):
    o_ref[...] = (r_ref[...] + acc).astype(o_ref.dtype)


def matmul(a, w, out_dtype, epilogue=_epi_plain, row_vec=None, residual=None, tm=1024, tn=1024,
           w_is_nk=False, name="matmul"):
    m, k = a.shape
    n = w.shape[0 if w_is_nk else 1]
    tm, tn = _tile(m, tm), _tile(n, tn)
    in_specs = [pl.BlockSpec((tm, k), lambda i, j: (i, 0)),
                pl.BlockSpec((tn, k), lambda i, j: (j, 0)) if w_is_nk
                else pl.BlockSpec((k, tn), lambda i, j: (0, j))]
    operands = [a, w]
    blocks = [_nbytes((tm, k), a.dtype), _nbytes((k, tn), w.dtype), _nbytes((tm, tn), out_dtype)]
    if row_vec is not None:
        in_specs.append(pl.BlockSpec((1, tn), lambda i, j: (0, j)))
        operands.append(row_vec.reshape(1, n).astype(F32))
    if residual is not None:
        in_specs.append(pl.BlockSpec((tm, tn), lambda i, j: (i, j)))
        operands.append(residual)
        blocks.append(_nbytes((tm, tn), residual.dtype))
    return pl.pallas_call(
        functools.partial(_mm_kernel, epilogue=epilogue, w_is_nk=w_is_nk),
        grid=(m // tm, n // tn),
        in_specs=in_specs,
        out_specs=pl.BlockSpec((tm, tn), lambda i, j: (i, j)),
        out_shape=jax.ShapeDtypeStruct((m, n), out_dtype),
        compiler_params=pltpu.CompilerParams(
            dimension_semantics=("parallel", "parallel"),
            vmem_limit_bytes=_vmem_limit(blocks, 2 * _nbytes((tm, tn), F32))),
        name=name,
    )(*operands)


def _mm_t_kernel(wt_ref, a_ref, o_ref):
    o_ref[0] = lax.dot_general(wt_ref[...], a_ref[...], (((1,), (1,)), ((), ())),
                               preferred_element_type=F32).astype(o_ref.dtype)


def matmul_t(a, w_t, out_dtype, tm=512, tn=1024, name="matmul_t"):
    m, k = a.shape
    n = w_t.shape[0]
    tm, tn = _tile(m, tm), _tile(n, tn)
    return pl.pallas_call(
        _mm_t_kernel,
        grid=(m // tm, n // tn),
        in_specs=[pl.BlockSpec((tn, k), lambda i, j: (j, 0)),
                  pl.BlockSpec((tm, k), lambda i, j: (i, 0))],
        out_specs=pl.BlockSpec((1, tn, tm), lambda i, j: (i, j, 0)),
        out_shape=jax.ShapeDtypeStruct((m // tm, n, tm), out_dtype),
        compiler_params=pltpu.CompilerParams(
            dimension_semantics=("parallel", "parallel"),
            vmem_limit_bytes=_vmem_limit([_nbytes((tm, k), a.dtype), _nbytes((tn, k), w_t.dtype),
                                          _nbytes((tn, tm), out_dtype)], 2 * _nbytes((tn, tm), F32))),
        name=name,
    )(w_t, a)


N_SPLIT = 3


def _fox_bias_kernel(f_ref, b_ref, e_ref, o_ref, carry):
    @pl.when(pl.program_id(1) == 0)
    def _():
        carry[...] = jnp.zeros(carry.shape, F32)

    x = f_ref[...] + b_ref[...]
    a = jnp.minimum(x, 0.0) - jnp.log(1.0 + jnp.exp(-jnp.abs(x)))
    r = a.shape[0]
    row = lax.broadcasted_iota(jnp.int32, a.shape, 0)
    sh = 1
    while sh < r:
        a = a + jnp.where(row >= sh, pltpu.roll(a, sh, axis=0), 0.0)
        sh *= 2
    c = a + carry[...]
    carry[...] = c[r - 1:r, :]
    rest = c * (-LOG2E)
    pieces = []
    for _ in range(N_SPLIT):
        piece = rest.astype(BF16)
        pieces.append(piece)
        rest = rest - piece.astype(F32)
    o_ref[...] = jnp.dot(jnp.concatenate(pieces, axis=1), e_ref[...],
                         preferred_element_type=F32).astype(o_ref.dtype)


def fox_key_bias(ff, bias, batch, heads, head_dim, rows=1024):
    m = ff.shape[0]
    s = m // batch
    rows = _tile(s, rows)
    nr = s // rows
    src = jnp.arange(N_SPLIT * LANES)
    dst = (src % LANES) * head_dim + src // LANES
    place = ((jnp.arange(heads * head_dim)[None, :] == dst[:, None])
             & ((src % LANES) < heads)[:, None]).astype(BF16)
    bias_row = jnp.zeros((1, LANES), F32).at[0, :heads].set(bias.astype(F32))
    return pl.pallas_call(
        _fox_bias_kernel,
        grid=(batch, nr),
        in_specs=[pl.BlockSpec((rows, LANES), lambda b, t: (b * nr + t, 0)),
                  pl.BlockSpec((1, LANES), lambda b, t: (0, 0)),
                  pl.BlockSpec((N_SPLIT * LANES, heads * head_dim), lambda b, t: (0, 0))],
        out_specs=pl.BlockSpec((rows, heads * head_dim), lambda b, t: (b * nr + t, 0)),
        out_shape=jax.ShapeDtypeStruct((m, heads * head_dim), BF16),
        scratch_shapes=[pltpu.VMEM((1, LANES), F32)],
        compiler_params=pltpu.CompilerParams(
            dimension_semantics=("parallel", "arbitrary"),
            vmem_limit_bytes=_vmem_limit([_nbytes((rows, heads * head_dim), BF16),
                                          _nbytes((N_SPLIT * LANES, heads * head_dim), BF16)],
                                         _nbytes((rows, heads * head_dim), F32) + 16 * _nbytes((rows, LANES), F32))),
        name="fox_key_bias",
    )(ff, bias_row, place)


SUM_ROWS = 16


KIND_PLAIN, KIND_DIAG, KIND_PAD = 0, 1, 2


def _fox_attn_kernel(row_ref, col_ref, kind_ref, start_ref, q_ref, k_ref, kb_ref, vt_ref, o_ref,
                     qt_scr, mask_scr, s_scr, m_scr, acc_scr, snap_scr, *, n_steps):
    nq, hd2, t = qt_scr.shape
    hd = hd2 // 2

    causal = (lax.broadcasted_iota(jnp.int32, (t, t), 0) <= lax.broadcasted_iota(jnp.int32, (t, t), 1))
    mask_scr[KIND_PLAIN] = jnp.zeros((t, t), F32)
    mask_scr[KIND_DIAG] = jnp.where(causal, 0.0, -jnp.inf)
    mask_scr[KIND_PAD] = jnp.full((t, t), -jnp.inf, F32)

    for r in range(nq):
        qt_scr[r, :hd, :] = q_ref[r * t:(r + 1) * t, :].astype(F32).T.astype(BF16)
        qt_scr[r, hd:, :] = jnp.where(lax.broadcasted_iota(jnp.int32, (hd, t), 0) < N_SPLIT,
                                      1.0, 0.0).astype(BF16)
    m_scr[...] = jnp.zeros(m_scr.shape, F32)
    acc_scr[...] = jnp.zeros(acc_scr.shape, F32)

    def scores(p):
        off = pl.multiple_of(col_ref[p] * t, t)
        k_aug = jnp.concatenate([k_ref[pl.ds(off, t), :], kb_ref[pl.ds(off, t), :]], axis=1)
        return jnp.dot(k_aug, qt_scr[row_ref[p]], preferred_element_type=F32)

    def update(s, p):
        s = s + mask_scr[kind_ref[p]]
        m_prev = jnp.where(start_ref[p] == 1, -jnp.inf, m_scr[...])
        m_new = jnp.maximum(m_prev, jnp.max(s, axis=0, keepdims=True))
        alpha = jnp.exp2(m_prev - m_new)
        prob = jnp.exp2((s - m_new).astype(BF16))
        vt_aug = jnp.concatenate([vt_ref[0, col_ref[p]], jnp.ones((SUM_ROWS, t), BF16)], axis=0)
        acc = alpha * acc_scr[...] + jnp.dot(vt_aug, prob, preferred_element_type=F32)
        m_scr[...] = m_new
        acc_scr[...] = acc
        return acc

    def write_row(p, acc_ref):
        @pl.when(kind_ref[p] == KIND_DIAG)
        def _():
            out = acc_ref[:hd, :] * (1.0 / acc_ref[hd:hd + 1, :])
            o_ref[pl.ds(pl.multiple_of(row_ref[p] * t, t), t), :] = out.T.astype(o_ref.dtype)

    group = s_scr.shape[0]
    s_scr[0] = scores(0)

    def pair_group(u, carry):
        p = group * u
        for k in range(group):
            s_scr[(k + 1) % group] = scores(p + k + 1)
            acc = update(s_scr[k], p + k)
            if k + 1 < group:
                snap_scr[k] = acc
        for k in range(group):
            write_row(p + k, snap_scr.at[k] if k + 1 < group else acc_scr)
        return carry

    lax.fori_loop(0, n_steps, pair_group, 0)


def fox_attention(qk, kbias, v_t, batch, heads, head_dim):
    m = qk.shape[0]
    s = m // batch
    t = v_t.shape[2]
    nq = s // t
    pairs = [(i, j, KIND_DIAG if i == j else KIND_PLAIN, int(j == 0)) for i in range(nq) for j in range(i + 1)]
    n_steps = -(-len(pairs) // FOX_GROUP)
    pairs += [(nq - 1, nq - 1, KIND_PAD, 0)] * (FOX_GROUP * n_steps + 1 - len(pairs))
    tables = [jnp.asarray(v, jnp.int32) for v in zip(*pairs)]
    grid_spec = pltpu.PrefetchScalarGridSpec(
        num_scalar_prefetch=len(tables),
        grid=(batch, heads),
        in_specs=[pl.BlockSpec((s, head_dim), lambda b, h, *_: (b, h)),
                  pl.BlockSpec((s, head_dim), lambda b, h, *_: (b, heads + h)),
                  pl.BlockSpec((s, head_dim), lambda b, h, *_: (b, h)),
                  pl.BlockSpec((1, nq, head_dim, t), lambda b, h, *_: (b, 0, h, 0))],
        out_specs=pl.BlockSpec((s, head_dim), lambda b, h, *_: (b, h)),
        scratch_shapes=[pltpu.VMEM((nq, 2 * head_dim, t), BF16), pltpu.VMEM((3, t, t), F32),
                        pltpu.VMEM((FOX_GROUP, t, t), F32),
                        pltpu.VMEM((1, t), F32), pltpu.VMEM((head_dim + SUM_ROWS, t), F32),
                        pltpu.VMEM((FOX_GROUP - 1, head_dim + SUM_ROWS, t), F32)],
    )
    return pl.pallas_call(
        functools.partial(_fox_attn_kernel, n_steps=n_steps),
        grid_spec=grid_spec,
        out_shape=jax.ShapeDtypeStruct((m, heads * head_dim), BF16),
        compiler_params=pltpu.CompilerParams(
            dimension_semantics=("parallel", "parallel"),
            vmem_limit_bytes=_vmem_limit([5 * _nbytes((s, head_dim), BF16)],
                                         2 * _nbytes((s, head_dim), BF16) + 10 * _nbytes((t, t), F32))),
        name="fox_attention",
    )(*tables, qk, qk, kbias, v_t.reshape(batch, nq, heads * head_dim, t))


def _hgrn_kernel(q_ref, z_ref, v_ref, hg_ref, lb_ref, og_ref, o_ref, st_ref):
    @pl.when(pl.program_id(2) == 0)
    def _():
        st_ref[...] = jnp.zeros(st_ref.shape, F32)

    t, dk = z_ref.shape
    dv = v_ref.shape[1]
    nc, nb, nsub = t // CHUNK, t // SUB, CHUNK // SUB

    q = q_ref[...].astype(F32)
    z = z_ref[...]
    lb = lb_ref[...]
    g = jnp.log(lb + (1.0 - lb) * jax.nn.sigmoid(z))
    kk = (1.0 - lb) * jax.nn.sigmoid(-z)

    row = lax.broadcasted_iota(jnp.int32, (t, dk), 0)
    rin = row % CHUNK
    b = g
    sh = 1
    while sh < CHUNK:
        b = b + jnp.where(rin >= sh, pltpu.roll(b, sh, axis=0), 0.0)
        sh *= 2

    b3 = b.reshape(nc, CHUNK, dk)
    q3 = q.reshape(nc, CHUNK, dk)
    k3 = kk.reshape(nc, CHUNK, dk)
    v3 = v_ref[...].reshape(nc, CHUNK, dv)
    b_last = b3[:, CHUNK - 1:CHUNK, :]
    q_in = (q3 * jnp.exp(b3)).astype(BF16)
    k_out = (k3 * jnp.exp(b_last - b3)).astype(BF16)
    d_last = jnp.exp(b_last)

    st = st_ref[...]
    inter = []
    for c in range(nc):
        inter.append(lax.dot_general(q_in[c], st.astype(BF16), (((1,), (1,)), ((), ())),
                                     preferred_element_type=F32))
        kv_t = lax.dot_general(v3[c], k_out[c], (((0,), (0,)), ((), ())),
                               preferred_element_type=F32)
        st = st * d_last[c] + kv_t
    st_ref[...] = st
    inter = jnp.concatenate(inter, axis=0)

    q_parts, k_parts = [], []
    for blk in range(1, nsub):
        lo, hi = blk * SUB, (blk + 1) * SUB
        b_ref_pt = b3[:, lo - 1:lo, :]
        q_hat = (q3[:, lo:hi, :] * jnp.exp(b3[:, lo:hi, :] - b_ref_pt)).astype(BF16)
        k_hat = (k3[:, :lo, :] * jnp.exp(b_ref_pt - b3[:, :lo, :])).astype(BF16)
        q_rows = [jnp.zeros((nc, lo, dk), BF16), q_hat] + ([jnp.zeros((nc, CHUNK - hi, dk), BF16)] if hi < CHUNK else [])
        q_parts.append(jnp.concatenate(q_rows, axis=1))
        k_parts.append(jnp.concatenate([k_hat, jnp.zeros((nc, CHUNK - lo, dk), BF16)], axis=1))
    q_cat = jnp.concatenate(q_parts, axis=2)
    k_cat = jnp.concatenate(k_parts, axis=2)
    s_off = lax.dot_general(q_cat, k_cat, (((2,), (2,)), ((0,), (0,))),
                            preferred_element_type=F32)

    b4 = b.reshape(nb, SUB, dk)
    q4 = q.reshape(nb, SUB, dk)
    k4 = kk.reshape(nb, SUB, dk)
    rsub = (row % SUB).reshape(nb, SUB, dk)
    u_row = lax.broadcasted_iota(jnp.int32, (dk, LANES), 1)
    half = SUB // 2
    s_lo = jnp.zeros((t, LANES), F32)
    s_hi = jnp.zeros((t // 2, LANES), F32)
    for j in range(SUB):
        r0 = 0 if j < half else half
        bj, kj = b4[:, j:j + 1, :], k4[:, j:j + 1, :]
        decay = jnp.exp(jnp.where(rsub[:, r0:, :] >= j, b4[:, r0:, :] - bj, -jnp.inf))
        p = (q4[:, r0:, :] * (kj * decay)).reshape(-1, dk).astype(BF16)
        place = jnp.where((u_row % SUB == j) & (u_row < CHUNK), 1.0, 0.0).astype(BF16)
        if j < half:
            s_lo = s_lo + jnp.dot(p, place, preferred_element_type=F32)
        else:
            s_hi = s_hi + jnp.dot(p, place, preferred_element_type=F32)
    s_diag = s_lo.reshape(nb, SUB, LANES) + jnp.concatenate(
        [jnp.zeros((nb, half, LANES), F32), s_hi.reshape(nb, half, LANES)], axis=1)
    s_diag = s_diag.reshape(nc, CHUNK, LANES)[:, :, :CHUNK]

    rb = lax.broadcasted_iota(jnp.int32, (nc, CHUNK, CHUNK), 1) // SUB
    cb = lax.broadcasted_iota(jnp.int32, (nc, CHUNK, CHUNK), 2) // SUB
    scores = jnp.where(cb == rb, s_diag, jnp.where(cb < rb, s_off, 0.0)).astype(BF16)
    intra = lax.dot_general(scores, v3, (((2,), (1,)), ((0,), (0,))),
                            preferred_element_type=F32).reshape(t, dv)

    o = inter + intra
    o = o * lax.rsqrt(jnp.mean(o * o, axis=-1, keepdims=True) + EPS) * og_ref[...]
    hg = hg_ref[...].astype(F32)
    o_ref[...] = (o * (hg * jax.nn.sigmoid(hg))).astype(o_ref.dtype)


def hgrn2(src, q_col0, v_col0, hg_col0, z, lb, out_g, batch, heads, rows=512):
    m = z.shape[0]
    s = m // batch
    dk = HGRN_EXPAND
    dv = out_g.shape[-1]
    rows = _tile(s, rows)
    nt = s // rows
    row_blk = lambda b, h, t: b * nt + t
    return pl.pallas_call(
        _hgrn_kernel,
        grid=(batch, heads, nt),
        in_specs=[pl.BlockSpec((rows, dk), lambda b, h, t: (row_blk(b, h, t), q_col0 + h)),
                  pl.BlockSpec((rows, dk), lambda b, h, t: (row_blk(b, h, t), h)),
                  pl.BlockSpec((rows, dv), lambda b, h, t: (row_blk(b, h, t), v_col0 + h)),
                  pl.BlockSpec((rows, dv), lambda b, h, t: (row_blk(b, h, t), hg_col0 + h)),
                  pl.BlockSpec((1, dk), lambda b, h, t: (0, h)),
                  pl.BlockSpec((1, dv), lambda b, h, t: (0, 0))],
        out_specs=pl.BlockSpec((rows, dv), lambda b, h, t: (row_blk(b, h, t), h)),
        out_shape=jax.ShapeDtypeStruct((m, heads * dv), BF16),
        scratch_shapes=[pltpu.VMEM((dv, dk), F32)],
        compiler_params=pltpu.CompilerParams(
            dimension_semantics=("parallel", "parallel", "arbitrary"),
            vmem_limit_bytes=_vmem_limit([6 * _nbytes((rows, dk), F32)], 40 * _nbytes((rows, dk), F32))),
        name="hgrn2",
    )(src, z, src, src, lb.reshape(1, -1).astype(F32), out_g.reshape(1, dv).astype(F32))


def _mem_attn_kernel(q_ref, k_ref, v_ref, o_ref):
    s = lax.dot_general(q_ref[...], k_ref[...], (((1,), (1,)), ((), ())), preferred_element_type=F32)
    m = jnp.max(s, axis=1, keepdims=True)
    p = jnp.exp(s - m)
    l = jnp.sum(p, axis=1, keepdims=True)
    o = jnp.dot(p.astype(BF16), v_ref[...], preferred_element_type=F32)
    o_ref[...] = (o / l).astype(o_ref.dtype)


def mem_attention(q, mk, mv, batch, heads, head_dim, tq=1024):
    m = q.shape[0]
    s = m // batch
    n_mem = mk.shape[0] // batch
    tq = _tile(s, tq)
    nq = s // tq
    return pl.pallas_call(
        _mem_attn_kernel,
        grid=(batch, heads, nq),
        in_specs=[pl.BlockSpec((tq, head_dim), lambda b, h, i: (b * nq + i, h)),
                  pl.BlockSpec((n_mem, head_dim), lambda b, h, i: (b, h)),
                  pl.BlockSpec((n_mem, head_dim), lambda b, h, i: (b, h))],
        out_specs=pl.BlockSpec((tq, head_dim), lambda b, h, i: (b * nq + i, h)),
        out_shape=jax.ShapeDtypeStruct((m, heads * head_dim), BF16),
        compiler_params=pltpu.CompilerParams(
            dimension_semantics=("parallel", "parallel", "parallel"),
            vmem_limit_bytes=_vmem_limit([2 * _nbytes((tq, head_dim), BF16), 2 * _nbytes((n_mem, head_dim), BF16)],
                                         4 * _nbytes((tq, head_dim), F32))),
        name="mem_attention",
    )(q, mk, mv)


def _merge_kernel(a_ref, b_ref, c_ref, wa_ref, wb_ref, wc_ref, ga_ref, gb_ref, gc_ref, o_ref):
    acc = ga_ref[...].astype(F32) * jnp.dot(a_ref[...], wa_ref[...], preferred_element_type=F32)
    acc = acc + gb_ref[...].astype(F32) * jnp.dot(b_ref[...], wb_ref[...], preferred_element_type=F32)
    acc = acc + gc_ref[...].astype(F32) * jnp.dot(c_ref[...], wc_ref[...], preferred_element_type=F32)
    o_ref[...] = acc.astype(o_ref.dtype)


def gated_merge(o_a, o_b, o_c, w_a, w_b, w_c, gates, tm=1024, tn=512):
    m = o_a.shape[0]
    d = w_a.shape[1]
    tm, tn = _tile(m, tm), _tile(d, tn)
    nj = d // tn
    a_spec = lambda arr: pl.BlockSpec((tm, arr.shape[1]), lambda i, j: (i, 0))
    w_spec = lambda arr: pl.BlockSpec((arr.shape[0], tn), lambda i, j: (0, j))
    g_spec = lambda r: pl.BlockSpec((tm, tn), lambda i, j: (i, r * nj + j))
    blocks = ([_nbytes((tm, a.shape[1]), BF16) for a in (o_a, o_b, o_c)]
              + [_nbytes((w.shape[0], tn), BF16) for w in (w_a, w_b, w_c)] + 4 * [_nbytes((tm, tn), BF16)])
    return pl.pallas_call(
        _merge_kernel,
        grid=(m // tm, nj),
        in_specs=[a_spec(o_a), a_spec(o_b), a_spec(o_c), w_spec(w_a), w_spec(w_b), w_spec(w_c),
                  g_spec(0), g_spec(1), g_spec(2)],
        out_specs=pl.BlockSpec((tm, tn), lambda i, j: (i, j)),
        out_shape=jax.ShapeDtypeStruct((m, d), BF16),
        compiler_params=pltpu.CompilerParams(
            dimension_semantics=("parallel", "parallel"),
            vmem_limit_bytes=_vmem_limit(blocks, 3 * _nbytes((tm, tn), F32))),
        name="gated_merge",
    )(o_a, o_b, o_c, w_a, w_b, w_c, gates, gates, gates)


def _router_kernel(h_ref, g_ref, whi_ref, wlo_ref, b_ref, lg_ref, id_ref, wt_ref, *, n_groups):
    x = h_ref[...]
    hn = x * lax.rsqrt(jnp.mean(x * x, axis=-1, keepdims=True) + EPS) * g_ref[...]
    hn_hi = hn.astype(BF16)
    hn_lo = (hn - hn_hi.astype(F32)).astype(BF16)
    logits = (jnp.dot(hn_hi, whi_ref[...], preferred_element_type=F32)
              + (jnp.dot(hn_hi, wlo_ref[...], preferred_element_type=F32)
                 + jnp.dot(hn_lo, whi_ref[...], preferred_element_type=F32))) + b_ref[...]
    lane = lax.broadcasted_iota(jnp.int32, logits.shape, 1).astype(F32)
    lane_grp = lg_ref[...]
    neg = -jnp.inf
    big = float(LANES)

    gl = jnp.where(lane < n_groups, logits, neg)
    gmax = jnp.max(gl, axis=1, keepdims=True)
    grp = jnp.min(jnp.where(gl == gmax, lane, big), axis=1, keepdims=True)
    p_grp = 1.0 / jnp.sum(jnp.exp(gl - gmax), axis=1, keepdims=True)

    el = jnp.where(lane_grp == grp, logits, neg)
    v1 = jnp.max(el, axis=1, keepdims=True)
    i1 = jnp.min(jnp.where(el == v1, lane, big), axis=1, keepdims=True)
    el2 = jnp.where(lane == i1, neg, el)
    v2 = jnp.max(el2, axis=1, keepdims=True)
    i2 = jnp.min(jnp.where(el2 == v2, lane, big), axis=1, keepdims=True)
    e21 = jnp.exp(v2 - v1)
    w1 = p_grp / (1.0 + e21)
    w2 = p_grp * e21 / (1.0 + e21)

    id_ref[...] = jnp.where(lane == 0.0, i1 - n_groups, i2 - n_groups).astype(jnp.int32)
    wt_ref[...] = jnp.where(lane == 0.0, w1, w2)


def moe_router(h, norm_g, w_group, b_group, w_router, b_router, tm=256):
    n, d = h.shape
    n_groups = w_group.shape[1]
    n_experts = w_router.shape[1]
    per_group = n_experts // n_groups
    assert n_groups + n_experts <= LANES
    pad = LANES - n_groups - n_experts
    w = jnp.concatenate([w_group, w_router, jnp.zeros((d, pad), F32)], axis=1).astype(F32)
    bias = jnp.concatenate([b_group, b_router, jnp.zeros((pad,), F32)]).reshape(1, LANES).astype(F32)
    w_hi = w.astype(BF16)
    w_lo = (w - w_hi.astype(F32)).astype(BF16)
    lane = jnp.arange(LANES)
    lane_grp = jnp.where((lane >= n_groups) & (lane < n_groups + n_experts),
                         (lane - n_groups) // per_group, -1).astype(F32).reshape(1, LANES)
    tm = _tile(n, tm)
    ids, wts = pl.pallas_call(
        functools.partial(_router_kernel, n_groups=n_groups),
        grid=(n // tm,),
        in_specs=[pl.BlockSpec((tm, d), lambda i: (i, 0)),
                  pl.BlockSpec((1, d), lambda i: (0, 0)),
                  pl.BlockSpec((d, LANES), lambda i: (0, 0)),
                  pl.BlockSpec((d, LANES), lambda i: (0, 0)),
                  pl.BlockSpec((1, LANES), lambda i: (0, 0)),
                  pl.BlockSpec((1, LANES), lambda i: (0, 0))],
        out_specs=[pl.BlockSpec((tm, LANES), lambda i: (i, 0)),
                   pl.BlockSpec((tm, LANES), lambda i: (i, 0))],
        out_shape=[jax.ShapeDtypeStruct((n, LANES), jnp.int32),
                   jax.ShapeDtypeStruct((n, LANES), F32)],
        compiler_params=pltpu.CompilerParams(
            dimension_semantics=("parallel",),
            vmem_limit_bytes=_vmem_limit([_nbytes((tm, d), F32), _nbytes((d, LANES), F32)],
                                         8 * _nbytes((tm, d), F32))),
        name="moe_router",
    )(h, norm_g.reshape(1, d).astype(F32), w_hi, w_lo, bias, lane_grp)
    return ids[:, :TOP_K], wts


def _row_copy(src_hbm, row, dst, slot, sem):
    return pltpu.make_async_copy(src_hbm.at[pl.ds(row, 1)], dst.at[pl.ds(slot, 1)], sem)


GATHER_UNROLL = 8


def _gather_rows(src_hbm, ids_ref, n_ids, stride, offset, dst, sem, wait):
    def body(r, c):
        cp = _row_copy(src_hbm, ids_ref[0, 0, stride * r + offset], dst, r, sem)
        if wait:
            cp.wait()
        else:
            cp.start()
        return c

    lax.fori_loop(0, n_ids, body, 0, unroll=GATHER_UNROLL)


W_STAGE_ROWS = 512
W_STAGE_SLOTS = 8


class _ExpertWeights:
    def __init__(self, hbm, stage, sems, bufs):
        self.hbm, self.stage, self.sems, self.bufs = hbm, stage, sems, bufs
        d, de = hbm[0].shape[1], hbm[0].shape[2]
        self.de = de
        self.sr = stage.shape[1]
        self.slots = stage.shape[0]
        self.ratio = d // de
        self.rb = self.sr // self.ratio
        self.n_a = d // self.sr
        self.n_chunks = 2 * self.n_a + de // self.rb

    def _each_copy(self, idx, e, fn):
        slot = idx % self.slots
        sr, n_a = self.sr, self.n_a
        for mat in range(2):
            @pl.when((idx >= mat * n_a) & (idx < (mat + 1) * n_a))
            def _(mat=mat):
                r0 = pl.multiple_of((idx - mat * n_a) * sr, sr)
                fn(pltpu.make_async_copy(self.hbm[mat].at[e, pl.ds(r0, sr), :], self.stage.at[slot],
                                         self.sems.at[slot]))

        @pl.when(idx >= 2 * n_a)
        def _():
            r0 = pl.multiple_of((idx - 2 * n_a) * self.rb, self.rb)
            for k in range(self.ratio):
                fn(pltpu.make_async_copy(self.hbm[2].at[e, pl.ds(r0, self.rb), pl.ds(k * self.de, self.de)],
                                         self.stage.at[slot, pl.ds(k * self.rb, self.rb), :],
                                         self.sems.at[slot]))

    def start(self, idx, e):
        self._each_copy(idx, e, lambda cp: cp.start())

    def wait(self, idx, e):
        self._each_copy(idx, e, lambda cp: cp.wait())

    def convert(self, idx, dst):
        slot = idx % self.slots
        sr, n_a = self.sr, self.n_a
        for mat in range(2):
            @pl.when((idx >= mat * n_a) & (idx < (mat + 1) * n_a))
            def _(mat=mat):
                r0 = pl.multiple_of((idx - mat * n_a) * sr, sr)
                self.bufs[mat][dst, pl.ds(r0, sr), :] = self.stage[slot].astype(BF16)

        @pl.when(idx >= 2 * n_a)
        def _():
            r0 = pl.multiple_of((idx - 2 * n_a) * self.rb, self.rb)
            for k in range(self.ratio):
                self.bufs[2][dst, pl.ds(r0, self.rb), pl.ds(k * self.de, self.de)] = (
                    self.stage[slot, pl.ds(k * self.rb, self.rb), :].astype(BF16))

    def start_first(self, e):
        for idx in range(min(self.slots, self.n_chunks)):
            self.start(jnp.int32(idx), e)

    def stream(self, e, dst, first, count):
        def body(c, carry):
            idx = first + c
            self.wait(idx, e)
            self.convert(idx, dst)

            @pl.when(idx + self.slots < self.n_chunks)
            def _():
                self.start(idx + self.slots, e)
            return carry

        lax.fori_loop(0, count, body, 0)


def _moe_expert_kernel(be_ref, nused_ref, pos_ref, nexte_ref, par_ref, c0_ref, cn_ref, tok_ref, tok_next_ref,
                       h_hbm, g_ref, wg_hbm, wu_hbm, wd_hbm, y_ref, x_buf, sems, stage, wsems,
                       wg_buf, wu_buf, wd_buf):
    i = pl.program_id(0)
    n_used = nused_ref[0]
    rows = x_buf.shape[1]
    slot = i % 2
    cur = par_ref[i]
    next_e = nexte_ref[i]
    w = _ExpertWeights((wg_hbm, wu_hbm, wd_hbm), stage, wsems, (wg_buf, wu_buf, wd_buf))

    @pl.when((i == 0) & (n_used > 0))
    def _():
        _gather_rows(h_hbm, tok_ref, rows, 1, 0, x_buf.at[0], sems.at[0], wait=False)
        w.start_first(be_ref[0])
        w.stream(be_ref[0], cur, 0, w.n_chunks)

    @pl.when(i + 1 < n_used)
    def _():
        _gather_rows(h_hbm, tok_next_ref, rows, 1, 0, x_buf.at[1 - slot], sems.at[1 - slot], wait=False)

    @pl.when((i < n_used) & (next_e >= 0))
    def _():
        @pl.when(pos_ref[i] == 0)
        def _():
            w.start_first(next_e)

        w.stream(next_e, 1 - cur, c0_ref[i], cn_ref[i])

    @pl.when(i < n_used)
    def _():
        _gather_rows(h_hbm, tok_ref, rows, 1, 0, x_buf.at[slot], sems.at[slot], wait=True)
        x = x_buf[slot]
        x = (x * lax.rsqrt(jnp.mean(x * x, axis=-1, keepdims=True) + EPS) * g_ref[...]).astype(BF16)
        a = jnp.dot(x, wg_buf[cur], preferred_element_type=F32)
        u = jnp.dot(x, wu_buf[cur], preferred_element_type=F32)
        hmid = (a * jax.nn.sigmoid(a) * u).astype(BF16)
        y_ref[...] = jnp.dot(hmid, wd_buf[cur], preferred_element_type=F32)

    @pl.when(i >= n_used)
    def _():
        y_ref[...] = jnp.zeros(y_ref.shape, y_ref.dtype)


def moe_experts(h, norm_g, tok_buf, sched, w_gate, w_up, w_down):
    n, d = h.shape
    blk_e, n_used, run_pos, run_len, next_e, parity = sched
    n_blocks = blk_e.shape[0]
    de = w_gate.shape[2]
    sr = min(W_STAGE_ROWS, d)
    assert d % sr == 0 and d % de == 0 and sr % (d // de) == 0 and de % (sr // (d // de)) == 0
    n_chunks = 2 * (d // sr) + de // (sr // (d // de))
    spread = jnp.maximum(run_len - 1, 1)
    k = jnp.maximum(run_pos - 1, 0)
    base, extra = n_chunks // spread, n_chunks % spread
    single = run_len == 1
    chunk0 = jnp.where(single, 0, k * base + jnp.minimum(k, extra)).astype(jnp.int32)
    chunk_n = jnp.where(single, n_chunks,
                        jnp.where(run_pos == 0, 0, base + (k < extra))).astype(jnp.int32)
    sched = (blk_e, n_used, run_pos, next_e, parity, chunk0, chunk_n)
    tok_blocks = tok_buf.reshape(n_blocks, 1, MOE_BLOCK)
    n_pre = len(sched)
    grid_spec = pltpu.PrefetchScalarGridSpec(
        num_scalar_prefetch=n_pre,
        grid=(n_blocks,),
        in_specs=[pl.BlockSpec((1, 1, MOE_BLOCK), lambda i, *_: (i, 0, 0), memory_space=pltpu.SMEM),
                  pl.BlockSpec((1, 1, MOE_BLOCK), lambda i, *_: (jnp.minimum(i + 1, n_blocks - 1), 0, 0),
                               memory_space=pltpu.SMEM),
                  pl.BlockSpec(memory_space=pl.ANY),
                  pl.BlockSpec((1, d), lambda i, *_: (0, 0)),
                  pl.BlockSpec(memory_space=pl.ANY),
                  pl.BlockSpec(memory_space=pl.ANY),
                  pl.BlockSpec(memory_space=pl.ANY)],
        out_specs=pl.BlockSpec((MOE_BLOCK, d), lambda i, *_: (i, 0)),
        scratch_shapes=[pltpu.VMEM((2, MOE_BLOCK, d), F32), pltpu.SemaphoreType.DMA((2,)),
                        pltpu.VMEM((W_STAGE_SLOTS, sr, de), F32), pltpu.SemaphoreType.DMA((W_STAGE_SLOTS,)),
                        pltpu.VMEM((2, d, de), BF16), pltpu.VMEM((2, d, de), BF16),
                        pltpu.VMEM((2, de, d), BF16)],
    )
    scratch_bytes = (2 * _nbytes((MOE_BLOCK, d), F32) + W_STAGE_SLOTS * _nbytes((sr, de), F32)
                     + 6 * _nbytes((d, de), BF16))
    return pl.pallas_call(
        _moe_expert_kernel,
        grid_spec=grid_spec,
        out_shape=jax.ShapeDtypeStruct((n_blocks * MOE_BLOCK, d), F32),
        compiler_params=pltpu.CompilerParams(
            dimension_semantics=("arbitrary",),
            vmem_limit_bytes=_vmem_limit([_nbytes((MOE_BLOCK, d), F32)],
                                         scratch_bytes + 4 * _nbytes((MOE_BLOCK, d), F32))),
        name="moe_experts",
    )(*sched, tok_blocks, tok_blocks, h, norm_g.reshape(1, d).astype(F32), w_gate, w_up, w_down)


def _combine_kernel(dest_ref, dest_next_ref, h_ref, w_ref, y_hbm, o_ref, buf, sems):
    i = pl.program_id(0)
    rows = buf.shape[2]
    slot = i % 2

    def gather(ids_ref, s, wait):
        for k in range(TOP_K):
            _gather_rows(y_hbm, ids_ref, rows, TOP_K, k, buf.at[s, k], sems.at[s], wait)

    @pl.when(i == 0)
    def _():
        gather(dest_ref, 0, False)

    @pl.when(i + 1 < pl.num_programs(0))
    def _():
        gather(dest_next_ref, 1 - slot, False)

    gather(dest_ref, slot, True)
    w = w_ref[...]
    o_ref[...] = h_ref[...] + (buf[slot, 0] * w[:, 0:1] + buf[slot, 1] * w[:, 1:2])


def moe_combine(h, y_buf, dest, weights, tm=128):
    n, d = h.shape
    tm = _tile(n, tm)
    nt = n // tm
    dest_blocks = dest.reshape(nt, 1, TOP_K * tm)
    return pl.pallas_call(
        _combine_kernel,
        grid=(nt,),
        in_specs=[pl.BlockSpec((1, 1, TOP_K * tm), lambda i: (i, 0, 0), memory_space=pltpu.SMEM),
                  pl.BlockSpec((1, 1, TOP_K * tm), lambda i: (jnp.minimum(i + 1, nt - 1), 0, 0),
                               memory_space=pltpu.SMEM),
                  pl.BlockSpec((tm, d), lambda i: (i, 0)),
                  pl.BlockSpec((tm, LANES), lambda i: (i, 0)),
                  pl.BlockSpec(memory_space=pl.ANY)],
        out_specs=pl.BlockSpec((tm, d), lambda i: (i, 0)),
        out_shape=jax.ShapeDtypeStruct((n, d), F32),
        scratch_shapes=[pltpu.VMEM((2, TOP_K, tm, d), F32), pltpu.SemaphoreType.DMA((2,))],
        compiler_params=pltpu.CompilerParams(
            dimension_semantics=("arbitrary",),
            vmem_limit_bytes=_vmem_limit([2 * _nbytes((tm, d), F32)], 8 * _nbytes((tm, d), F32))),
        name="moe_combine",
    )(dest_blocks, dest_blocks, h, weights, y_buf)


def _moe_dispatch(expert_id, n_experts):
    n = expert_id.shape[0]
    nk = n * TOP_K
    experts = jnp.arange(n_experts, dtype=jnp.int32)

    def lookup(table, idx):
        return jnp.sum(jnp.where(idx[:, None] == experts[None, :], table[None, :], 0), axis=1)

    flat_e = expert_id.reshape(nk).astype(jnp.int32)
    order = jnp.argsort(flat_e).astype(jnp.int32)
    rank = jnp.argsort(order).astype(jnp.int32)
    counts = jnp.sum((flat_e[:, None] == experts[None, :]).astype(jnp.int32), axis=0)
    start = jnp.cumsum(counts) - counts
    padded = (counts + MOE_BLOCK - 1) // MOE_BLOCK * MOE_BLOCK
    pad_end = jnp.cumsum(padded)
    pad_start = pad_end - padded
    n_blocks = (nk + n_experts * (MOE_BLOCK - 1) + MOE_BLOCK - 1) // MOE_BLOCK
    blk = jnp.arange(n_blocks, dtype=jnp.int32)
    blk_e = jnp.minimum(jnp.sum(pad_end[None, :] <= (blk * MOE_BLOCK)[:, None], axis=1),
                        n_experts - 1).astype(jnp.int32)
    dest = (rank + lookup(pad_start - start, flat_e)).astype(jnp.int32)
    blk_pad_start, blk_start, blk_count = (lookup(tbl, blk_e) for tbl in (pad_start, start, counts))
    within = (blk * MOE_BLOCK - blk_pad_start)[:, None] + jnp.arange(MOE_BLOCK, dtype=jnp.int32)[None, :]
    src = jnp.clip(blk_start[:, None] + within, 0, nk - 1)
    tok_buf = jnp.where((within >= 0) & (within < blk_count[:, None]), order[src] // TOP_K, 0)
    tok_buf = tok_buf.reshape(n_blocks * MOE_BLOCK).astype(jnp.int32)
    n_used = (pad_end[-1] // MOE_BLOCK).astype(jnp.int32)
    used = blk < n_used
    is_first = used & (blk_e != jnp.concatenate([jnp.full((1,), -1, jnp.int32), blk_e[:-1]]))
    parity = (jnp.cumsum(is_first.astype(jnp.int32)) - 1) % 2
    run_pos = blk - blk_pad_start // MOE_BLOCK
    run_len = lookup(padded, blk_e) // MOE_BLOCK
    live = jnp.where(counts > 0, experts, n_experts)
    next_live = jnp.concatenate([lax.cummin(live, reverse=True)[1:], jnp.full((1,), n_experts, jnp.int32)])
    blk_next = lookup(next_live, blk_e)
    next_e = jnp.where(blk_next < n_experts, blk_next, -1)
    sched = (blk_e, n_used.reshape(1), run_pos.astype(jnp.int32), run_len.astype(jnp.int32),
             next_e.astype(jnp.int32), parity.astype(jnp.int32))
    return tok_buf, sched, dest


def kernel(x, mem, norm1_g, norm2_g, mem_norm_g, w_in, b_gate, fox_f_bias, fox_q_g, fox_k_g,
           hgrn_lb_logits, hgrn_out_g, mem_q_g, mem_k_g, w_mem_kv, w_branch_fox, w_branch_hgrn,
           w_branch_mem, w_out, w_group, b_group, w_router, b_router, w_exp_gate, w_exp_up,
           w_exp_down):
    batch, seq, d = x.shape
    n_mem = mem.shape[1]
    depth = w_in.shape[0]
    fox_heads, fox_hd = fox_f_bias.shape[1], fox_q_g.shape[1]
    fox_w = fox_heads * fox_hd
    hgrn_w = hgrn_lb_logits.shape[1]
    hgrn_heads = hgrn_w // HGRN_EXPAND
    hgrn_dv = hgrn_out_g.shape[1]
    mem_hd = mem_q_g.shape[1]
    mem_w = w_branch_mem.shape[1]
    mem_heads = mem_w // mem_hd
    n_experts = w_router.shape[2]
    m = batch * seq

    sizes = (fox_w, fox_w, fox_w, fox_heads, hgrn_w, hgrn_w, hgrn_w, hgrn_w, mem_w, 3 * d)
    offs = [0]
    for sz in sizes:
        offs.append(offs[-1] + sz)
    (o_fq, o_fk, o_fv, o_ff, o_hq, o_hf, o_hi, o_hg, o_mq, o_gt, o_end) = offs

    lb_all = jnp.cumsum(jax.nn.softmax(hgrn_lb_logits.astype(F32), axis=0), axis=0)
    mem_n = rmsnorm_rows(mem.reshape(batch * n_mem, d), mem_norm_g, BF16)

    h = x.reshape(m, d)
    for layer in range(depth):
        w_t = jnp.swapaxes(w_in[layer], 0, 1)
        rows = lambda a, b: w_t[a:b].astype(BF16)
        nk = dict(w_is_nk=True)
        hn = rmsnorm_rows(h, norm1_g[layer], BF16)

        qk_gain = jnp.concatenate([jnp.tile(fox_q_g[layer] * (LOG2E * fox_hd ** -0.5), fox_heads),
                                   jnp.tile(fox_k_g[layer], fox_heads)])
        qk = matmul(hn, rows(o_fq, o_fv), BF16, functools.partial(_epi_headnorm, head_dim=fox_hd),
                    row_vec=qk_gain, name="proj_fox_qk", **nk)
        v_t = matmul_t(hn, rows(o_fv, o_ff), BF16, tm=FOX_BLOCK, name="proj_fox_vt")
        w_plain = jnp.concatenate([w_t[o_hq:o_hf], w_t[o_hi:o_mq]], axis=0).astype(BF16)
        plain = matmul(hn, w_plain, BF16, name="proj_plain", **nk)
        hd_blk = lambda width: width // HGRN_EXPAND
        c_hq = 0
        c_hi = c_hq + hd_blk(hgrn_w)
        c_hg = c_hi + hd_blk(hgrn_w)
        z = matmul(hn, rows(o_hf, o_hi), F32, name="proj_hgrn_f", **nk)
        w_ff = jnp.pad(w_t[o_ff:o_hq], ((0, LANES - fox_heads), (0, 0))).astype(BF16)
        ff = matmul(hn, w_ff, F32, tn=LANES, name="proj_fox_f", **nk)
        mq = matmul(hn, rows(o_mq, o_gt), BF16, functools.partial(_epi_headnorm, head_dim=mem_hd),
                    row_vec=jnp.tile(mem_q_g[layer] * mem_hd ** -0.5, mem_heads), tn=mem_hd,
                    name="proj_mem_q", **nk)
        gates = matmul(hn, rows(o_gt, o_end), BF16, _epi_sigmoid_bias, row_vec=b_gate[layer],
                       name="proj_gates", **nk)

        kbias = fox_key_bias(ff, fox_f_bias[layer], batch, fox_heads, fox_hd)
        o_a = fox_attention(qk, kbias, v_t, batch, fox_heads, fox_hd)

        o_b = hgrn2(plain, c_hq, c_hi, c_hg, z, lb_all[layer], hgrn_out_g[layer], batch, hgrn_heads)

        wkv = w_mem_kv[layer]
        mk = matmul(mem_n, wkv[:, :mem_w].astype(BF16), BF16,
                    functools.partial(_epi_headnorm, head_dim=mem_hd),
                    row_vec=jnp.tile(mem_k_g[layer], mem_heads), tn=mem_hd, name="proj_mem_k")
        mv = matmul(mem_n, wkv[:, mem_w:].astype(BF16), BF16, name="proj_mem_v")
        o_c = mem_attention(mq, mk, mv, batch, mem_heads, mem_hd)

        merged = gated_merge(o_a, o_b, o_c, w_branch_fox[layer].astype(BF16),
                             w_branch_hgrn[layer].astype(BF16), w_branch_mem[layer].astype(BF16), gates)
        h = matmul(merged, w_out[layer].astype(BF16), F32, _epi_residual, residual=h, tn=512,
                   name="proj_out")

        expert_id, weights = moe_router(h, norm2_g[layer], w_group[layer], b_group[layer],
                                        w_router[layer], b_router[layer])
        tok_buf, sched, dest = _moe_dispatch(expert_id, n_experts)
        y_buf = moe_experts(h, norm2_g[layer], tok_buf, sched, w_exp_gate[layer].astype(F32),
                            w_exp_up[layer].astype(F32), w_exp_down[layer].astype(F32))
        h = moe_combine(h, y_buf, dest, weights)
    return h.reshape(batch, seq, d)
```

```python
import functools

import jax
import jax.numpy as jnp
from jax import lax
from jax.experimental import pallas as pl
from jax.experimental.pallas import tpu as pltpu

F32 = jnp.float32
BF16 = jnp.bfloat16

EPS = 1e-6
CHUNK = 64
SUB = 16
HGRN_EXPAND = 128
TOP_K = 2
MOE_BLOCK = 128
FOX_BLOCK = 512
FOX_GROUP = 4
LANES = 128
BF16_ROWS = 16
LOG2E = 1.4426950408889634
V7X_VMEM_LIMIT_CAP = 60000 * 1024


def _vmem_limit(block_bytes, temp_bytes=0):
    need = int(1.25 * (2 * sum(block_bytes) + temp_bytes)) + (2 << 20)
    return min(max(need, 16 << 20), V7X_VMEM_LIMIT_CAP)


def _nbytes(shape, dtype):
    n = 1
    for s in shape:
        n *= s
    return n * jnp.dtype(dtype).itemsize


def _tile(dim, want):
    if dim <= want:
        return dim
    for t in range(want - want % LANES, 0, -LANES):
        if dim % t == 0:
            return t
    raise ValueError((dim, want))


def _rmsnorm_kernel(x_ref, g_ref, o_ref):
    x = x_ref[...].astype(F32)
    y = x * lax.rsqrt(jnp.mean(x * x, axis=-1, keepdims=True) + EPS)
    o_ref[...] = (y * g_ref[...]).astype(o_ref.dtype)


def rmsnorm_rows(x, g, out_dtype, tm=256):
    m, d = x.shape
    tm = _tile(m, tm)
    return pl.pallas_call(
        _rmsnorm_kernel,
        grid=(m // tm,),
        in_specs=[pl.BlockSpec((tm, d), lambda i: (i, 0)),
                  pl.BlockSpec((1, d), lambda i: (0, 0))],
        out_specs=pl.BlockSpec((tm, d), lambda i: (i, 0)),
        out_shape=jax.ShapeDtypeStruct((m, d), out_dtype),
        compiler_params=pltpu.CompilerParams(
            dimension_semantics=("parallel",),
            vmem_limit_bytes=_vmem_limit([_nbytes((tm, d), x.dtype), _nbytes((tm, d), out_dtype)],
                                         2 * _nbytes((tm, d), F32))),
        name="rmsnorm_rows",
    )(x, g.reshape(1, d).astype(F32))


def _mm_kernel(a_ref, w_ref, *refs, epilogue, w_is_nk):
    o_ref = refs[-1]
    contract = (((1,), (1 if w_is_nk else 0,)), ((), ()))
    acc = lax.dot_general(a_ref[...], w_ref[...], contract, preferred_element_type=F32)
    epilogue(acc, o_ref, *refs[:-1])


def _epi_plain(acc, o_ref):
    o_ref[...] = acc.astype(o_ref.dtype)


def _epi_headnorm(acc, o_ref, g_ref, *, head_dim):
    for h in range(acc.shape[1] // head_dim):
        sl = slice(h * head_dim, (h + 1) * head_dim)
        blk = acc[:, sl]
        y = blk * lax.rsqrt(jnp.mean(blk * blk, axis=-1, keepdims=True) + EPS)
        o_ref[:, sl] = (y * g_ref[:, sl]).astype(o_ref.dtype)


def _epi_sigmoid_bias(acc, o_ref, b_ref):
    o_ref[...] = jax.nn.sigmoid(acc + b_ref[...]).astype(o_ref.dtype)


def _epi_residual(acc, o_ref, r_ref):
    o_ref[...] = (r_ref[...] + acc).astype(o_ref.dtype)


def _w_rows_spec(w_rows, tn, k):
    starts, first = [], 0
    for row0, rows in w_rows:
        assert rows % tn == 0
        starts.append((first, row0))
        first += rows // tn

    def index_map(i, j):
        row = starts[0][1] + j * tn
        for first_tile, row0 in starts[1:]:
            row = jnp.where(j >= first_tile, row0 + (j - first_tile) * tn, row)
        return pl.multiple_of(row, BF16_ROWS), 0

    assert all(row0 % BF16_ROWS == 0 for row0, _ in w_rows)
    return pl.BlockSpec((pl.Element(tn), pl.Element(k)), index_map)


def matmul(a, w, out_dtype, epilogue=_epi_plain, row_vec=None, residual=None, tm=1024, tn=1024,
           w_is_nk=False, w_rows=None, name="matmul"):
    m, k = a.shape
    if w_rows is not None:
        w_is_nk = True
        n = sum(rows for _, rows in w_rows)
        tn = min(tn, min(rows for _, rows in w_rows))
    else:
        n = w.shape[0 if w_is_nk else 1]
    tm, tn = _tile(m, tm), _tile(n, tn)
    in_specs = [pl.BlockSpec((tm, k), lambda i, j: (i, 0)),
                _w_rows_spec(w_rows, tn, k) if w_rows is not None
                else pl.BlockSpec((tn, k), lambda i, j: (j, 0)) if w_is_nk
                else pl.BlockSpec((k, tn), lambda i, j: (0, j))]
    operands = [a, w]
    blocks = [_nbytes((tm, k), a.dtype), _nbytes((k, tn), w.dtype), _nbytes((tm, tn), out_dtype)]
    if row_vec is not None:
        in_specs.append(pl.BlockSpec((1, tn), lambda i, j: (0, j)))
        operands.append(row_vec.reshape(1, n).astype(F32))
    if residual is not None:
        in_specs.append(pl.BlockSpec((tm, tn), lambda i, j: (i, j)))
        operands.append(residual)
        blocks.append(_nbytes((tm, tn), residual.dtype))
    return pl.pallas_call(
        functools.partial(_mm_kernel, epilogue=epilogue, w_is_nk=w_is_nk),
        grid=(m // tm, n // tn),
        in_specs=in_specs,
        out_specs=pl.BlockSpec((tm, tn), lambda i, j: (i, j)),
        out_shape=jax.ShapeDtypeStruct((m, n), out_dtype),
        compiler_params=pltpu.CompilerParams(
            dimension_semantics=("parallel", "parallel"),
            vmem_limit_bytes=_vmem_limit(blocks, 2 * _nbytes((tm, tn), F32))),
        name=name,
    )(*operands)


def _mm_t_kernel(wt_ref, a_ref, o_ref):
    o_ref[0] = lax.dot_general(wt_ref[...], a_ref[...], (((1,), (1,)), ((), ())),
                               preferred_element_type=F32).astype(o_ref.dtype)


def matmul_t(a, w_t, out_dtype, tm=512, tn=1024, w_rows=None, name="matmul_t"):
    m, k = a.shape
    n = w_t.shape[0] if w_rows is None else sum(rows for _, rows in w_rows)
    tm, tn = _tile(m, tm), _tile(n, tn)
    return pl.pallas_call(
        _mm_t_kernel,
        grid=(m // tm, n // tn),
        in_specs=[pl.BlockSpec((tn, k), lambda i, j: (j, 0)) if w_rows is None else _w_rows_spec(w_rows, tn, k),
                  pl.BlockSpec((tm, k), lambda i, j: (i, 0))],
        out_specs=pl.BlockSpec((1, tn, tm), lambda i, j: (i, j, 0)),
        out_shape=jax.ShapeDtypeStruct((m // tm, n, tm), out_dtype),
        compiler_params=pltpu.CompilerParams(
            dimension_semantics=("parallel", "parallel"),
            vmem_limit_bytes=_vmem_limit([_nbytes((tm, k), a.dtype), _nbytes((tn, k), w_t.dtype),
                                          _nbytes((tn, tm), out_dtype)], 2 * _nbytes((tn, tm), F32))),
        name=name,
    )(w_t, a)


N_SPLIT = 3


def _fox_bias_kernel(f_ref, b_ref, e_ref, o_ref, carry):
    @pl.when(pl.program_id(1) == 0)
    def _():
        carry[...] = jnp.zeros(carry.shape, F32)

    x = f_ref[...] + b_ref[...]
    a = jnp.minimum(x, 0.0) - jnp.log(1.0 + jnp.exp(-jnp.abs(x)))
    r = a.shape[0]
    row = lax.broadcasted_iota(jnp.int32, a.shape, 0)
    sh = 1
    while sh < r:
        a = a + jnp.where(row >= sh, pltpu.roll(a, sh, axis=0), 0.0)
        sh *= 2
    c = a + carry[...]
    carry[...] = c[r - 1:r, :]
    rest = c * (-LOG2E)
    pieces = []
    for _ in range(N_SPLIT):
        piece = rest.astype(BF16)
        pieces.append(piece)
        rest = rest - piece.astype(F32)
    o_ref[...] = jnp.dot(jnp.concatenate(pieces, axis=1), e_ref[...],
                         preferred_element_type=F32).astype(o_ref.dtype)


def fox_key_bias(ff, bias, batch, heads, head_dim, rows=1024):
    m = ff.shape[0]
    s = m // batch
    rows = _tile(s, rows)
    nr = s // rows
    src = jnp.arange(N_SPLIT * LANES)
    dst = (src % LANES) * head_dim + src // LANES
    place = ((jnp.arange(heads * head_dim)[None, :] == dst[:, None])
             & ((src % LANES) < heads)[:, None]).astype(BF16)
    bias_row = jnp.zeros((1, LANES), F32).at[0, :heads].set(bias.astype(F32))
    return pl.pallas_call(
        _fox_bias_kernel,
        grid=(batch, nr),
        in_specs=[pl.BlockSpec((rows, LANES), lambda b, t: (b * nr + t, 0)),
                  pl.BlockSpec((1, LANES), lambda b, t: (0, 0)),
                  pl.BlockSpec((N_SPLIT * LANES, heads * head_dim), lambda b, t: (0, 0))],
        out_specs=pl.BlockSpec((rows, heads * head_dim), lambda b, t: (b * nr + t, 0)),
        out_shape=jax.ShapeDtypeStruct((m, heads * head_dim), BF16),
        scratch_shapes=[pltpu.VMEM((1, LANES), F32)],
        compiler_params=pltpu.CompilerParams(
            dimension_semantics=("parallel", "arbitrary"),
            vmem_limit_bytes=_vmem_limit([_nbytes((rows, heads * head_dim), BF16),
                                          _nbytes((N_SPLIT * LANES, heads * head_dim), BF16)],
                                         _nbytes((rows, heads * head_dim), F32) + 16 * _nbytes((rows, LANES), F32))),
        name="fox_key_bias",
    )(ff, bias_row, place)


SUM_ROWS = 16


KIND_PLAIN, KIND_DIAG, KIND_PAD = 0, 1, 2


def _fox_attn_kernel(row_ref, col_ref, kind_ref, start_ref, q_ref, k_ref, kb_ref, vt_ref, o_ref,
                     qt_scr, mask_scr, s_scr, m_scr, acc_scr, snap_scr, *, n_steps):
    nq, hd2, t = qt_scr.shape
    hd = hd2 // 2

    causal = (lax.broadcasted_iota(jnp.int32, (t, t), 0) <= lax.broadcasted_iota(jnp.int32, (t, t), 1))
    mask_scr[KIND_PLAIN] = jnp.zeros((t, t), F32)
    mask_scr[KIND_DIAG] = jnp.where(causal, 0.0, -jnp.inf)
    mask_scr[KIND_PAD] = jnp.full((t, t), -jnp.inf, F32)

    for r in range(nq):
        qt_scr[r, :hd, :] = q_ref[r * t:(r + 1) * t, :].astype(F32).T.astype(BF16)
        qt_scr[r, hd:, :] = jnp.where(lax.broadcasted_iota(jnp.int32, (hd, t), 0) < N_SPLIT,
                                      1.0, 0.0).astype(BF16)
    m_scr[...] = jnp.zeros(m_scr.shape, F32)
    acc_scr[...] = jnp.zeros(acc_scr.shape, F32)

    def scores(p):
        off = pl.multiple_of(col_ref[p] * t, t)
        k_aug = jnp.concatenate([k_ref[pl.ds(off, t), :], kb_ref[pl.ds(off, t), :]], axis=1)
        return jnp.dot(k_aug, qt_scr[row_ref[p]], preferred_element_type=F32)

    def update(s, p):
        s = s + mask_scr[kind_ref[p]]
        m_prev = jnp.where(start_ref[p] == 1, -jnp.inf, m_scr[...])
        m_new = jnp.maximum(m_prev, jnp.max(s, axis=0, keepdims=True))
        alpha = jnp.exp2(m_prev - m_new)
        prob = jnp.exp2((s - m_new).astype(BF16))
        vt_aug = jnp.concatenate([vt_ref[0, col_ref[p]], jnp.ones((SUM_ROWS, t), BF16)], axis=0)
        acc = alpha * acc_scr[...] + jnp.dot(vt_aug, prob, preferred_element_type=F32)
        m_scr[...] = m_new
        acc_scr[...] = acc
        return acc

    def write_row(p, acc_ref):
        @pl.when(kind_ref[p] == KIND_DIAG)
        def _():
            out = acc_ref[:hd, :] * (1.0 / acc_ref[hd:hd + 1, :])
            o_ref[pl.ds(pl.multiple_of(row_ref[p] * t, t), t), :] = out.T.astype(o_ref.dtype)

    group = s_scr.shape[0]
    s_scr[0] = scores(0)

    def pair_group(u, carry):
        p = group * u
        for k in range(group):
            s_scr[(k + 1) % group] = scores(p + k + 1)
            acc = update(s_scr[k], p + k)
            if k + 1 < group:
                snap_scr[k] = acc
        for k in range(group):
            write_row(p + k, snap_scr.at[k] if k + 1 < group else acc_scr)
        return carry

    lax.fori_loop(0, n_steps, pair_group, 0)


def fox_attention(qk, kbias, v_t, batch, heads, head_dim):
    m = qk.shape[0]
    s = m // batch
    t = v_t.shape[2]
    nq = s // t
    pairs = [(i, j, KIND_DIAG if i == j else KIND_PLAIN, int(j == 0)) for i in range(nq) for j in range(i + 1)]
    n_steps = -(-len(pairs) // FOX_GROUP)
    pairs += [(nq - 1, nq - 1, KIND_PAD, 0)] * (FOX_GROUP * n_steps + 1 - len(pairs))
    tables = [jnp.asarray(v, jnp.int32) for v in zip(*pairs)]
    grid_spec = pltpu.PrefetchScalarGridSpec(
        num_scalar_prefetch=len(tables),
        grid=(batch, heads),
        in_specs=[pl.BlockSpec((s, head_dim), lambda b, h, *_: (b, h)),
                  pl.BlockSpec((s, head_dim), lambda b, h, *_: (b, heads + h)),
                  pl.BlockSpec((s, head_dim), lambda b, h, *_: (b, h)),
                  pl.BlockSpec((1, nq, head_dim, t), lambda b, h, *_: (b, 0, h, 0))],
        out_specs=pl.BlockSpec((s, head_dim), lambda b, h, *_: (b, h)),
        scratch_shapes=[pltpu.VMEM((nq, 2 * head_dim, t), BF16), pltpu.VMEM((3, t, t), F32),
                        pltpu.VMEM((FOX_GROUP, t, t), F32),
                        pltpu.VMEM((1, t), F32), pltpu.VMEM((head_dim + SUM_ROWS, t), F32),
                        pltpu.VMEM((FOX_GROUP - 1, head_dim + SUM_ROWS, t), F32)],
    )
    return pl.pallas_call(
        functools.partial(_fox_attn_kernel, n_steps=n_steps),
        grid_spec=grid_spec,
        out_shape=jax.ShapeDtypeStruct((m, heads * head_dim), BF16),
        compiler_params=pltpu.CompilerParams(
            dimension_semantics=("parallel", "parallel"),
            vmem_limit_bytes=_vmem_limit([5 * _nbytes((s, head_dim), BF16)],
                                         2 * _nbytes((s, head_dim), BF16) + 10 * _nbytes((t, t), F32))),
        name="fox_attention",
    )(*tables, qk, qk, kbias, v_t.reshape(batch, nq, heads * head_dim, t))


def _hgrn_kernel(q_ref, z_ref, v_ref, hg_ref, lb_ref, og_ref, o_ref, st_ref):
    @pl.when(pl.program_id(2) == 0)
    def _():
        st_ref[...] = jnp.zeros(st_ref.shape, F32)

    t, dk = z_ref.shape
    dv = v_ref.shape[1]
    nc, nb, nsub = t // CHUNK, t // SUB, CHUNK // SUB

    q = q_ref[...].astype(F32)
    z = z_ref[...]
    lb = lb_ref[...]
    g = jnp.log(lb + (1.0 - lb) * jax.nn.sigmoid(z))
    kk = (1.0 - lb) * jax.nn.sigmoid(-z)

    row = lax.broadcasted_iota(jnp.int32, (t, dk), 0)
    rin = row % CHUNK
    b = g
    sh = 1
    while sh < CHUNK:
        b = b + jnp.where(rin >= sh, pltpu.roll(b, sh, axis=0), 0.0)
        sh *= 2

    b3 = b.reshape(nc, CHUNK, dk)
    q3 = q.reshape(nc, CHUNK, dk)
    k3 = kk.reshape(nc, CHUNK, dk)
    v3 = v_ref[...].reshape(nc, CHUNK, dv)
    b_last = b3[:, CHUNK - 1:CHUNK, :]
    q_in = (q3 * jnp.exp(b3)).astype(BF16)
    k_out = (k3 * jnp.exp(b_last - b3)).astype(BF16)
    d_last = jnp.exp(b_last)

    st = st_ref[...]
    inter = []
    for c in range(nc):
        inter.append(lax.dot_general(q_in[c], st.astype(BF16), (((1,), (1,)), ((), ())),
                                     preferred_element_type=F32))
        kv_t = lax.dot_general(v3[c], k_out[c], (((0,), (0,)), ((), ())),
                               preferred_element_type=F32)
        st = st * d_last[c] + kv_t
    st_ref[...] = st
    inter = jnp.concatenate(inter, axis=0)

    q_parts, k_parts = [], []
    for blk in range(1, nsub):
        lo, hi = blk * SUB, (blk + 1) * SUB
        b_ref_pt = b3[:, lo - 1:lo, :]
        q_hat = (q3[:, lo:hi, :] * jnp.exp(b3[:, lo:hi, :] - b_ref_pt)).astype(BF16)
        k_hat = (k3[:, :lo, :] * jnp.exp(b_ref_pt - b3[:, :lo, :])).astype(BF16)
        q_rows = [jnp.zeros((nc, lo, dk), BF16), q_hat] + ([jnp.zeros((nc, CHUNK - hi, dk), BF16)] if hi < CHUNK else [])
        q_parts.append(jnp.concatenate(q_rows, axis=1))
        k_parts.append(jnp.concatenate([k_hat, jnp.zeros((nc, CHUNK - lo, dk), BF16)], axis=1))
    q_cat = jnp.concatenate(q_parts, axis=2)
    k_cat = jnp.concatenate(k_parts, axis=2)
    s_off = lax.dot_general(q_cat, k_cat, (((2,), (2,)), ((0,), (0,))),
                            preferred_element_type=F32)

    b4 = b.reshape(nb, SUB, dk)
    q4 = q.reshape(nb, SUB, dk)
    k4 = kk.reshape(nb, SUB, dk)
    rsub = (row % SUB).reshape(nb, SUB, dk)
    u_row = lax.broadcasted_iota(jnp.int32, (dk, LANES), 1)
    half = SUB // 2
    s_lo = jnp.zeros((t, LANES), F32)
    s_hi = jnp.zeros((t // 2, LANES), F32)
    for j in range(SUB):
        r0 = 0 if j < half else half
        bj, kj = b4[:, j:j + 1, :], k4[:, j:j + 1, :]
        decay = jnp.exp(jnp.where(rsub[:, r0:, :] >= j, b4[:, r0:, :] - bj, -jnp.inf))
        p = (q4[:, r0:, :] * (kj * decay)).reshape(-1, dk).astype(BF16)
        place = jnp.where((u_row % SUB == j) & (u_row < CHUNK), 1.0, 0.0).astype(BF16)
        if j < half:
            s_lo = s_lo + jnp.dot(p, place, preferred_element_type=F32)
        else:
            s_hi = s_hi + jnp.dot(p, place, preferred_element_type=F32)
    s_diag = s_lo.reshape(nb, SUB, LANES) + jnp.concatenate(
        [jnp.zeros((nb, half, LANES), F32), s_hi.reshape(nb, half, LANES)], axis=1)
    s_diag = s_diag.reshape(nc, CHUNK, LANES)[:, :, :CHUNK]

    rb = lax.broadcasted_iota(jnp.int32, (nc, CHUNK, CHUNK), 1) // SUB
    cb = lax.broadcasted_iota(jnp.int32, (nc, CHUNK, CHUNK), 2) // SUB
    scores = jnp.where(cb == rb, s_diag, jnp.where(cb < rb, s_off, 0.0)).astype(BF16)
    intra = lax.dot_general(scores, v3, (((2,), (1,)), ((0,), (0,))),
                            preferred_element_type=F32).reshape(t, dv)

    o = inter + intra
    o = o * lax.rsqrt(jnp.mean(o * o, axis=-1, keepdims=True) + EPS) * og_ref[...]
    hg = hg_ref[...].astype(F32)
    o_ref[...] = (o * (hg * jax.nn.sigmoid(hg))).astype(o_ref.dtype)


def hgrn2(src, q_col0, v_col0, hg_col0, z, lb, out_g, batch, heads, rows=512):
    m = z.shape[0]
    s = m // batch
    dk = HGRN_EXPAND
    dv = out_g.shape[-1]
    rows = _tile(s, rows)
    nt = s // rows
    row_blk = lambda b, h, t: b * nt + t
    return pl.pallas_call(
        _hgrn_kernel,
        grid=(batch, heads, nt),
        in_specs=[pl.BlockSpec((rows, dk), lambda b, h, t: (row_blk(b, h, t), q_col0 + h)),
                  pl.BlockSpec((rows, dk), lambda b, h, t: (row_blk(b, h, t), h)),
                  pl.BlockSpec((rows, dv), lambda b, h, t: (row_blk(b, h, t), v_col0 + h)),
                  pl.BlockSpec((rows, dv), lambda b, h, t: (row_blk(b, h, t), hg_col0 + h)),
                  pl.BlockSpec((1, dk), lambda b, h, t: (0, h)),
                  pl.BlockSpec((1, dv), lambda b, h, t: (0, 0))],
        out_specs=pl.BlockSpec((rows, dv), lambda b, h, t: (row_blk(b, h, t), h)),
        out_shape=jax.ShapeDtypeStruct((m, heads * dv), BF16),
        scratch_shapes=[pltpu.VMEM((dv, dk), F32)],
        compiler_params=pltpu.CompilerParams(
            dimension_semantics=("parallel", "parallel", "arbitrary"),
            vmem_limit_bytes=_vmem_limit([6 * _nbytes((rows, dk), F32)], 40 * _nbytes((rows, dk), F32))),
        name="hgrn2",
    )(src, z, src, src, lb.reshape(1, -1).astype(F32), out_g.reshape(1, dv).astype(F32))


def _mem_attn_kernel(q_ref, k_ref, v_ref, o_ref):
    s = lax.dot_general(q_ref[...], k_ref[...], (((1,), (1,)), ((), ())), preferred_element_type=F32)
    m = jnp.max(s, axis=1, keepdims=True)
    p = jnp.exp(s - m)
    l = jnp.sum(p, axis=1, keepdims=True)
    o = jnp.dot(p.astype(BF16), v_ref[...], preferred_element_type=F32)
    o_ref[...] = (o / l).astype(o_ref.dtype)


def mem_attention(q, mk, mv, batch, heads, head_dim, tq=1024):
    m = q.shape[0]
    s = m // batch
    n_mem = mk.shape[0] // batch
    tq = _tile(s, tq)
    nq = s // tq
    return pl.pallas_call(
        _mem_attn_kernel,
        grid=(batch, heads, nq),
        in_specs=[pl.BlockSpec((tq, head_dim), lambda b, h, i: (b * nq + i, h)),
                  pl.BlockSpec((n_mem, head_dim), lambda b, h, i: (b, h)),
                  pl.BlockSpec((n_mem, head_dim), lambda b, h, i: (b, h))],
        out_specs=pl.BlockSpec((tq, head_dim), lambda b, h, i: (b * nq + i, h)),
        out_shape=jax.ShapeDtypeStruct((m, heads * head_dim), BF16),
        compiler_params=pltpu.CompilerParams(
            dimension_semantics=("parallel", "parallel", "parallel"),
            vmem_limit_bytes=_vmem_limit([2 * _nbytes((tq, head_dim), BF16), 2 * _nbytes((n_mem, head_dim), BF16)],
                                         4 * _nbytes((tq, head_dim), F32))),
        name="mem_attention",
    )(q, mk, mv)


def _merge_kernel(a_ref, b_ref, c_ref, wa_ref, wb_ref, wc_ref, ga_ref, gb_ref, gc_ref, o_ref):
    acc = ga_ref[...].astype(F32) * jnp.dot(a_ref[...], wa_ref[...], preferred_element_type=F32)
    acc = acc + gb_ref[...].astype(F32) * jnp.dot(b_ref[...], wb_ref[...], preferred_element_type=F32)
    acc = acc + gc_ref[...].astype(F32) * jnp.dot(c_ref[...], wc_ref[...], preferred_element_type=F32)
    o_ref[...] = acc.astype(o_ref.dtype)


def gated_merge(o_a, o_b, o_c, w_a, w_b, w_c, gates, tm=1024, tn=512):
    m = o_a.shape[0]
    d = w_a.shape[1]
    tm, tn = _tile(m, tm), _tile(d, tn)
    nj = d // tn
    a_spec = lambda arr: pl.BlockSpec((tm, arr.shape[1]), lambda i, j: (i, 0))
    w_spec = lambda arr: pl.BlockSpec((arr.shape[0], tn), lambda i, j: (0, j))
    g_spec = lambda r: pl.BlockSpec((tm, tn), lambda i, j: (i, r * nj + j))
    blocks = ([_nbytes((tm, a.shape[1]), BF16) for a in (o_a, o_b, o_c)]
              + [_nbytes((w.shape[0], tn), BF16) for w in (w_a, w_b, w_c)] + 4 * [_nbytes((tm, tn), BF16)])
    return pl.pallas_call(
        _merge_kernel,
        grid=(m // tm, nj),
        in_specs=[a_spec(o_a), a_spec(o_b), a_spec(o_c), w_spec(w_a), w_spec(w_b), w_spec(w_c),
                  g_spec(0), g_spec(1), g_spec(2)],
        out_specs=pl.BlockSpec((tm, tn), lambda i, j: (i, j)),
        out_shape=jax.ShapeDtypeStruct((m, d), BF16),
        compiler_params=pltpu.CompilerParams(
            dimension_semantics=("parallel", "parallel"),
            vmem_limit_bytes=_vmem_limit(blocks, 3 * _nbytes((tm, tn), F32))),
        name="gated_merge",
    )(o_a, o_b, o_c, w_a, w_b, w_c, gates, gates, gates)


def _router_kernel(h_ref, g_ref, whi_ref, wlo_ref, b_ref, lg_ref, id_ref, wt_ref, *, n_groups):
    x = h_ref[...]
    hn = x * lax.rsqrt(jnp.mean(x * x, axis=-1, keepdims=True) + EPS) * g_ref[...]
    hn_hi = hn.astype(BF16)
    hn_lo = (hn - hn_hi.astype(F32)).astype(BF16)
    logits = (jnp.dot(hn_hi, whi_ref[...], preferred_element_type=F32)
              + (jnp.dot(hn_hi, wlo_ref[...], preferred_element_type=F32)
                 + jnp.dot(hn_lo, whi_ref[...], preferred_element_type=F32))) + b_ref[...]
    lane = lax.broadcasted_iota(jnp.int32, logits.shape, 1).astype(F32)
    lane_grp = lg_ref[...]
    neg = -jnp.inf
    big = float(LANES)

    gl = jnp.where(lane < n_groups, logits, neg)
    gmax = jnp.max(gl, axis=1, keepdims=True)
    grp = jnp.min(jnp.where(gl == gmax, lane, big), axis=1, keepdims=True)
    p_grp = 1.0 / jnp.sum(jnp.exp(gl - gmax), axis=1, keepdims=True)

    el = jnp.where(lane_grp == grp, logits, neg)
    v1 = jnp.max(el, axis=1, keepdims=True)
    i1 = jnp.min(jnp.where(el == v1, lane, big), axis=1, keepdims=True)
    el2 = jnp.where(lane == i1, neg, el)
    v2 = jnp.max(el2, axis=1, keepdims=True)
    i2 = jnp.min(jnp.where(el2 == v2, lane, big), axis=1, keepdims=True)
    e21 = jnp.exp(v2 - v1)
    w1 = p_grp / (1.0 + e21)
    w2 = p_grp * e21 / (1.0 + e21)

    id_ref[...] = jnp.where(lane == 0.0, i1 - n_groups, i2 - n_groups).astype(jnp.int32)
    wt_ref[...] = jnp.where(lane == 0.0, w1, w2)


def moe_router(h, norm_g, w_group, b_group, w_router, b_router, tm=256):
    n, d = h.shape
    n_groups = w_group.shape[1]
    n_experts = w_router.shape[1]
    per_group = n_experts // n_groups
    assert n_groups + n_experts <= LANES
    pad = LANES - n_groups - n_experts
    w = jnp.concatenate([w_group, w_router, jnp.zeros((d, pad), F32)], axis=1).astype(F32)
    bias = jnp.concatenate([b_group, b_router, jnp.zeros((pad,), F32)]).reshape(1, LANES).astype(F32)
    w_hi = w.astype(BF16)
    w_lo = (w - w_hi.astype(F32)).astype(BF16)
    lane = jnp.arange(LANES)
    lane_grp = jnp.where((lane >= n_groups) & (lane < n_groups + n_experts),
                         (lane - n_groups) // per_group, -1).astype(F32).reshape(1, LANES)
    tm = _tile(n, tm)
    ids, wts = pl.pallas_call(
        functools.partial(_router_kernel, n_groups=n_groups),
        grid=(n // tm,),
        in_specs=[pl.BlockSpec((tm, d), lambda i: (i, 0)),
                  pl.BlockSpec((1, d), lambda i: (0, 0)),
                  pl.BlockSpec((d, LANES), lambda i: (0, 0)),
                  pl.BlockSpec((d, LANES), lambda i: (0, 0)),
                  pl.BlockSpec((1, LANES), lambda i: (0, 0)),
                  pl.BlockSpec((1, LANES), lambda i: (0, 0))],
        out_specs=[pl.BlockSpec((tm, LANES), lambda i: (i, 0)),
                   pl.BlockSpec((tm, LANES), lambda i: (i, 0))],
        out_shape=[jax.ShapeDtypeStruct((n, LANES), jnp.int32),
                   jax.ShapeDtypeStruct((n, LANES), F32)],
        compiler_params=pltpu.CompilerParams(
            dimension_semantics=("parallel",),
            vmem_limit_bytes=_vmem_limit([_nbytes((tm, d), F32), _nbytes((d, LANES), F32)],
                                         8 * _nbytes((tm, d), F32))),
        name="moe_router",
    )(h, norm_g.reshape(1, d).astype(F32), w_hi, w_lo, bias, lane_grp)
    return ids[:, :TOP_K], wts


def _row_copy(src_hbm, row, dst, slot, sem):
    return pltpu.make_async_copy(src_hbm.at[pl.ds(row, 1)], dst.at[pl.ds(slot, 1)], sem)


GATHER_UNROLL = 8


def _gather_rows(src_hbm, ids_ref, n_ids, stride, offset, dst, sem, wait):
    def body(r, c):
        cp = _row_copy(src_hbm, ids_ref[0, 0, stride * r + offset], dst, r, sem)
        if wait:
            cp.wait()
        else:
            cp.start()
        return c

    lax.fori_loop(0, n_ids, body, 0, unroll=GATHER_UNROLL)


W_STAGE_ROWS = 512
W_STAGE_SLOTS = 8


class _ExpertWeights:
    def __init__(self, hbm, stage, sems, bufs):
        self.hbm, self.stage, self.sems, self.bufs = hbm, stage, sems, bufs
        d, de = hbm[0].shape[1], hbm[0].shape[2]
        self.de = de
        self.sr = stage.shape[1]
        self.slots = stage.shape[0]
        self.ratio = d // de
        self.rb = self.sr // self.ratio
        self.n_a = d // self.sr
        self.n_chunks = 2 * self.n_a + de // self.rb

    def _each_copy(self, idx, e, fn):
        slot = idx % self.slots
        sr, n_a = self.sr, self.n_a
        for mat in range(2):
            @pl.when((idx >= mat * n_a) & (idx < (mat + 1) * n_a))
            def _(mat=mat):
                r0 = pl.multiple_of((idx - mat * n_a) * sr, sr)
                fn(pltpu.make_async_copy(self.hbm[mat].at[e, pl.ds(r0, sr), :], self.stage.at[slot],
                                         self.sems.at[slot]))

        @pl.when(idx >= 2 * n_a)
        def _():
            r0 = pl.multiple_of((idx - 2 * n_a) * self.rb, self.rb)
            for k in range(self.ratio):
                fn(pltpu.make_async_copy(self.hbm[2].at[e, pl.ds(r0, self.rb), pl.ds(k * self.de, self.de)],
                                         self.stage.at[slot, pl.ds(k * self.rb, self.rb), :],
                                         self.sems.at[slot]))

    def start(self, idx, e):
        self._each_copy(idx, e, lambda cp: cp.start())

    def wait(self, idx, e):
        self._each_copy(idx, e, lambda cp: cp.wait())

    def convert(self, idx, dst):
        slot = idx % self.slots
        sr, n_a = self.sr, self.n_a
        for mat in range(2):
            @pl.when((idx >= mat * n_a) & (idx < (mat + 1) * n_a))
            def _(mat=mat):
                r0 = pl.multiple_of((idx - mat * n_a) * sr, sr)
                self.bufs[mat][dst, pl.ds(r0, sr), :] = self.stage[slot].astype(BF16)

        @pl.when(idx >= 2 * n_a)
        def _():
            r0 = pl.multiple_of((idx - 2 * n_a) * self.rb, self.rb)
            for k in range(self.ratio):
                self.bufs[2][dst, pl.ds(r0, self.rb), pl.ds(k * self.de, self.de)] = (
                    self.stage[slot, pl.ds(k * self.rb, self.rb), :].astype(BF16))

    def start_first(self, e):
        for idx in range(min(self.slots, self.n_chunks)):
            self.start(jnp.int32(idx), e)

    def stream(self, e, dst, first, count):
        def body(c, carry):
            idx = first + c
            self.wait(idx, e)
            self.convert(idx, dst)

            @pl.when(idx + self.slots < self.n_chunks)
            def _():
                self.start(idx + self.slots, e)
            return carry

        lax.fori_loop(0, count, body, 0)


def _moe_expert_kernel(be_ref, nused_ref, pos_ref, nexte_ref, par_ref, c0_ref, cn_ref, tok_ref, tok_next_ref,
                       h_hbm, g_ref, wg_hbm, wu_hbm, wd_hbm, y_ref, x_buf, sems, stage, wsems,
                       wg_buf, wu_buf, wd_buf):
    i = pl.program_id(0)
    n_used = nused_ref[0]
    rows = x_buf.shape[1]
    slot = i % 2
    cur = par_ref[i]
    next_e = nexte_ref[i]
    w = _ExpertWeights((wg_hbm, wu_hbm, wd_hbm), stage, wsems, (wg_buf, wu_buf, wd_buf))

    @pl.when((i == 0) & (n_used > 0))
    def _():
        _gather_rows(h_hbm, tok_ref, rows, 1, 0, x_buf.at[0], sems.at[0], wait=False)
        w.start_first(be_ref[0])
        w.stream(be_ref[0], cur, 0, w.n_chunks)

    @pl.when(i + 1 < n_used)
    def _():
        _gather_rows(h_hbm, tok_next_ref, rows, 1, 0, x_buf.at[1 - slot], sems.at[1 - slot], wait=False)

    @pl.when((i < n_used) & (next_e >= 0))
    def _():
        @pl.when(pos_ref[i] == 0)
        def _():
            w.start_first(next_e)

        w.stream(next_e, 1 - cur, c0_ref[i], cn_ref[i])

    @pl.when(i < n_used)
    def _():
        _gather_rows(h_hbm, tok_ref, rows, 1, 0, x_buf.at[slot], sems.at[slot], wait=True)
        x = x_buf[slot]
        x = (x * lax.rsqrt(jnp.mean(x * x, axis=-1, keepdims=True) + EPS) * g_ref[...]).astype(BF16)
        a = jnp.dot(x, wg_buf[cur], preferred_element_type=F32)
        u = jnp.dot(x, wu_buf[cur], preferred_element_type=F32)
        hmid = (a * jax.nn.sigmoid(a) * u).astype(BF16)
        y_ref[...] = jnp.dot(hmid, wd_buf[cur], preferred_element_type=F32)

    @pl.when(i >= n_used)
    def _():
        y_ref[...] = jnp.zeros(y_ref.shape, y_ref.dtype)


def moe_experts(h, norm_g, tok_buf, sched, w_gate, w_up, w_down):
    n, d = h.shape
    blk_e, n_used, run_pos, run_len, next_e, parity = sched
    n_blocks = blk_e.shape[0]
    de = w_gate.shape[2]
    sr = min(W_STAGE_ROWS, d)
    assert d % sr == 0 and d % de == 0 and sr % (d // de) == 0 and de % (sr // (d // de)) == 0
    n_chunks = 2 * (d // sr) + de // (sr // (d // de))
    spread = jnp.maximum(run_len - 1, 1)
    k = jnp.maximum(run_pos - 1, 0)
    base, extra = n_chunks // spread, n_chunks % spread
    single = run_len == 1
    chunk0 = jnp.where(single, 0, k * base + jnp.minimum(k, extra)).astype(jnp.int32)
    chunk_n = jnp.where(single, n_chunks,
                        jnp.where(run_pos == 0, 0, base + (k < extra))).astype(jnp.int32)
    sched = (blk_e, n_used, run_pos, next_e, parity, chunk0, chunk_n)
    tok_blocks = tok_buf.reshape(n_blocks, 1, MOE_BLOCK)
    n_pre = len(sched)
    grid_spec = pltpu.PrefetchScalarGridSpec(
        num_scalar_prefetch=n_pre,
        grid=(n_blocks,),
        in_specs=[pl.BlockSpec((1, 1, MOE_BLOCK), lambda i, *_: (i, 0, 0), memory_space=pltpu.SMEM),
                  pl.BlockSpec((1, 1, MOE_BLOCK), lambda i, *_: (jnp.minimum(i + 1, n_blocks - 1), 0, 0),
                               memory_space=pltpu.SMEM),
                  pl.BlockSpec(memory_space=pl.ANY),
                  pl.BlockSpec((1, d), lambda i, *_: (0, 0)),
                  pl.BlockSpec(memory_space=pl.ANY),
                  pl.BlockSpec(memory_space=pl.ANY),
                  pl.BlockSpec(memory_space=pl.ANY)],
        out_specs=pl.BlockSpec((MOE_BLOCK, d), lambda i, *_: (i, 0)),
        scratch_shapes=[pltpu.VMEM((2, MOE_BLOCK, d), F32), pltpu.SemaphoreType.DMA((2,)),
                        pltpu.VMEM((W_STAGE_SLOTS, sr, de), F32), pltpu.SemaphoreType.DMA((W_STAGE_SLOTS,)),
                        pltpu.VMEM((2, d, de), BF16), pltpu.VMEM((2, d, de), BF16),
                        pltpu.VMEM((2, de, d), BF16)],
    )
    scratch_bytes = (2 * _nbytes((MOE_BLOCK, d), F32) + W_STAGE_SLOTS * _nbytes((sr, de), F32)
                     + 6 * _nbytes((d, de), BF16))
    return pl.pallas_call(
        _moe_expert_kernel,
        grid_spec=grid_spec,
        out_shape=jax.ShapeDtypeStruct((n_blocks * MOE_BLOCK, d), F32),
        compiler_params=pltpu.CompilerParams(
            dimension_semantics=("arbitrary",),
            vmem_limit_bytes=_vmem_limit([_nbytes((MOE_BLOCK, d), F32)],
                                         scratch_bytes + 4 * _nbytes((MOE_BLOCK, d), F32))),
        name="moe_experts",
    )(*sched, tok_blocks, tok_blocks, h, norm_g.reshape(1, d).astype(F32), w_gate, w_up, w_down)


def _combine_kernel(dest_ref, dest_next_ref, h_ref, w_ref, y_hbm, o_ref, buf, sems):
    i = pl.program_id(0)
    rows = buf.shape[2]
    slot = i % 2

    def gather(ids_ref, s, wait):
        for k in range(TOP_K):
            _gather_rows(y_hbm, ids_ref, rows, TOP_K, k, buf.at[s, k], sems.at[s], wait)

    @pl.when(i == 0)
    def _():
        gather(dest_ref, 0, False)

    @pl.when(i + 1 < pl.num_programs(0))
    def _():
        gather(dest_next_ref, 1 - slot, False)

    gather(dest_ref, slot, True)
    w = w_ref[...]
    o_ref[...] = h_ref[...] + (buf[slot, 0] * w[:, 0:1] + buf[slot, 1] * w[:, 1:2])


def moe_combine(h, y_buf, dest, weights, tm=128):
    n, d = h.shape
    tm = _tile(n, tm)
    nt = n // tm
    dest_blocks = dest.reshape(nt, 1, TOP_K * tm)
    return pl.pallas_call(
        _combine_kernel,
        grid=(nt,),
        in_specs=[pl.BlockSpec((1, 1, TOP_K * tm), lambda i: (i, 0, 0), memory_space=pltpu.SMEM),
                  pl.BlockSpec((1, 1, TOP_K * tm), lambda i: (jnp.minimum(i + 1, nt - 1), 0, 0),
                               memory_space=pltpu.SMEM),
                  pl.BlockSpec((tm, d), lambda i: (i, 0)),
                  pl.BlockSpec((tm, LANES), lambda i: (i, 0)),
                  pl.BlockSpec(memory_space=pl.ANY)],
        out_specs=pl.BlockSpec((tm, d), lambda i: (i, 0)),
        out_shape=jax.ShapeDtypeStruct((n, d), F32),
        scratch_shapes=[pltpu.VMEM((2, TOP_K, tm, d), F32), pltpu.SemaphoreType.DMA((2,))],
        compiler_params=pltpu.CompilerParams(
            dimension_semantics=("arbitrary",),
            vmem_limit_bytes=_vmem_limit([2 * _nbytes((tm, d), F32)], 8 * _nbytes((tm, d), F32))),
        name="moe_combine",
    )(dest_blocks, dest_blocks, h, weights, y_buf)


def _moe_dispatch(expert_id, n_experts):
    n = expert_id.shape[0]
    nk = n * TOP_K
    experts = jnp.arange(n_experts, dtype=jnp.int32)

    def lookup(table, idx):
        return jnp.sum(jnp.where(idx[:, None] == experts[None, :], table[None, :], 0), axis=1)

    flat_e = expert_id.reshape(nk).astype(jnp.int32)
    order = jnp.argsort(flat_e).astype(jnp.int32)
    rank = jnp.argsort(order).astype(jnp.int32)
    counts = jnp.sum((flat_e[:, None] == experts[None, :]).astype(jnp.int32), axis=0)
    start = jnp.cumsum(counts) - counts
    padded = (counts + MOE_BLOCK - 1) // MOE_BLOCK * MOE_BLOCK
    pad_end = jnp.cumsum(padded)
    pad_start = pad_end - padded
    n_blocks = (nk + n_experts * (MOE_BLOCK - 1) + MOE_BLOCK - 1) // MOE_BLOCK
    blk = jnp.arange(n_blocks, dtype=jnp.int32)
    blk_e = jnp.minimum(jnp.sum(pad_end[None, :] <= (blk * MOE_BLOCK)[:, None], axis=1),
                        n_experts - 1).astype(jnp.int32)
    dest = (rank + lookup(pad_start - start, flat_e)).astype(jnp.int32)
    blk_pad_start, blk_start, blk_count = (lookup(tbl, blk_e) for tbl in (pad_start, start, counts))
    within = (blk * MOE_BLOCK - blk_pad_start)[:, None] + jnp.arange(MOE_BLOCK, dtype=jnp.int32)[None, :]
    src = jnp.clip(blk_start[:, None] + within, 0, nk - 1)
    tok_buf = jnp.where((within >= 0) & (within < blk_count[:, None]), order[src] // TOP_K, 0)
    tok_buf = tok_buf.reshape(n_blocks * MOE_BLOCK).astype(jnp.int32)
    n_used = (pad_end[-1] // MOE_BLOCK).astype(jnp.int32)
    used = blk < n_used
    is_first = used & (blk_e != jnp.concatenate([jnp.full((1,), -1, jnp.int32), blk_e[:-1]]))
    parity = (jnp.cumsum(is_first.astype(jnp.int32)) - 1) % 2
    run_pos = blk - blk_pad_start // MOE_BLOCK
    run_len = lookup(padded, blk_e) // MOE_BLOCK
    live = jnp.where(counts > 0, experts, n_experts)
    next_live = jnp.concatenate([lax.cummin(live, reverse=True)[1:], jnp.full((1,), n_experts, jnp.int32)])
    blk_next = lookup(next_live, blk_e)
    next_e = jnp.where(blk_next < n_experts, blk_next, -1)
    sched = (blk_e, n_used.reshape(1), run_pos.astype(jnp.int32), run_len.astype(jnp.int32),
             next_e.astype(jnp.int32), parity.astype(jnp.int32))
    return tok_buf, sched, dest


def kernel(x, mem, norm1_g, norm2_g, mem_norm_g, w_in, b_gate, fox_f_bias, fox_q_g, fox_k_g,
           hgrn_lb_logits, hgrn_out_g, mem_q_g, mem_k_g, w_mem_kv, w_branch_fox, w_branch_hgrn,
           w_branch_mem, w_out, w_group, b_group, w_router, b_router, w_exp_gate, w_exp_up,
           w_exp_down):
    batch, seq, d = x.shape
    n_mem = mem.shape[1]
    depth = w_in.shape[0]
    fox_heads, fox_hd = fox_f_bias.shape[1], fox_q_g.shape[1]
    fox_w = fox_heads * fox_hd
    hgrn_w = hgrn_lb_logits.shape[1]
    hgrn_heads = hgrn_w // HGRN_EXPAND
    hgrn_dv = hgrn_out_g.shape[1]
    mem_hd = mem_q_g.shape[1]
    mem_w = w_branch_mem.shape[1]
    mem_heads = mem_w // mem_hd
    n_experts = w_router.shape[2]
    m = batch * seq

    sizes = (fox_w, fox_w, fox_w, fox_heads, hgrn_w, hgrn_w, hgrn_w, hgrn_w, mem_w, 3 * d)
    offs = [0]
    for sz in sizes:
        offs.append(offs[-1] + sz)
    (o_fq, o_fk, o_fv, o_ff, o_hq, o_hf, o_hi, o_hg, o_mq, o_gt, o_end) = offs

    lb_all = jnp.cumsum(jax.nn.softmax(hgrn_lb_logits.astype(F32), axis=0), axis=0)
    mem_n = rmsnorm_rows(mem.reshape(batch * n_mem, d), mem_norm_g, BF16)

    h = x.reshape(m, d)
    for layer in range(depth):
        w_t = jnp.swapaxes(w_in[layer], 0, 1).astype(BF16)
        seg = lambda a, b: ((a, b - a),)
        hn = rmsnorm_rows(h, norm1_g[layer], BF16)

        qk_gain = jnp.concatenate([jnp.tile(fox_q_g[layer] * (LOG2E * fox_hd ** -0.5), fox_heads),
                                   jnp.tile(fox_k_g[layer], fox_heads)])
        qk = matmul(hn, w_t, BF16, functools.partial(_epi_headnorm, head_dim=fox_hd),
                    row_vec=qk_gain, w_rows=seg(o_fq, o_fv), name="proj_fox_qk")
        v_t = matmul_t(hn, w_t, BF16, tm=FOX_BLOCK, w_rows=seg(o_fv, o_ff), name="proj_fox_vt")
        plain = matmul(hn, w_t, BF16, w_rows=seg(o_hq, o_hf) + seg(o_hi, o_mq), name="proj_plain")
        hd_blk = lambda width: width // HGRN_EXPAND
        c_hq = 0
        c_hi = c_hq + hd_blk(hgrn_w)
        c_hg = c_hi + hd_blk(hgrn_w)
        z = matmul(hn, w_t, F32, w_rows=seg(o_hf, o_hi), name="proj_hgrn_f")
        ff = matmul(hn, w_t, F32, tn=LANES, w_rows=seg(o_ff, o_ff + LANES), name="proj_fox_f")
        mq = matmul(hn, w_t, BF16, functools.partial(_epi_headnorm, head_dim=mem_hd),
                    row_vec=jnp.tile(mem_q_g[layer] * mem_hd ** -0.5, mem_heads), tn=mem_hd,
                    w_rows=seg(o_mq, o_gt), name="proj_mem_q")
        gates = matmul(hn, w_t, BF16, _epi_sigmoid_bias, row_vec=b_gate[layer],
                       w_rows=seg(o_gt, o_end), name="proj_gates")

        kbias = fox_key_bias(ff, fox_f_bias[layer], batch, fox_heads, fox_hd)
        o_a = fox_attention(qk, kbias, v_t, batch, fox_heads, fox_hd)

        o_b = hgrn2(plain, c_hq, c_hi, c_hg, z, lb_all[layer], hgrn_out_g[layer], batch, hgrn_heads)

        wkv = w_mem_kv[layer]
        mk = matmul(mem_n, wkv[:, :mem_w].astype(BF16), BF16,
                    functools.partial(_epi_headnorm, head_dim=mem_hd),
                    row_vec=jnp.tile(mem_k_g[layer], mem_heads), tn=mem_hd, name="proj_mem_k")
        mv = matmul(mem_n, wkv[:, mem_w:].astype(BF16), BF16, name="proj_mem_v")
        o_c = mem_attention(mq, mk, mv, batch, mem_heads, mem_hd)

        merged = gated_merge(o_a, o_b, o_c, w_branch_fox[layer].astype(BF16),
                             w_branch_hgrn[layer].astype(BF16), w_branch_mem[layer].astype(BF16), gates)
        h = matmul(merged, w_out[layer].astype(BF16), F32, _epi_residual, residual=h, tn=512,
                   name="proj_out")

        expert_id, weights = moe_router(h, norm2_g[layer], w_group[layer], b_group[layer],
                                        w_router[layer], b_router[layer])
        tok_buf, sched, dest = _moe_dispatch(expert_id, n_experts)
        y_buf = moe_experts(h, norm2_g[layer], tok_buf, sched, w_exp_gate[layer].astype(F32),
                            w_exp_up[layer].astype(F32), w_exp_down[layer].astype(F32))
        h = moe_combine(h, y_buf, dest, weights)
    return h.reshape(batch, seq, d)
```

```python
import functools

import jax
import jax.numpy as jnp
from jax import lax
from jax.experimental import pallas as pl
from jax.experimental.pallas import tpu as pltpu

F32 = jnp.float32
BF16 = jnp.bfloat16

EPS = 1e-6
CHUNK = 64
SUB = 16
HGRN_EXPAND = 128
TOP_K = 2
MOE_BLOCK = 128
FOX_BLOCK = 512
FOX_GROUP = 4
LANES = 128
BF16_ROWS = 16
LOG2E = 1.4426950408889634
V7X_VMEM_LIMIT_CAP = 60000 * 1024


def _vmem_limit(block_bytes, temp_bytes=0):
    need = int(1.25 * (2 * sum(block_bytes) + temp_bytes)) + (2 << 20)
    return min(max(need, 16 << 20), V7X_VMEM_LIMIT_CAP)


def _nbytes(shape, dtype):
    n = 1
    for s in shape:
        n *= s
    return n * jnp.dtype(dtype).itemsize


def _tile(dim, want):
    if dim <= want:
        return dim
    for t in range(want - want % LANES, 0, -LANES):
        if dim % t == 0:
            return t
    raise ValueError((dim, want))


def _rmsnorm_kernel(x_ref, g_ref, o_ref):
    x = x_ref[...].astype(F32)
    y = x * lax.rsqrt(jnp.mean(x * x, axis=-1, keepdims=True) + EPS)
    o_ref[...] = (y * g_ref[...]).astype(o_ref.dtype)


def rmsnorm_rows(x, g, out_dtype, tm=256):
    m, d = x.shape
    tm = _tile(m, tm)
    return pl.pallas_call(
        _rmsnorm_kernel,
        grid=(m // tm,),
        in_specs=[pl.BlockSpec((tm, d), lambda i: (i, 0)),
                  pl.BlockSpec((1, d), lambda i: (0, 0))],
        out_specs=pl.BlockSpec((tm, d), lambda i: (i, 0)),
        out_shape=jax.ShapeDtypeStruct((m, d), out_dtype),
        compiler_params=pltpu.CompilerParams(
            dimension_semantics=("parallel",),
            vmem_limit_bytes=_vmem_limit([_nbytes((tm, d), x.dtype), _nbytes((tm, d), out_dtype)],
                                         2 * _nbytes((tm, d), F32))),
        name="rmsnorm_rows",
    )(x, g.reshape(1, d).astype(F32))


def _mm_kernel(a_ref, w_ref, *refs, epilogue, w_is_nk):
    o_ref = refs[-1]
    contract = (((1,), (1 if w_is_nk else 0,)), ((), ()))
    acc = lax.dot_general(a_ref[...], w_ref[...], contract, preferred_element_type=F32)
    epilogue(acc, o_ref, *refs[:-1])


def _epi_plain(acc, o_ref):
    o_ref[...] = acc.astype(o_ref.dtype)


def _epi_headnorm(acc, o_ref, g_ref, *, head_dim):
    for h in range(acc.shape[1] // head_dim):
        sl = slice(h * head_dim, (h + 1) * head_dim)
        blk = acc[:, sl]
        y = blk * lax.rsqrt(jnp.mean(blk * blk, axis=-1, keepdims=True) + EPS)
        o_ref[:, sl] = (y * g_ref[:, sl]).astype(o_ref.dtype)


def _epi_sigmoid_bias(acc, o_ref, b_ref):
    o_ref[...] = jax.nn.sigmoid(acc + b_ref[...]).astype(o_ref.dtype)


def _epi_residual(acc, o_ref, r_ref):
    o_ref[...] = (r_ref[...] + acc).astype(o_ref.dtype)


def _w_rows_spec(w_rows, tn, k):
    starts, first = [], 0
    for row0, rows in w_rows:
        assert rows % tn == 0
        starts.append((first, row0))
        first += rows // tn

    def index_map(i, j):
        row = starts[0][1] + j * tn
        for first_tile, row0 in starts[1:]:
            row = jnp.where(j >= first_tile, row0 + (j - first_tile) * tn, row)
        return pl.multiple_of(row, BF16_ROWS), 0

    assert all(row0 % BF16_ROWS == 0 for row0, _ in w_rows)
    return pl.BlockSpec((pl.Element(tn), pl.Element(k)), index_map)


def matmul(a, w, out_dtype, epilogue=_epi_plain, row_vec=None, residual=None, tm=1024, tn=1024,
           w_is_nk=False, w_rows=None, name="matmul"):
    m, k = a.shape
    if w_rows is not None:
        w_is_nk = True
        n = sum(rows for _, rows in w_rows)
        tn = min(tn, min(rows for _, rows in w_rows))
    else:
        n = w.shape[0 if w_is_nk else 1]
    tm, tn = _tile(m, tm), _tile(n, tn)
    in_specs = [pl.BlockSpec((tm, k), lambda i, j: (i, 0)),
                _w_rows_spec(w_rows, tn, k) if w_rows is not None
                else pl.BlockSpec((tn, k), lambda i, j: (j, 0)) if w_is_nk
                else pl.BlockSpec((k, tn), lambda i, j: (0, j))]
    operands = [a, w]
    blocks = [_nbytes((tm, k), a.dtype), _nbytes((k, tn), w.dtype), _nbytes((tm, tn), out_dtype)]
    if row_vec is not None:
        in_specs.append(pl.BlockSpec((1, tn), lambda i, j: (0, j)))
        operands.append(row_vec.reshape(1, n).astype(F32))
    if residual is not None:
        in_specs.append(pl.BlockSpec((tm, tn), lambda i, j: (i, j)))
        operands.append(residual)
        blocks.append(_nbytes((tm, tn), residual.dtype))
    return pl.pallas_call(
        functools.partial(_mm_kernel, epilogue=epilogue, w_is_nk=w_is_nk),
        grid=(m // tm, n // tn),
        in_specs=in_specs,
        out_specs=pl.BlockSpec((tm, tn), lambda i, j: (i, j)),
        out_shape=jax.ShapeDtypeStruct((m, n), out_dtype),
        compiler_params=pltpu.CompilerParams(
            dimension_semantics=("parallel", "parallel"),
            vmem_limit_bytes=_vmem_limit(blocks, 2 * _nbytes((tm, tn), F32))),
        name=name,
    )(*operands)


def _mm_t_kernel(wt_ref, a_ref, o_ref):
    o_ref[0] = lax.dot_general(wt_ref[...], a_ref[...], (((1,), (1,)), ((), ())),
                               preferred_element_type=F32).astype(o_ref.dtype)


def matmul_t(a, w_t, out_dtype, tm=512, tn=1024, w_rows=None, name="matmul_t"):
    m, k = a.shape
    n = w_t.shape[0] if w_rows is None else sum(rows for _, rows in w_rows)
    tm, tn = _tile(m, tm), _tile(n, tn)
    return pl.pallas_call(
        _mm_t_kernel,
        grid=(m // tm, n // tn),
        in_specs=[pl.BlockSpec((tn, k), lambda i, j: (j, 0)) if w_rows is None else _w_rows_spec(w_rows, tn, k),
                  pl.BlockSpec((tm, k), lambda i, j: (i, 0))],
        out_specs=pl.BlockSpec((1, tn, tm), lambda i, j: (i, j, 0)),
        out_shape=jax.ShapeDtypeStruct((m // tm, n, tm), out_dtype),
        compiler_params=pltpu.CompilerParams(
            dimension_semantics=("parallel", "parallel"),
            vmem_limit_bytes=_vmem_limit([_nbytes((tm, k), a.dtype), _nbytes((tn, k), w_t.dtype),
                                          _nbytes((tn, tm), out_dtype)], 2 * _nbytes((tn, tm), F32))),
        name=name,
    )(w_t, a)


N_SPLIT = 3


def _fox_bias_kernel(f_ref, b_ref, e_ref, o_ref, carry):
    @pl.when(pl.program_id(1) == 0)
    def _():
        carry[...] = jnp.zeros(carry.shape, F32)

    x = f_ref[...] + b_ref[...]
    a = jnp.minimum(x, 0.0) - jnp.log(1.0 + jnp.exp(-jnp.abs(x)))
    r = a.shape[0]
    row = lax.broadcasted_iota(jnp.int32, a.shape, 0)
    sh = 1
    while sh < r:
        a = a + jnp.where(row >= sh, pltpu.roll(a, sh, axis=0), 0.0)
        sh *= 2
    c = a + carry[...]
    carry[...] = c[r - 1:r, :]
    rest = c * (-LOG2E)
    pieces = []
    for _ in range(N_SPLIT):
        piece = rest.astype(BF16)
        pieces.append(piece)
        rest = rest - piece.astype(F32)
    o_ref[...] = jnp.dot(jnp.concatenate(pieces, axis=1), e_ref[...],
                         preferred_element_type=F32).astype(o_ref.dtype)


def fox_key_bias(ff, bias, batch, heads, head_dim, rows=1024):
    m = ff.shape[0]
    s = m // batch
    rows = _tile(s, rows)
    nr = s // rows
    src = jnp.arange(N_SPLIT * LANES)
    dst = (src % LANES) * head_dim + src // LANES
    place = ((jnp.arange(heads * head_dim)[None, :] == dst[:, None])
             & ((src % LANES) < heads)[:, None]).astype(BF16)
    bias_row = jnp.zeros((1, LANES), F32).at[0, :heads].set(bias.astype(F32))
    return pl.pallas_call(
        _fox_bias_kernel,
        grid=(batch, nr),
        in_specs=[pl.BlockSpec((rows, LANES), lambda b, t: (b * nr + t, 0)),
                  pl.BlockSpec((1, LANES), lambda b, t: (0, 0)),
                  pl.BlockSpec((N_SPLIT * LANES, heads * head_dim), lambda b, t: (0, 0))],
        out_specs=pl.BlockSpec((rows, heads * head_dim), lambda b, t: (b * nr + t, 0)),
        out_shape=jax.ShapeDtypeStruct((m, heads * head_dim), BF16),
        scratch_shapes=[pltpu.VMEM((1, LANES), F32)],
        compiler_params=pltpu.CompilerParams(
            dimension_semantics=("parallel", "arbitrary"),
            vmem_limit_bytes=_vmem_limit([_nbytes((rows, heads * head_dim), BF16),
                                          _nbytes((N_SPLIT * LANES, heads * head_dim), BF16)],
                                         _nbytes((rows, heads * head_dim), F32) + 16 * _nbytes((rows, LANES), F32))),
        name="fox_key_bias",
    )(ff, bias_row, place)


SUM_ROWS = 16


KIND_PLAIN, KIND_DIAG, KIND_PAD = 0, 1, 2


def _fox_attn_kernel(row_ref, col_ref, kind_ref, start_ref, q_ref, k_ref, kb_ref, vt_ref, o_ref,
                     qt_scr, mask_scr, s_scr, m_scr, acc_scr, snap_scr, *, n_steps):
    nq, hd2, t = qt_scr.shape
    hd = hd2 // 2

    causal = (lax.broadcasted_iota(jnp.int32, (t, t), 0) <= lax.broadcasted_iota(jnp.int32, (t, t), 1))
    mask_scr[KIND_PLAIN] = jnp.zeros((t, t), F32)
    mask_scr[KIND_DIAG] = jnp.where(causal, 0.0, -jnp.inf)
    mask_scr[KIND_PAD] = jnp.full((t, t), -jnp.inf, F32)

    for r in range(nq):
        qt_scr[r, :hd, :] = q_ref[r * t:(r + 1) * t, :].astype(F32).T.astype(BF16)
        qt_scr[r, hd:, :] = jnp.where(lax.broadcasted_iota(jnp.int32, (hd, t), 0) < N_SPLIT,
                                      1.0, 0.0).astype(BF16)
    m_scr[...] = jnp.zeros(m_scr.shape, F32)
    acc_scr[...] = jnp.zeros(acc_scr.shape, F32)

    def scores(p):
        off = pl.multiple_of(col_ref[p] * t, t)
        k_aug = jnp.concatenate([k_ref[pl.ds(off, t), :], kb_ref[pl.ds(off, t), :]], axis=1)
        return jnp.dot(k_aug, qt_scr[row_ref[p]], preferred_element_type=F32)

    def update(s, p):
        s = s + mask_scr[kind_ref[p]]
        m_prev = jnp.where(start_ref[p] == 1, -jnp.inf, m_scr[...])
        m_new = jnp.maximum(m_prev, jnp.max(s, axis=0, keepdims=True))
        alpha = jnp.exp2(m_prev - m_new)
        prob = jnp.exp2((s - m_new).astype(BF16))
        vt_aug = jnp.concatenate([vt_ref[0, col_ref[p]], jnp.ones((SUM_ROWS, t), BF16)], axis=0)
        acc = alpha * acc_scr[...] + jnp.dot(vt_aug, prob, preferred_element_type=F32)
        m_scr[...] = m_new
        acc_scr[...] = acc
        return acc

    def write_row(p, acc_ref):
        @pl.when(kind_ref[p] == KIND_DIAG)
        def _():
            out = acc_ref[:hd, :] * (1.0 / acc_ref[hd:hd + 1, :])
            o_ref[pl.ds(pl.multiple_of(row_ref[p] * t, t), t), :] = out.T.astype(o_ref.dtype)

    group = s_scr.shape[0]
    s_scr[0] = scores(0)

    def pair_group(u, carry):
        p = group * u
        for k in range(group):
            s_scr[(k + 1) % group] = scores(p + k + 1)
            acc = update(s_scr[k], p + k)
            if k + 1 < group:
                snap_scr[k] = acc
        for k in range(group):
            write_row(p + k, snap_scr.at[k] if k + 1 < group else acc_scr)
        return carry

    lax.fori_loop(0, n_steps, pair_group, 0)


def fox_attention(qk, kbias, v_t, batch, heads, head_dim):
    m = qk.shape[0]
    s = m // batch
    t = v_t.shape[2]
    nq = s // t
    pairs = [(i, j, KIND_DIAG if i == j else KIND_PLAIN, int(j == 0)) for i in range(nq) for j in range(i + 1)]
    n_steps = -(-len(pairs) // FOX_GROUP)
    pairs += [(nq - 1, nq - 1, KIND_PAD, 0)] * (FOX_GROUP * n_steps + 1 - len(pairs))
    tables = [jnp.asarray(v, jnp.int32) for v in zip(*pairs)]
    grid_spec = pltpu.PrefetchScalarGridSpec(
        num_scalar_prefetch=len(tables),
        grid=(batch, heads),
        in_specs=[pl.BlockSpec((s, head_dim), lambda b, h, *_: (b, h)),
                  pl.BlockSpec((s, head_dim), lambda b, h, *_: (b, heads + h)),
                  pl.BlockSpec((s, head_dim), lambda b, h, *_: (b, h)),
                  pl.BlockSpec((1, nq, head_dim, t), lambda b, h, *_: (b, 0, h, 0))],
        out_specs=pl.BlockSpec((s, head_dim), lambda b, h, *_: (b, h)),
        scratch_shapes=[pltpu.VMEM((nq, 2 * head_dim, t), BF16), pltpu.VMEM((3, t, t), F32),
                        pltpu.VMEM((FOX_GROUP, t, t), F32),
                        pltpu.VMEM((1, t), F32), pltpu.VMEM((head_dim + SUM_ROWS, t), F32),
                        pltpu.VMEM((FOX_GROUP - 1, head_dim + SUM_ROWS, t), F32)],
    )
    return pl.pallas_call(
        functools.partial(_fox_attn_kernel, n_steps=n_steps),
        grid_spec=grid_spec,
        out_shape=jax.ShapeDtypeStruct((m, heads * head_dim), BF16),
        compiler_params=pltpu.CompilerParams(
            dimension_semantics=("parallel", "parallel"),
            vmem_limit_bytes=_vmem_limit([5 * _nbytes((s, head_dim), BF16)],
                                         2 * _nbytes((s, head_dim), BF16) + 10 * _nbytes((t, t), F32))),
        name="fox_attention",
    )(*tables, qk, qk, kbias, v_t.reshape(batch, nq, heads * head_dim, t))


def _hgrn_kernel(q_ref, z_ref, v_ref, hg_ref, lb_ref, og_ref, o_ref, st_ref):
    @pl.when(pl.program_id(2) == 0)
    def _():
        st_ref[...] = jnp.zeros(st_ref.shape, F32)

    t, dk = z_ref.shape
    dv = v_ref.shape[1]
    nc, nb, nsub = t // CHUNK, t // SUB, CHUNK // SUB

    q = q_ref[...].astype(F32)
    z = z_ref[...]
    lb = lb_ref[...]
    g = jnp.log(lb + (1.0 - lb) * jax.nn.sigmoid(z))
    kk = (1.0 - lb) * jax.nn.sigmoid(-z)

    row = lax.broadcasted_iota(jnp.int32, (t, dk), 0)
    rin = row % CHUNK
    b = g
    sh = 1
    while sh < CHUNK:
        b = b + jnp.where(rin >= sh, pltpu.roll(b, sh, axis=0), 0.0)
        sh *= 2

    b3 = b.reshape(nc, CHUNK, dk)
    q3 = q.reshape(nc, CHUNK, dk)
    k3 = kk.reshape(nc, CHUNK, dk)
    v3 = v_ref[...].reshape(nc, CHUNK, dv)
    b_last = b3[:, CHUNK - 1:CHUNK, :]
    q_in = (q3 * jnp.exp(b3)).astype(BF16)
    k_out = (k3 * jnp.exp(b_last - b3)).astype(BF16)
    d_last = jnp.exp(b_last)

    st = st_ref[...]
    inter = []
    for c in range(nc):
        inter.append(lax.dot_general(q_in[c], st.astype(BF16), (((1,), (1,)), ((), ())),
                                     preferred_element_type=F32))
        kv_t = lax.dot_general(v3[c], k_out[c], (((0,), (0,)), ((), ())),
                               preferred_element_type=F32)
        st = st * d_last[c] + kv_t
    st_ref[...] = st
    inter = jnp.concatenate(inter, axis=0)

    q_parts, k_parts = [], []
    for blk in range(1, nsub):
        lo, hi = blk * SUB, (blk + 1) * SUB
        b_ref_pt = b3[:, lo - 1:lo, :]
        q_hat = (q3[:, lo:hi, :] * jnp.exp(b3[:, lo:hi, :] - b_ref_pt)).astype(BF16)
        k_hat = (k3[:, :lo, :] * jnp.exp(b_ref_pt - b3[:, :lo, :])).astype(BF16)
        q_rows = [jnp.zeros((nc, lo, dk), BF16), q_hat] + ([jnp.zeros((nc, CHUNK - hi, dk), BF16)] if hi < CHUNK else [])
        q_parts.append(jnp.concatenate(q_rows, axis=1))
        k_parts.append(jnp.concatenate([k_hat, jnp.zeros((nc, CHUNK - lo, dk), BF16)], axis=1))
    q_cat = jnp.concatenate(q_parts, axis=2)
    k_cat = jnp.concatenate(k_parts, axis=2)
    s_off = lax.dot_general(q_cat, k_cat, (((2,), (2,)), ((0,), (0,))),
                            preferred_element_type=F32)

    b4 = b.reshape(nb, SUB, dk)
    q4 = q.reshape(nb, SUB, dk)
    k4 = kk.reshape(nb, SUB, dk)
    rsub = (row % SUB).reshape(nb, SUB, dk)
    u_row = lax.broadcasted_iota(jnp.int32, (dk, LANES), 1)
    half = SUB // 2
    s_lo = jnp.zeros((t, LANES), F32)
    s_hi = jnp.zeros((t // 2, LANES), F32)
    for j in range(SUB):
        r0 = 0 if j < half else half
        bj, kj = b4[:, j:j + 1, :], k4[:, j:j + 1, :]
        decay = jnp.exp(jnp.where(rsub[:, r0:, :] >= j, b4[:, r0:, :] - bj, -jnp.inf))
        p = (q4[:, r0:, :] * (kj * decay)).reshape(-1, dk).astype(BF16)
        place = jnp.where((u_row % SUB == j) & (u_row < CHUNK), 1.0, 0.0).astype(BF16)
        if j < half:
            s_lo = s_lo + jnp.dot(p, place, preferred_element_type=F32)
        else:
            s_hi = s_hi + jnp.dot(p, place, preferred_element_type=F32)
    s_diag = s_lo.reshape(nb, SUB, LANES) + jnp.concatenate(
        [jnp.zeros((nb, half, LANES), F32), s_hi.reshape(nb, half, LANES)], axis=1)
    s_diag = s_diag.reshape(nc, CHUNK, LANES)[:, :, :CHUNK]

    rb = lax.broadcasted_iota(jnp.int32, (nc, CHUNK, CHUNK), 1) // SUB
    cb = lax.broadcasted_iota(jnp.int32, (nc, CHUNK, CHUNK), 2) // SUB
    scores = jnp.where(cb == rb, s_diag, jnp.where(cb < rb, s_off, 0.0)).astype(BF16)
    intra = lax.dot_general(scores, v3, (((2,), (1,)), ((0,), (0,))),
                            preferred_element_type=F32).reshape(t, dv)

    o = inter + intra
    o = o * lax.rsqrt(jnp.mean(o * o, axis=-1, keepdims=True) + EPS) * og_ref[...]
    hg = hg_ref[...].astype(F32)
    o_ref[...] = (o * (hg * jax.nn.sigmoid(hg))).astype(o_ref.dtype)


def hgrn2(src, q_col0, v_col0, hg_col0, z, lb, out_g, batch, heads, rows=1024):
    m = z.shape[0]
    s = m // batch
    dk = HGRN_EXPAND
    dv = out_g.shape[-1]
    rows = _tile(s, rows)
    nt = s // rows
    row_blk = lambda b, h, t: b * nt + t
    return pl.pallas_call(
        _hgrn_kernel,
        grid=(batch, heads, nt),
        in_specs=[pl.BlockSpec((rows, dk), lambda b, h, t: (row_blk(b, h, t), q_col0 + h)),
                  pl.BlockSpec((rows, dk), lambda b, h, t: (row_blk(b, h, t), h)),
                  pl.BlockSpec((rows, dv), lambda b, h, t: (row_blk(b, h, t), v_col0 + h)),
                  pl.BlockSpec((rows, dv), lambda b, h, t: (row_blk(b, h, t), hg_col0 + h)),
                  pl.BlockSpec((1, dk), lambda b, h, t: (0, h)),
                  pl.BlockSpec((1, dv), lambda b, h, t: (0, 0))],
        out_specs=pl.BlockSpec((rows, dv), lambda b, h, t: (row_blk(b, h, t), h)),
        out_shape=jax.ShapeDtypeStruct((m, heads * dv), BF16),
        scratch_shapes=[pltpu.VMEM((dv, dk), F32)],
        compiler_params=pltpu.CompilerParams(
            dimension_semantics=("parallel", "parallel", "arbitrary"),
            vmem_limit_bytes=_vmem_limit([6 * _nbytes((rows, dk), F32)], 40 * _nbytes((rows, dk), F32))),
        name="hgrn2",
    )(src, z, src, src, lb.reshape(1, -1).astype(F32), out_g.reshape(1, dv).astype(F32))


def _mem_attn_kernel(q_ref, k_ref, v_ref, o_ref):
    s = lax.dot_general(q_ref[...], k_ref[...], (((1,), (1,)), ((), ())), preferred_element_type=F32)
    m = jnp.max(s, axis=1, keepdims=True)
    p = jnp.exp(s - m)
    l = jnp.sum(p, axis=1, keepdims=True)
    o = jnp.dot(p.astype(BF16), v_ref[...], preferred_element_type=F32)
    o_ref[...] = (o / l).astype(o_ref.dtype)


def mem_attention(q, mk, mv, batch, heads, head_dim, tq=1024):
    m = q.shape[0]
    s = m // batch
    n_mem = mk.shape[0] // batch
    tq = _tile(s, tq)
    nq = s // tq
    return pl.pallas_call(
        _mem_attn_kernel,
        grid=(batch, heads, nq),
        in_specs=[pl.BlockSpec((tq, head_dim), lambda b, h, i: (b * nq + i, h)),
                  pl.BlockSpec((n_mem, head_dim), lambda b, h, i: (b, h)),
                  pl.BlockSpec((n_mem, head_dim), lambda b, h, i: (b, h))],
        out_specs=pl.BlockSpec((tq, head_dim), lambda b, h, i: (b * nq + i, h)),
        out_shape=jax.ShapeDtypeStruct((m, heads * head_dim), BF16),
        compiler_params=pltpu.CompilerParams(
            dimension_semantics=("parallel", "parallel", "parallel"),
            vmem_limit_bytes=_vmem_limit([2 * _nbytes((tq, head_dim), BF16), 2 * _nbytes((n_mem, head_dim), BF16)],
                                         4 * _nbytes((tq, head_dim), F32))),
        name="mem_attention",
    )(q, mk, mv)


def _merge_kernel(a_ref, b_ref, c_ref, wa_ref, wb_ref, wc_ref, ga_ref, gb_ref, gc_ref, o_ref):
    acc = ga_ref[...].astype(F32) * jnp.dot(a_ref[...], wa_ref[...], preferred_element_type=F32)
    acc = acc + gb_ref[...].astype(F32) * jnp.dot(b_ref[...], wb_ref[...], preferred_element_type=F32)
    acc = acc + gc_ref[...].astype(F32) * jnp.dot(c_ref[...], wc_ref[...], preferred_element_type=F32)
    o_ref[...] = acc.astype(o_ref.dtype)


def gated_merge(o_a, o_b, o_c, w_a, w_b, w_c, gates, tm=1024, tn=512):
    m = o_a.shape[0]
    d = w_a.shape[1]
    tm, tn = _tile(m, tm), _tile(d, tn)
    nj = d // tn
    a_spec = lambda arr: pl.BlockSpec((tm, arr.shape[1]), lambda i, j: (i, 0))
    w_spec = lambda arr: pl.BlockSpec((arr.shape[0], tn), lambda i, j: (0, j))
    g_spec = lambda r: pl.BlockSpec((tm, tn), lambda i, j: (i, r * nj + j))
    blocks = ([_nbytes((tm, a.shape[1]), BF16) for a in (o_a, o_b, o_c)]
              + [_nbytes((w.shape[0], tn), BF16) for w in (w_a, w_b, w_c)] + 4 * [_nbytes((tm, tn), BF16)])
    return pl.pallas_call(
        _merge_kernel,
        grid=(m // tm, nj),
        in_specs=[a_spec(o_a), a_spec(o_b), a_spec(o_c), w_spec(w_a), w_spec(w_b), w_spec(w_c),
                  g_spec(0), g_spec(1), g_spec(2)],
        out_specs=pl.BlockSpec((tm, tn), lambda i, j: (i, j)),
        out_shape=jax.ShapeDtypeStruct((m, d), BF16),
        compiler_params=pltpu.CompilerParams(
            dimension_semantics=("parallel", "parallel"),
            vmem_limit_bytes=_vmem_limit(blocks, 3 * _nbytes((tm, tn), F32))),
        name="gated_merge",
    )(o_a, o_b, o_c, w_a, w_b, w_c, gates, gates, gates)


def _router_kernel(h_ref, g_ref, whi_ref, wlo_ref, b_ref, lg_ref, id_ref, wt_ref, *, n_groups):
    x = h_ref[...]
    hn = x * lax.rsqrt(jnp.mean(x * x, axis=-1, keepdims=True) + EPS) * g_ref[...]
    hn_hi = hn.astype(BF16)
    hn_lo = (hn - hn_hi.astype(F32)).astype(BF16)
    logits = (jnp.dot(hn_hi, whi_ref[...], preferred_element_type=F32)
              + (jnp.dot(hn_hi, wlo_ref[...], preferred_element_type=F32)
                 + jnp.dot(hn_lo, whi_ref[...], preferred_element_type=F32))) + b_ref[...]
    lane = lax.broadcasted_iota(jnp.int32, logits.shape, 1).astype(F32)
    lane_grp = lg_ref[...]
    neg = -jnp.inf
    big = float(LANES)

    gl = jnp.where(lane < n_groups, logits, neg)
    gmax = jnp.max(gl, axis=1, keepdims=True)
    grp = jnp.min(jnp.where(gl == gmax, lane, big), axis=1, keepdims=True)
    p_grp = 1.0 / jnp.sum(jnp.exp(gl - gmax), axis=1, keepdims=True)

    el = jnp.where(lane_grp == grp, logits, neg)
    v1 = jnp.max(el, axis=1, keepdims=True)
    i1 = jnp.min(jnp.where(el == v1, lane, big), axis=1, keepdims=True)
    el2 = jnp.where(lane == i1, neg, el)
    v2 = jnp.max(el2, axis=1, keepdims=True)
    i2 = jnp.min(jnp.where(el2 == v2, lane, big), axis=1, keepdims=True)
    e21 = jnp.exp(v2 - v1)
    w1 = p_grp / (1.0 + e21)
    w2 = p_grp * e21 / (1.0 + e21)

    id_ref[...] = jnp.where(lane == 0.0, i1 - n_groups, i2 - n_groups).astype(jnp.int32)
    wt_ref[...] = jnp.where(lane == 0.0, w1, w2)


def moe_router(h, norm_g, w_group, b_group, w_router, b_router, tm=256):
    n, d = h.shape
    n_groups = w_group.shape[1]
    n_experts = w_router.shape[1]
    per_group = n_experts // n_groups
    assert n_groups + n_experts <= LANES
    pad = LANES - n_groups - n_experts
    w = jnp.concatenate([w_group, w_router, jnp.zeros((d, pad), F32)], axis=1).astype(F32)
    bias = jnp.concatenate([b_group, b_router, jnp.zeros((pad,), F32)]).reshape(1, LANES).astype(F32)
    w_hi = w.astype(BF16)
    w_lo = (w - w_hi.astype(F32)).astype(BF16)
    lane = jnp.arange(LANES)
    lane_grp = jnp.where((lane >= n_groups) & (lane < n_groups + n_experts),
                         (lane - n_groups) // per_group, -1).astype(F32).reshape(1, LANES)
    tm = _tile(n, tm)
    ids, wts = pl.pallas_call(
        functools.partial(_router_kernel, n_groups=n_groups),
        grid=(n // tm,),
        in_specs=[pl.BlockSpec((tm, d), lambda i: (i, 0)),
                  pl.BlockSpec((1, d), lambda i: (0, 0)),
                  pl.BlockSpec((d, LANES), lambda i: (0, 0)),
                  pl.BlockSpec((d, LANES), lambda i: (0, 0)),
                  pl.BlockSpec((1, LANES), lambda i: (0, 0)),
                  pl.BlockSpec((1, LANES), lambda i: (0, 0))],
        out_specs=[pl.BlockSpec((tm, LANES), lambda i: (i, 0)),
                   pl.BlockSpec((tm, LANES), lambda i: (i, 0))],
        out_shape=[jax.ShapeDtypeStruct((n, LANES), jnp.int32),
                   jax.ShapeDtypeStruct((n, LANES), F32)],
        compiler_params=pltpu.CompilerParams(
            dimension_semantics=("parallel",),
            vmem_limit_bytes=_vmem_limit([_nbytes((tm, d), F32), _nbytes((d, LANES), F32)],
                                         8 * _nbytes((tm, d), F32))),
        name="moe_router",
    )(h, norm_g.reshape(1, d).astype(F32), w_hi, w_lo, bias, lane_grp)
    return ids[:, :TOP_K], wts


def _row_copy(src_hbm, row, dst, slot, sem):
    return pltpu.make_async_copy(src_hbm.at[pl.ds(row, 1)], dst.at[pl.ds(slot, 1)], sem)


GATHER_UNROLL = 8


def _gather_rows(src_hbm, ids_ref, n_ids, stride, offset, dst, sem, wait):
    def body(r, c):
        cp = _row_copy(src_hbm, ids_ref[0, 0, stride * r + offset], dst, r, sem)
        if wait:
            cp.wait()
        else:
            cp.start()
        return c

    lax.fori_loop(0, n_ids, body, 0, unroll=GATHER_UNROLL)


W_STAGE_ROWS = 512
W_STAGE_SLOTS = 8


class _ExpertWeights:
    def __init__(self, hbm, stage, sems, bufs):
        self.hbm, self.stage, self.sems, self.bufs = hbm, stage, sems, bufs
        d, de = hbm[0].shape[1], hbm[0].shape[2]
        self.de = de
        self.sr = stage.shape[1]
        self.slots = stage.shape[0]
        self.ratio = d // de
        self.rb = self.sr // self.ratio
        self.n_a = d // self.sr
        self.n_chunks = 2 * self.n_a + de // self.rb

    def _each_copy(self, idx, e, fn):
        slot = idx % self.slots
        sr, n_a = self.sr, self.n_a
        for mat in range(2):
            @pl.when((idx >= mat * n_a) & (idx < (mat + 1) * n_a))
            def _(mat=mat):
                r0 = pl.multiple_of((idx - mat * n_a) * sr, sr)
                fn(pltpu.make_async_copy(self.hbm[mat].at[e, pl.ds(r0, sr), :], self.stage.at[slot],
                                         self.sems.at[slot]))

        @pl.when(idx >= 2 * n_a)
        def _():
            r0 = pl.multiple_of((idx - 2 * n_a) * self.rb, self.rb)
            for k in range(self.ratio):
                fn(pltpu.make_async_copy(self.hbm[2].at[e, pl.ds(r0, self.rb), pl.ds(k * self.de, self.de)],
                                         self.stage.at[slot, pl.ds(k * self.rb, self.rb), :],
                                         self.sems.at[slot]))

    def start(self, idx, e):
        self._each_copy(idx, e, lambda cp: cp.start())

    def wait(self, idx, e):
        self._each_copy(idx, e, lambda cp: cp.wait())

    def convert(self, idx, dst):
        slot = idx % self.slots
        sr, n_a = self.sr, self.n_a
        for mat in range(2):
            @pl.when((idx >= mat * n_a) & (idx < (mat + 1) * n_a))
            def _(mat=mat):
                r0 = pl.multiple_of((idx - mat * n_a) * sr, sr)
                self.bufs[mat][dst, pl.ds(r0, sr), :] = self.stage[slot].astype(BF16)

        @pl.when(idx >= 2 * n_a)
        def _():
            r0 = pl.multiple_of((idx - 2 * n_a) * self.rb, self.rb)
            for k in range(self.ratio):
                self.bufs[2][dst, pl.ds(r0, self.rb), pl.ds(k * self.de, self.de)] = (
                    self.stage[slot, pl.ds(k * self.rb, self.rb), :].astype(BF16))

    def start_first(self, e):
        for idx in range(min(self.slots, self.n_chunks)):
            self.start(jnp.int32(idx), e)

    def stream(self, e, dst, first, count):
        def body(c, carry):
            idx = first + c
            self.wait(idx, e)
            self.convert(idx, dst)

            @pl.when(idx + self.slots < self.n_chunks)
            def _():
                self.start(idx + self.slots, e)
            return carry

        lax.fori_loop(0, count, body, 0)


def _moe_expert_kernel(be_ref, nused_ref, pos_ref, nexte_ref, par_ref, c0_ref, cn_ref, tok_ref, tok_next_ref,
                       h_hbm, g_ref, wg_hbm, wu_hbm, wd_hbm, y_ref, x_buf, sems, stage, wsems,
                       wg_buf, wu_buf, wd_buf):
    i = pl.program_id(0)
    n_used = nused_ref[0]
    rows = x_buf.shape[1]
    slot = i % 2
    cur = par_ref[i]
    next_e = nexte_ref[i]
    w = _ExpertWeights((wg_hbm, wu_hbm, wd_hbm), stage, wsems, (wg_buf, wu_buf, wd_buf))

    @pl.when((i == 0) & (n_used > 0))
    def _():
        _gather_rows(h_hbm, tok_ref, rows, 1, 0, x_buf.at[0], sems.at[0], wait=False)
        w.start_first(be_ref[0])
        w.stream(be_ref[0], cur, 0, w.n_chunks)

    @pl.when(i + 1 < n_used)
    def _():
        _gather_rows(h_hbm, tok_next_ref, rows, 1, 0, x_buf.at[1 - slot], sems.at[1 - slot], wait=False)

    @pl.when((i < n_used) & (next_e >= 0))
    def _():
        @pl.when(pos_ref[i] == 0)
        def _():
            w.start_first(next_e)

        w.stream(next_e, 1 - cur, c0_ref[i], cn_ref[i])

    @pl.when(i < n_used)
    def _():
        _gather_rows(h_hbm, tok_ref, rows, 1, 0, x_buf.at[slot], sems.at[slot], wait=True)
        x = x_buf[slot]
        x = (x * lax.rsqrt(jnp.mean(x * x, axis=-1, keepdims=True) + EPS) * g_ref[...]).astype(BF16)
        a = jnp.dot(x, wg_buf[cur], preferred_element_type=F32)
        u = jnp.dot(x, wu_buf[cur], preferred_element_type=F32)
        hmid = (a * jax.nn.sigmoid(a) * u).astype(BF16)
        y_ref[...] = jnp.dot(hmid, wd_buf[cur], preferred_element_type=F32)

    @pl.when(i >= n_used)
    def _():
        y_ref[...] = jnp.zeros(y_ref.shape, y_ref.dtype)


def moe_experts(h, norm_g, tok_buf, sched, w_gate, w_up, w_down):
    n, d = h.shape
    blk_e, n_used, run_pos, run_len, next_e, parity = sched
    n_blocks = blk_e.shape[0]
    de = w_gate.shape[2]
    sr = min(W_STAGE_ROWS, d)
    assert d % sr == 0 and d % de == 0 and sr % (d // de) == 0 and de % (sr // (d // de)) == 0
    n_chunks = 2 * (d // sr) + de // (sr // (d // de))
    spread = jnp.maximum(run_len - 1, 1)
    k = jnp.maximum(run_pos - 1, 0)
    base, extra = n_chunks // spread, n_chunks % spread
    single = run_len == 1
    chunk0 = jnp.where(single, 0, k * base + jnp.minimum(k, extra)).astype(jnp.int32)
    chunk_n = jnp.where(single, n_chunks,
                        jnp.where(run_pos == 0, 0, base + (k < extra))).astype(jnp.int32)
    sched = (blk_e, n_used, run_pos, next_e, parity, chunk0, chunk_n)
    tok_blocks = tok_buf.reshape(n_blocks, 1, MOE_BLOCK)
    n_pre = len(sched)
    grid_spec = pltpu.PrefetchScalarGridSpec(
        num_scalar_prefetch=n_pre,
        grid=(n_blocks,),
        in_specs=[pl.BlockSpec((1, 1, MOE_BLOCK), lambda i, *_: (i, 0, 0), memory_space=pltpu.SMEM),
                  pl.BlockSpec((1, 1, MOE_BLOCK), lambda i, *_: (jnp.minimum(i + 1, n_blocks - 1), 0, 0),
                               memory_space=pltpu.SMEM),
                  pl.BlockSpec(memory_space=pl.ANY),
                  pl.BlockSpec((1, d), lambda i, *_: (0, 0)),
                  pl.BlockSpec(memory_space=pl.ANY),
                  pl.BlockSpec(memory_space=pl.ANY),
                  pl.BlockSpec(memory_space=pl.ANY)],
        out_specs=pl.BlockSpec((MOE_BLOCK, d), lambda i, *_: (i, 0)),
        scratch_shapes=[pltpu.VMEM((2, MOE_BLOCK, d), F32), pltpu.SemaphoreType.DMA((2,)),
                        pltpu.VMEM((W_STAGE_SLOTS, sr, de), F32), pltpu.SemaphoreType.DMA((W_STAGE_SLOTS,)),
                        pltpu.VMEM((2, d, de), BF16), pltpu.VMEM((2, d, de), BF16),
                        pltpu.VMEM((2, de, d), BF16)],
    )
    scratch_bytes = (2 * _nbytes((MOE_BLOCK, d), F32) + W_STAGE_SLOTS * _nbytes((sr, de), F32)
                     + 6 * _nbytes((d, de), BF16))
    return pl.pallas_call(
        _moe_expert_kernel,
        grid_spec=grid_spec,
        out_shape=jax.ShapeDtypeStruct((n_blocks * MOE_BLOCK, d), F32),
        compiler_params=pltpu.CompilerParams(
            dimension_semantics=("arbitrary",),
            vmem_limit_bytes=_vmem_limit([_nbytes((MOE_BLOCK, d), F32)],
                                         scratch_bytes + 4 * _nbytes((MOE_BLOCK, d), F32))),
        name="moe_experts",
    )(*sched, tok_blocks, tok_blocks, h, norm_g.reshape(1, d).astype(F32), w_gate, w_up, w_down)


def _combine_kernel(dest_ref, dest_next_ref, h_ref, w_ref, y_hbm, o_ref, buf, sems):
    i = pl.program_id(0)
    rows = buf.shape[2]
    slot = i % 2

    def gather(ids_ref, s, wait):
        for k in range(TOP_K):
            _gather_rows(y_hbm, ids_ref, rows, TOP_K, k, buf.at[s, k], sems.at[s], wait)

    @pl.when(i == 0)
    def _():
        gather(dest_ref, 0, False)

    @pl.when(i + 1 < pl.num_programs(0))
    def _():
        gather(dest_next_ref, 1 - slot, False)

    gather(dest_ref, slot, True)
    w = w_ref[...]
    o_ref[...] = h_ref[...] + (buf[slot, 0] * w[:, 0:1] + buf[slot, 1] * w[:, 1:2])


def moe_combine(h, y_buf, dest, weights, tm=128):
    n, d = h.shape
    tm = _tile(n, tm)
    nt = n // tm
    dest_blocks = dest.reshape(nt, 1, TOP_K * tm)
    return pl.pallas_call(
        _combine_kernel,
        grid=(nt,),
        in_specs=[pl.BlockSpec((1, 1, TOP_K * tm), lambda i: (i, 0, 0), memory_space=pltpu.SMEM),
                  pl.BlockSpec((1, 1, TOP_K * tm), lambda i: (jnp.minimum(i + 1, nt - 1), 0, 0),
                               memory_space=pltpu.SMEM),
                  pl.BlockSpec((tm, d), lambda i: (i, 0)),
                  pl.BlockSpec((tm, LANES), lambda i: (i, 0)),
                  pl.BlockSpec(memory_space=pl.ANY)],
        out_specs=pl.BlockSpec((tm, d), lambda i: (i, 0)),
        out_shape=jax.ShapeDtypeStruct((n, d), F32),
        scratch_shapes=[pltpu.VMEM((2, TOP_K, tm, d), F32), pltpu.SemaphoreType.DMA((2,))],
        compiler_params=pltpu.CompilerParams(
            dimension_semantics=("arbitrary",),
            vmem_limit_bytes=_vmem_limit([2 * _nbytes((tm, d), F32)], 8 * _nbytes((tm, d), F32))),
        name="moe_combine",
    )(dest_blocks, dest_blocks, h, weights, y_buf)


def _moe_dispatch(expert_id, n_experts):
    n = expert_id.shape[0]
    nk = n * TOP_K
    experts = jnp.arange(n_experts, dtype=jnp.int32)

    def lookup(table, idx):
        return jnp.sum(jnp.where(idx[:, None] == experts[None, :], table[None, :], 0), axis=1)

    flat_e = expert_id.reshape(nk).astype(jnp.int32)
    order = jnp.argsort(flat_e).astype(jnp.int32)
    rank = jnp.argsort(order).astype(jnp.int32)
    counts = jnp.sum((flat_e[:, None] == experts[None, :]).astype(jnp.int32), axis=0)
    start = jnp.cumsum(counts) - counts
    padded = (counts + MOE_BLOCK - 1) // MOE_BLOCK * MOE_BLOCK
    pad_end = jnp.cumsum(padded)
    pad_start = pad_end - padded
    n_blocks = (nk + n_experts * (MOE_BLOCK - 1) + MOE_BLOCK - 1) // MOE_BLOCK
    blk = jnp.arange(n_blocks, dtype=jnp.int32)
    blk_e = jnp.minimum(jnp.sum(pad_end[None, :] <= (blk * MOE_BLOCK)[:, None], axis=1),
                        n_experts - 1).astype(jnp.int32)
    dest = (rank + lookup(pad_start - start, flat_e)).astype(jnp.int32)
    blk_pad_start, blk_start, blk_count = (lookup(tbl, blk_e) for tbl in (pad_start, start, counts))
    within = (blk * MOE_BLOCK - blk_pad_start)[:, None] + jnp.arange(MOE_BLOCK, dtype=jnp.int32)[None, :]
    src = jnp.clip(blk_start[:, None] + within, 0, nk - 1)
    tok_buf = jnp.where((within >= 0) & (within < blk_count[:, None]), order[src] // TOP_K, 0)
    tok_buf = tok_buf.reshape(n_blocks * MOE_BLOCK).astype(jnp.int32)
    n_used = (pad_end[-1] // MOE_BLOCK).astype(jnp.int32)
    used = blk < n_used
    is_first = used & (blk_e != jnp.concatenate([jnp.full((1,), -1, jnp.int32), blk_e[:-1]]))
    parity = (jnp.cumsum(is_first.astype(jnp.int32)) - 1) % 2
    run_pos = blk - blk_pad_start // MOE_BLOCK
    run_len = lookup(padded, blk_e) // MOE_BLOCK
    live = jnp.where(counts > 0, experts, n_experts)
    next_live = jnp.concatenate([lax.cummin(live, reverse=True)[1:], jnp.full((1,), n_experts, jnp.int32)])
    blk_next = lookup(next_live, blk_e)
    next_e = jnp.where(blk_next < n_experts, blk_next, -1)
    sched = (blk_e, n_used.reshape(1), run_pos.astype(jnp.int32), run_len.astype(jnp.int32),
             next_e.astype(jnp.int32), parity.astype(jnp.int32))
    return tok_buf, sched, dest


def kernel(x, mem, norm1_g, norm2_g, mem_norm_g, w_in, b_gate, fox_f_bias, fox_q_g, fox_k_g,
           hgrn_lb_logits, hgrn_out_g, mem_q_g, mem_k_g, w_mem_kv, w_branch_fox, w_branch_hgrn,
           w_branch_mem, w_out, w_group, b_group, w_router, b_router, w_exp_gate, w_exp_up,
           w_exp_down):
    batch, seq, d = x.shape
    n_mem = mem.shape[1]
    depth = w_in.shape[0]
    fox_heads, fox_hd = fox_f_bias.shape[1], fox_q_g.shape[1]
    fox_w = fox_heads * fox_hd
    hgrn_w = hgrn_lb_logits.shape[1]
    hgrn_heads = hgrn_w // HGRN_EXPAND
    hgrn_dv = hgrn_out_g.shape[1]
    mem_hd = mem_q_g.shape[1]
    mem_w = w_branch_mem.shape[1]
    mem_heads = mem_w // mem_hd
    n_experts = w_router.shape[2]
    m = batch * seq

    sizes = (fox_w, fox_w, fox_w, fox_heads, hgrn_w, hgrn_w, hgrn_w, hgrn_w, mem_w, 3 * d)
    offs = [0]
    for sz in sizes:
        offs.append(offs[-1] + sz)
    (o_fq, o_fk, o_fv, o_ff, o_hq, o_hf, o_hi, o_hg, o_mq, o_gt, o_end) = offs

    lb_all = jnp.cumsum(jax.nn.softmax(hgrn_lb_logits.astype(F32), axis=0), axis=0)
    mem_n = rmsnorm_rows(mem.reshape(batch * n_mem, d), mem_norm_g, BF16)

    h = x.reshape(m, d)
    for layer in range(depth):
        w_t = jnp.swapaxes(w_in[layer], 0, 1).astype(BF16)
        seg = lambda a, b: ((a, b - a),)
        hn = rmsnorm_rows(h, norm1_g[layer], BF16)

        qk_gain = jnp.concatenate([jnp.tile(fox_q_g[layer] * (LOG2E * fox_hd ** -0.5), fox_heads),
                                   jnp.tile(fox_k_g[layer], fox_heads)])
        qk = matmul(hn, w_t, BF16, functools.partial(_epi_headnorm, head_dim=fox_hd),
                    row_vec=qk_gain, w_rows=seg(o_fq, o_fv), name="proj_fox_qk")
        v_t = matmul_t(hn, w_t, BF16, tm=FOX_BLOCK, w_rows=seg(o_fv, o_ff), name="proj_fox_vt")
        plain = matmul(hn, w_t, BF16, w_rows=seg(o_hq, o_hf) + seg(o_hi, o_mq), name="proj_plain")
        hd_blk = lambda width: width // HGRN_EXPAND
        c_hq = 0
        c_hi = c_hq + hd_blk(hgrn_w)
        c_hg = c_hi + hd_blk(hgrn_w)
        z = matmul(hn, w_t, F32, w_rows=seg(o_hf, o_hi), name="proj_hgrn_f")
        ff = matmul(hn, w_t, F32, tn=LANES, w_rows=seg(o_ff, o_ff + LANES), name="proj_fox_f")
        mq = matmul(hn, w_t, BF16, functools.partial(_epi_headnorm, head_dim=mem_hd),
                    row_vec=jnp.tile(mem_q_g[layer] * mem_hd ** -0.5, mem_heads), tn=mem_hd,
                    w_rows=seg(o_mq, o_gt), name="proj_mem_q")
        gates = matmul(hn, w_t, BF16, _epi_sigmoid_bias, row_vec=b_gate[layer],
                       w_rows=seg(o_gt, o_end), name="proj_gates")

        kbias = fox_key_bias(ff, fox_f_bias[layer], batch, fox_heads, fox_hd)
        o_a = fox_attention(qk, kbias, v_t, batch, fox_heads, fox_hd)

        o_b = hgrn2(plain, c_hq, c_hi, c_hg, z, lb_all[layer], hgrn_out_g[layer], batch, hgrn_heads)

        wkv = w_mem_kv[layer]
        mk = matmul(mem_n, wkv[:, :mem_w].astype(BF16), BF16,
                    functools.partial(_epi_headnorm, head_dim=mem_hd),
                    row_vec=jnp.tile(mem_k_g[layer], mem_heads), tn=mem_hd, name="proj_mem_k")
        mv = matmul(mem_n, wkv[:, mem_w:].astype(BF16), BF16, name="proj_mem_v")
        o_c = mem_attention(mq, mk, mv, batch, mem_heads, mem_hd)

        merged = gated_merge(o_a, o_b, o_c, w_branch_fox[layer].astype(BF16),
                             w_branch_hgrn[layer].astype(BF16), w_branch_mem[layer].astype(BF16), gates)
        h = matmul(merged, w_out[layer].astype(BF16), F32, _epi_residual, residual=h, tn=512,
                   name="proj_out")

        expert_id, weights = moe_router(h, norm2_g[layer], w_group[layer], b_group[layer],
                                        w_router[layer], b_router[layer])
        tok_buf, sched, dest = _moe_dispatch(expert_id, n_experts)
        y_buf = moe_experts(h, norm2_g[layer], tok_buf, sched, w_exp_gate[layer].astype(F32),
                            w_exp_up[layer].astype(F32), w_exp_down[layer].astype(F32))
        h = moe_combine(h, y_buf, dest, weights)
    return h.reshape(batch, seq, d)
```

```python
import functools

import jax
import jax.numpy as jnp
from jax import lax
from jax.experimental import pallas as pl
from jax.experimental.pallas import tpu as pltpu

F32 = jnp.float32
BF16 = jnp.bfloat16

EPS = 1e-6
CHUNK = 64
SUB = 16
HGRN_EXPAND = 128
TOP_K = 2
MOE_BLOCK = 128
FOX_BLOCK = 512
FOX_GROUP = 4
LANES = 128
BF16_ROWS = 16
LOG2E = 1.4426950408889634
V7X_VMEM_LIMIT_CAP = 60000 * 1024


def _vmem_limit(block_bytes, temp_bytes=0):
    need = int(1.25 * (2 * sum(block_bytes) + temp_bytes)) + (2 << 20)
    return min(max(need, 16 << 20), V7X_VMEM_LIMIT_CAP)


def _nbytes(shape, dtype):
    n = 1
    for s in shape:
        n *= s
    return n * jnp.dtype(dtype).itemsize


def _tile(dim, want):
    if dim <= want:
        return dim
    for t in range(want - want % LANES, 0, -LANES):
        if dim % t == 0:
            return t
    raise ValueError((dim, want))


def _rmsnorm_kernel(x_ref, g_ref, o_ref):
    x = x_ref[...].astype(F32)
    y = x * lax.rsqrt(jnp.mean(x * x, axis=-1, keepdims=True) + EPS)
    o_ref[...] = (y * g_ref[...]).astype(o_ref.dtype)


def rmsnorm_rows(x, g, out_dtype, tm=256):
    m, d = x.shape
    tm = _tile(m, tm)
    return pl.pallas_call(
        _rmsnorm_kernel,
        grid=(m // tm,),
        in_specs=[pl.BlockSpec((tm, d), lambda i: (i, 0)),
                  pl.BlockSpec((1, d), lambda i: (0, 0))],
        out_specs=pl.BlockSpec((tm, d), lambda i: (i, 0)),
        out_shape=jax.ShapeDtypeStruct((m, d), out_dtype),
        compiler_params=pltpu.CompilerParams(
            dimension_semantics=("parallel",),
            vmem_limit_bytes=_vmem_limit([_nbytes((tm, d), x.dtype), _nbytes((tm, d), out_dtype)],
                                         2 * _nbytes((tm, d), F32))),
        name="rmsnorm_rows",
    )(x, g.reshape(1, d).astype(F32))


def _mm_kernel(a_ref, w_ref, *refs, epilogue, w_is_nk):
    o_ref = refs[-1]
    contract = (((1,), (1 if w_is_nk else 0,)), ((), ()))
    acc = lax.dot_general(a_ref[...], w_ref[...], contract, preferred_element_type=F32)
    epilogue(acc, o_ref, *refs[:-1])


def _epi_plain(acc, o_ref):
    o_ref[...] = acc.astype(o_ref.dtype)


def _epi_headnorm(acc, o_ref, g_ref, *, head_dim):
    for h in range(acc.shape[1] // head_dim):
        sl = slice(h * head_dim, (h + 1) * head_dim)
        blk = acc[:, sl]
        y = blk * lax.rsqrt(jnp.mean(blk * blk, axis=-1, keepdims=True) + EPS)
        o_ref[:, sl] = (y * g_ref[:, sl]).astype(o_ref.dtype)


def _epi_sigmoid_bias(acc, o_ref, b_ref):
    o_ref[...] = jax.nn.sigmoid(acc + b_ref[...]).astype(o_ref.dtype)


def _epi_residual(acc, o_ref, r_ref):
    o_ref[...] = (r_ref[...] + acc).astype(o_ref.dtype)


def _w_rows_spec(w_rows, tn, k):
    starts, first = [], 0
    for row0, rows in w_rows:
        assert rows % tn == 0
        starts.append((first, row0))
        first += rows // tn

    def index_map(i, j):
        row = starts[0][1] + j * tn
        for first_tile, row0 in starts[1:]:
            row = jnp.where(j >= first_tile, row0 + (j - first_tile) * tn, row)
        return pl.multiple_of(row, BF16_ROWS), 0

    assert all(row0 % BF16_ROWS == 0 for row0, _ in w_rows)
    return pl.BlockSpec((pl.Element(tn), pl.Element(k)), index_map)


def matmul(a, w, out_dtype, epilogue=_epi_plain, row_vec=None, residual=None, tm=1024, tn=1024,
           w_is_nk=False, w_rows=None, name="matmul"):
    m, k = a.shape
    if w_rows is not None:
        w_is_nk = True
        n = sum(rows for _, rows in w_rows)
        tn = min(tn, min(rows for _, rows in w_rows))
    else:
        n = w.shape[0 if w_is_nk else 1]
    tm, tn = _tile(m, tm), _tile(n, tn)
    in_specs = [pl.BlockSpec((tm, k), lambda i, j: (i, 0)),
                _w_rows_spec(w_rows, tn, k) if w_rows is not None
                else pl.BlockSpec((tn, k), lambda i, j: (j, 0)) if w_is_nk
                else pl.BlockSpec((k, tn), lambda i, j: (0, j))]
    operands = [a, w]
    blocks = [_nbytes((tm, k), a.dtype), _nbytes((k, tn), w.dtype), _nbytes((tm, tn), out_dtype)]
    if row_vec is not None:
        in_specs.append(pl.BlockSpec((1, tn), lambda i, j: (0, j)))
        operands.append(row_vec.reshape(1, n).astype(F32))
    if residual is not None:
        in_specs.append(pl.BlockSpec((tm, tn), lambda i, j: (i, j)))
        operands.append(residual)
        blocks.append(_nbytes((tm, tn), residual.dtype))
    return pl.pallas_call(
        functools.partial(_mm_kernel, epilogue=epilogue, w_is_nk=w_is_nk),
        grid=(m // tm, n // tn),
        in_specs=in_specs,
        out_specs=pl.BlockSpec((tm, tn), lambda i, j: (i, j)),
        out_shape=jax.ShapeDtypeStruct((m, n), out_dtype),
        compiler_params=pltpu.CompilerParams(
            dimension_semantics=("parallel", "parallel"),
            vmem_limit_bytes=_vmem_limit(blocks, 2 * _nbytes((tm, tn), F32))),
        name=name,
    )(*operands)


def _mm_t_kernel(wt_ref, a_ref, o_ref):
    o_ref[0] = lax.dot_general(wt_ref[...], a_ref[...], (((1,), (1,)), ((), ())),
                               preferred_element_type=F32).astype(o_ref.dtype)


def matmul_t(a, w_t, out_dtype, tm=512, tn=1024, w_rows=None, name="matmul_t"):
    m, k = a.shape
    n = w_t.shape[0] if w_rows is None else sum(rows for _, rows in w_rows)
    tm, tn = _tile(m, tm), _tile(n, tn)
    return pl.pallas_call(
        _mm_t_kernel,
        grid=(m // tm, n // tn),
        in_specs=[pl.BlockSpec((tn, k), lambda i, j: (j, 0)) if w_rows is None else _w_rows_spec(w_rows, tn, k),
                  pl.BlockSpec((tm, k), lambda i, j: (i, 0))],
        out_specs=pl.BlockSpec((1, tn, tm), lambda i, j: (i, j, 0)),
        out_shape=jax.ShapeDtypeStruct((m // tm, n, tm), out_dtype),
        compiler_params=pltpu.CompilerParams(
            dimension_semantics=("parallel", "parallel"),
            vmem_limit_bytes=_vmem_limit([_nbytes((tm, k), a.dtype), _nbytes((tn, k), w_t.dtype),
                                          _nbytes((tn, tm), out_dtype)], 2 * _nbytes((tn, tm), F32))),
        name=name,
    )(w_t, a)


N_SPLIT = 3


def _fox_bias_kernel(f_ref, b_ref, e_ref, o_ref, carry):
    @pl.when(pl.program_id(1) == 0)
    def _():
        carry[...] = jnp.zeros(carry.shape, F32)

    x = f_ref[...] + b_ref[...]
    a = jnp.minimum(x, 0.0) - jnp.log(1.0 + jnp.exp(-jnp.abs(x)))
    r = a.shape[0]
    row = lax.broadcasted_iota(jnp.int32, a.shape, 0)
    sh = 1
    while sh < r:
        a = a + jnp.where(row >= sh, pltpu.roll(a, sh, axis=0), 0.0)
        sh *= 2
    c = a + carry[...]
    carry[...] = c[r - 1:r, :]
    rest = c * (-LOG2E)
    pieces = []
    for _ in range(N_SPLIT):
        piece = rest.astype(BF16)
        pieces.append(piece)
        rest = rest - piece.astype(F32)
    o_ref[...] = jnp.dot(jnp.concatenate(pieces, axis=1), e_ref[...],
                         preferred_element_type=F32).astype(o_ref.dtype)


def fox_key_bias(ff, bias, batch, heads, head_dim, rows=1024):
    m = ff.shape[0]
    s = m // batch
    rows = _tile(s, rows)
    nr = s // rows
    src = jnp.arange(N_SPLIT * LANES)
    dst = (src % LANES) * head_dim + src // LANES
    place = ((jnp.arange(heads * head_dim)[None, :] == dst[:, None])
             & ((src % LANES) < heads)[:, None]).astype(BF16)
    bias_row = jnp.zeros((1, LANES), F32).at[0, :heads].set(bias.astype(F32))
    return pl.pallas_call(
        _fox_bias_kernel,
        grid=(batch, nr),
        in_specs=[pl.BlockSpec((rows, LANES), lambda b, t: (b * nr + t, 0)),
                  pl.BlockSpec((1, LANES), lambda b, t: (0, 0)),
                  pl.BlockSpec((N_SPLIT * LANES, heads * head_dim), lambda b, t: (0, 0))],
        out_specs=pl.BlockSpec((rows, heads * head_dim), lambda b, t: (b * nr + t, 0)),
        out_shape=jax.ShapeDtypeStruct((m, heads * head_dim), BF16),
        scratch_shapes=[pltpu.VMEM((1, LANES), F32)],
        compiler_params=pltpu.CompilerParams(
            dimension_semantics=("parallel", "arbitrary"),
            vmem_limit_bytes=_vmem_limit([_nbytes((rows, heads * head_dim), BF16),
                                          _nbytes((N_SPLIT * LANES, heads * head_dim), BF16)],
                                         _nbytes((rows, heads * head_dim), F32) + 16 * _nbytes((rows, LANES), F32))),
        name="fox_key_bias",
    )(ff, bias_row, place)


SUM_ROWS = 16


KIND_PLAIN, KIND_DIAG, KIND_PAD = 0, 1, 2


def _fox_attn_kernel(row_ref, col_ref, kind_ref, start_ref, q_ref, k_ref, kb_ref, vt_ref, o_ref,
                     qt_scr, mask_scr, s_scr, m_scr, acc_scr, snap_scr, *, n_steps):
    nq, hd2, t = qt_scr.shape
    hd = hd2 // 2

    causal = (lax.broadcasted_iota(jnp.int32, (t, t), 0) <= lax.broadcasted_iota(jnp.int32, (t, t), 1))
    mask_scr[KIND_PLAIN] = jnp.zeros((t, t), F32)
    mask_scr[KIND_DIAG] = jnp.where(causal, 0.0, -jnp.inf)
    mask_scr[KIND_PAD] = jnp.full((t, t), -jnp.inf, F32)

    for r in range(nq):
        qt_scr[r, :hd, :] = q_ref[r * t:(r + 1) * t, :].astype(F32).T.astype(BF16)
        qt_scr[r, hd:, :] = jnp.where(lax.broadcasted_iota(jnp.int32, (hd, t), 0) < N_SPLIT,
                                      1.0, 0.0).astype(BF16)
    m_scr[...] = jnp.zeros(m_scr.shape, F32)
    acc_scr[...] = jnp.zeros(acc_scr.shape, F32)

    def scores(p):
        off = pl.multiple_of(col_ref[p] * t, t)
        k_aug = jnp.concatenate([k_ref[pl.ds(off, t), :], kb_ref[pl.ds(off, t), :]], axis=1)
        return jnp.dot(k_aug, qt_scr[row_ref[p]], preferred_element_type=F32)

    def update(s, p):
        s = s + mask_scr[kind_ref[p]]
        m_prev = jnp.where(start_ref[p] == 1, -jnp.inf, m_scr[...])
        m_new = jnp.maximum(m_prev, jnp.max(s, axis=0, keepdims=True))
        alpha = jnp.exp2(m_prev - m_new)
        prob = jnp.exp2((s - m_new).astype(BF16))
        vt_aug = jnp.concatenate([vt_ref[0, col_ref[p]], jnp.ones((SUM_ROWS, t), BF16)], axis=0)
        acc = alpha * acc_scr[...] + jnp.dot(vt_aug, prob, preferred_element_type=F32)
        m_scr[...] = m_new
        acc_scr[...] = acc
        return acc

    def write_row(p, acc_ref):
        @pl.when(kind_ref[p] == KIND_DIAG)
        def _():
            out = acc_ref[:hd, :] * (1.0 / acc_ref[hd:hd + 1, :])
            o_ref[pl.ds(pl.multiple_of(row_ref[p] * t, t), t), :] = out.T.astype(o_ref.dtype)

    group = s_scr.shape[0]
    s_scr[0] = scores(0)

    def pair_group(u, carry):
        p = group * u
        for k in range(group):
            s_scr[(k + 1) % group] = scores(p + k + 1)
            acc = update(s_scr[k], p + k)
            if k + 1 < group:
                snap_scr[k] = acc
        for k in range(group):
            write_row(p + k, snap_scr.at[k] if k + 1 < group else acc_scr)
        return carry

    lax.fori_loop(0, n_steps, pair_group, 0)


def fox_attention(qk, kbias, v_t, batch, heads, head_dim):
    m = qk.shape[0]
    s = m // batch
    t = v_t.shape[2]
    nq = s // t
    pairs = [(i, j, KIND_DIAG if i == j else KIND_PLAIN, int(j == 0)) for i in range(nq) for j in range(i + 1)]
    n_steps = -(-len(pairs) // FOX_GROUP)
    pairs += [(nq - 1, nq - 1, KIND_PAD, 0)] * (FOX_GROUP * n_steps + 1 - len(pairs))
    tables = [jnp.asarray(v, jnp.int32) for v in zip(*pairs)]
    grid_spec = pltpu.PrefetchScalarGridSpec(
        num_scalar_prefetch=len(tables),
        grid=(batch, heads),
        in_specs=[pl.BlockSpec((s, head_dim), lambda b, h, *_: (b, h)),
                  pl.BlockSpec((s, head_dim), lambda b, h, *_: (b, heads + h)),
                  pl.BlockSpec((s, head_dim), lambda b, h, *_: (b, h)),
                  pl.BlockSpec((1, nq, head_dim, t), lambda b, h, *_: (b, 0, h, 0))],
        out_specs=pl.BlockSpec((s, head_dim), lambda b, h, *_: (b, h)),
        scratch_shapes=[pltpu.VMEM((nq, 2 * head_dim, t), BF16), pltpu.VMEM((3, t, t), F32),
                        pltpu.VMEM((FOX_GROUP, t, t), F32),
                        pltpu.VMEM((1, t), F32), pltpu.VMEM((head_dim + SUM_ROWS, t), F32),
                        pltpu.VMEM((FOX_GROUP - 1, head_dim + SUM_ROWS, t), F32)],
    )
    return pl.pallas_call(
        functools.partial(_fox_attn_kernel, n_steps=n_steps),
        grid_spec=grid_spec,
        out_shape=jax.ShapeDtypeStruct((m, heads * head_dim), BF16),
        compiler_params=pltpu.CompilerParams(
            dimension_semantics=("parallel", "parallel"),
            vmem_limit_bytes=_vmem_limit([5 * _nbytes((s, head_dim), BF16)],
                                         2 * _nbytes((s, head_dim), BF16) + 10 * _nbytes((t, t), F32))),
        name="fox_attention",
    )(*tables, qk, qk, kbias, v_t.reshape(batch, nq, heads * head_dim, t))


def _hgrn_kernel(q_ref, z_ref, v_ref, hg_ref, lb_ref, og_ref, o_ref, st_ref):
    @pl.when(pl.program_id(2) == 0)
    def _():
        st_ref[...] = jnp.zeros(st_ref.shape, F32)

    t, dk = z_ref.shape
    dv = v_ref.shape[1]
    nc, nb, nsub = t // CHUNK, t // SUB, CHUNK // SUB

    q = q_ref[...].astype(F32)
    z = z_ref[...]
    lb = lb_ref[...]
    g = jnp.log(lb + (1.0 - lb) * jax.nn.sigmoid(z))
    kk = (1.0 - lb) * jax.nn.sigmoid(-z)

    row = lax.broadcasted_iota(jnp.int32, (t, dk), 0)
    rin = row % CHUNK
    b = g
    sh = 1
    while sh < CHUNK:
        b = b + jnp.where(rin >= sh, pltpu.roll(b, sh, axis=0), 0.0)
        sh *= 2

    b3 = b.reshape(nc, CHUNK, dk)
    q3 = q.reshape(nc, CHUNK, dk)
    k3 = kk.reshape(nc, CHUNK, dk)
    v3 = v_ref[...].reshape(nc, CHUNK, dv)
    b_last = b3[:, CHUNK - 1:CHUNK, :]
    q_in = (q3 * jnp.exp(b3)).astype(BF16)
    k_out = (k3 * jnp.exp(b_last - b3)).astype(BF16)
    d_last = jnp.exp(b_last)

    st = st_ref[...]
    inter = []
    for c in range(nc):
        inter.append(lax.dot_general(q_in[c], st.astype(BF16), (((1,), (1,)), ((), ())),
                                     preferred_element_type=F32))
        kv_t = lax.dot_general(v3[c], k_out[c], (((0,), (0,)), ((), ())),
                               preferred_element_type=F32)
        st = st * d_last[c] + kv_t
    st_ref[...] = st
    inter = jnp.concatenate(inter, axis=0)

    q_parts, k_parts = [], []
    for blk in range(1, nsub):
        lo, hi = blk * SUB, (blk + 1) * SUB
        b_ref_pt = b3[:, lo - 1:lo, :]
        q_hat = (q3[:, lo:hi, :] * jnp.exp(b3[:, lo:hi, :] - b_ref_pt)).astype(BF16)
        k_hat = (k3[:, :lo, :] * jnp.exp(b_ref_pt - b3[:, :lo, :])).astype(BF16)
        q_rows = [jnp.zeros((nc, lo, dk), BF16), q_hat] + ([jnp.zeros((nc, CHUNK - hi, dk), BF16)] if hi < CHUNK else [])
        q_parts.append(jnp.concatenate(q_rows, axis=1))
        k_parts.append(jnp.concatenate([k_hat, jnp.zeros((nc, CHUNK - lo, dk), BF16)], axis=1))
    q_cat = jnp.concatenate(q_parts, axis=2)
    k_cat = jnp.concatenate(k_parts, axis=2)
    s_off = lax.dot_general(q_cat, k_cat, (((2,), (2,)), ((0,), (0,))),
                            preferred_element_type=F32)

    b4 = b.reshape(nb, SUB, dk)
    q4 = q.reshape(nb, SUB, dk)
    k4 = kk.reshape(nb, SUB, dk)
    rsub = (row % SUB).reshape(nb, SUB, dk)
    u_row = lax.broadcasted_iota(jnp.int32, (dk, LANES), 1)
    half = SUB // 2
    s_lo = jnp.zeros((t, LANES), F32)
    s_hi = jnp.zeros((t // 2, LANES), F32)
    for j in range(SUB):
        r0 = 0 if j < half else half
        bj, kj = b4[:, j:j + 1, :], k4[:, j:j + 1, :]
        decay = jnp.exp(jnp.where(rsub[:, r0:, :] >= j, b4[:, r0:, :] - bj, -jnp.inf))
        p = (q4[:, r0:, :] * (kj * decay)).reshape(-1, dk).astype(BF16)
        place = jnp.where((u_row % SUB == j) & (u_row < CHUNK), 1.0, 0.0).astype(BF16)
        if j < half:
            s_lo = s_lo + jnp.dot(p, place, preferred_element_type=F32)
        else:
            s_hi = s_hi + jnp.dot(p, place, preferred_element_type=F32)
    s_diag = s_lo.reshape(nb, SUB, LANES) + jnp.concatenate(
        [jnp.zeros((nb, half, LANES), F32), s_hi.reshape(nb, half, LANES)], axis=1)
    s_diag = s_diag.reshape(nc, CHUNK, LANES)[:, :, :CHUNK]

    rb = lax.broadcasted_iota(jnp.int32, (nc, CHUNK, CHUNK), 1) // SUB
    cb = lax.broadcasted_iota(jnp.int32, (nc, CHUNK, CHUNK), 2) // SUB
    scores = jnp.where(cb == rb, s_diag, jnp.where(cb < rb, s_off, 0.0)).astype(BF16)
    intra = lax.dot_general(scores, v3, (((2,), (1,)), ((0,), (0,))),
                            preferred_element_type=F32).reshape(t, dv)

    o = inter + intra
    o = o * lax.rsqrt(jnp.mean(o * o, axis=-1, keepdims=True) + EPS) * og_ref[...]
    hg = hg_ref[...].astype(F32)
    o_ref[...] = (o * (hg * jax.nn.sigmoid(hg))).astype(o_ref.dtype)


def hgrn2(src, q_col0, v_col0, hg_col0, z, lb, out_g, batch, heads, rows=1024):
    m = z.shape[0]
    s = m // batch
    dk = HGRN_EXPAND
    dv = out_g.shape[-1]
    rows = _tile(s, rows)
    nt = s // rows
    row_blk = lambda b, h, t: b * nt + t
    return pl.pallas_call(
        _hgrn_kernel,
        grid=(batch, heads, nt),
        in_specs=[pl.BlockSpec((rows, dk), lambda b, h, t: (row_blk(b, h, t), q_col0 + h)),
                  pl.BlockSpec((rows, dk), lambda b, h, t: (row_blk(b, h, t), h)),
                  pl.BlockSpec((rows, dv), lambda b, h, t: (row_blk(b, h, t), v_col0 + h)),
                  pl.BlockSpec((rows, dv), lambda b, h, t: (row_blk(b, h, t), hg_col0 + h)),
                  pl.BlockSpec((1, dk), lambda b, h, t: (0, h)),
                  pl.BlockSpec((1, dv), lambda b, h, t: (0, 0))],
        out_specs=pl.BlockSpec((rows, dv), lambda b, h, t: (row_blk(b, h, t), h)),
        out_shape=jax.ShapeDtypeStruct((m, heads * dv), BF16),
        scratch_shapes=[pltpu.VMEM((dv, dk), F32)],
        compiler_params=pltpu.CompilerParams(
            dimension_semantics=("parallel", "parallel", "arbitrary"),
            vmem_limit_bytes=_vmem_limit([6 * _nbytes((rows, dk), F32)], 40 * _nbytes((rows, dk), F32))),
        name="hgrn2",
    )(src, z, src, src, lb.reshape(1, -1).astype(F32), out_g.reshape(1, dv).astype(F32))


def _mem_attn_kernel(q_ref, k_ref, v_ref, o_ref):
    s = lax.dot_general(q_ref[...], k_ref[...], (((1,), (1,)), ((), ())), preferred_element_type=F32)
    m = jnp.max(s, axis=1, keepdims=True)
    p = jnp.exp(s - m)
    l = jnp.sum(p, axis=1, keepdims=True)
    o = jnp.dot(p.astype(BF16), v_ref[...], preferred_element_type=F32)
    o_ref[...] = (o / l).astype(o_ref.dtype)


def mem_attention(q, mk, mv, batch, heads, head_dim, tq=1024):
    m = q.shape[0]
    s = m // batch
    n_mem = mk.shape[0] // batch
    tq = _tile(s, tq)
    nq = s // tq
    return pl.pallas_call(
        _mem_attn_kernel,
        grid=(batch, heads, nq),
        in_specs=[pl.BlockSpec((tq, head_dim), lambda b, h, i: (b * nq + i, h)),
                  pl.BlockSpec((n_mem, head_dim), lambda b, h, i: (b, h)),
                  pl.BlockSpec((n_mem, head_dim), lambda b, h, i: (b, h))],
        out_specs=pl.BlockSpec((tq, head_dim), lambda b, h, i: (b * nq + i, h)),
        out_shape=jax.ShapeDtypeStruct((m, heads * head_dim), BF16),
        compiler_params=pltpu.CompilerParams(
            dimension_semantics=("parallel", "parallel", "parallel"),
            vmem_limit_bytes=_vmem_limit([2 * _nbytes((tq, head_dim), BF16), 2 * _nbytes((n_mem, head_dim), BF16)],
                                         4 * _nbytes((tq, head_dim), F32))),
        name="mem_attention",
    )(q, mk, mv)


def _merge_kernel(a_ref, b_ref, c_ref, wa_ref, wb_ref, wc_ref, ga_ref, gb_ref, gc_ref, o_ref):
    acc = ga_ref[...].astype(F32) * jnp.dot(a_ref[...], wa_ref[...], preferred_element_type=F32)
    acc = acc + gb_ref[...].astype(F32) * jnp.dot(b_ref[...], wb_ref[...], preferred_element_type=F32)
    acc = acc + gc_ref[...].astype(F32) * jnp.dot(c_ref[...], wc_ref[...], preferred_element_type=F32)
    o_ref[...] = acc.astype(o_ref.dtype)


def gated_merge(o_a, o_b, o_c, w_a, w_b, w_c, gates, tm=1024, tn=512):
    m = o_a.shape[0]
    d = w_a.shape[1]
    tm, tn = _tile(m, tm), _tile(d, tn)
    nj = d // tn
    a_spec = lambda arr: pl.BlockSpec((tm, arr.shape[1]), lambda i, j: (i, 0))
    w_spec = lambda arr: pl.BlockSpec((arr.shape[0], tn), lambda i, j: (0, j))
    g_spec = lambda r: pl.BlockSpec((tm, tn), lambda i, j: (i, r * nj + j))
    blocks = ([_nbytes((tm, a.shape[1]), BF16) for a in (o_a, o_b, o_c)]
              + [_nbytes((w.shape[0], tn), BF16) for w in (w_a, w_b, w_c)] + 4 * [_nbytes((tm, tn), BF16)])
    return pl.pallas_call(
        _merge_kernel,
        grid=(m // tm, nj),
        in_specs=[a_spec(o_a), a_spec(o_b), a_spec(o_c), w_spec(w_a), w_spec(w_b), w_spec(w_c),
                  g_spec(0), g_spec(1), g_spec(2)],
        out_specs=pl.BlockSpec((tm, tn), lambda i, j: (i, j)),
        out_shape=jax.ShapeDtypeStruct((m, d), BF16),
        compiler_params=pltpu.CompilerParams(
            dimension_semantics=("parallel", "parallel"),
            vmem_limit_bytes=_vmem_limit(blocks, 3 * _nbytes((tm, tn), F32))),
        name="gated_merge",
    )(o_a, o_b, o_c, w_a, w_b, w_c, gates, gates, gates)


def _router_kernel(h_ref, g_ref, whi_ref, wlo_ref, b_ref, lg_ref, id_ref, wt_ref, *, n_groups):
    x = h_ref[...]
    hn = x * lax.rsqrt(jnp.mean(x * x, axis=-1, keepdims=True) + EPS) * g_ref[...]
    hn_hi = hn.astype(BF16)
    hn_lo = (hn - hn_hi.astype(F32)).astype(BF16)
    logits = (jnp.dot(hn_hi, whi_ref[...], preferred_element_type=F32)
              + (jnp.dot(hn_hi, wlo_ref[...], preferred_element_type=F32)
                 + jnp.dot(hn_lo, whi_ref[...], preferred_element_type=F32))) + b_ref[...]
    lane = lax.broadcasted_iota(jnp.int32, logits.shape, 1).astype(F32)
    lane_grp = lg_ref[...]
    neg = -jnp.inf
    big = float(LANES)

    gl = jnp.where(lane < n_groups, logits, neg)
    gmax = jnp.max(gl, axis=1, keepdims=True)
    grp = jnp.min(jnp.where(gl == gmax, lane, big), axis=1, keepdims=True)
    p_grp = 1.0 / jnp.sum(jnp.exp(gl - gmax), axis=1, keepdims=True)

    el = jnp.where(lane_grp == grp, logits, neg)
    v1 = jnp.max(el, axis=1, keepdims=True)
    i1 = jnp.min(jnp.where(el == v1, lane, big), axis=1, keepdims=True)
    el2 = jnp.where(lane == i1, neg, el)
    v2 = jnp.max(el2, axis=1, keepdims=True)
    i2 = jnp.min(jnp.where(el2 == v2, lane, big), axis=1, keepdims=True)
    e21 = jnp.exp(v2 - v1)
    w1 = p_grp / (1.0 + e21)
    w2 = p_grp * e21 / (1.0 + e21)

    id_ref[...] = jnp.where(lane == 0.0, i1 - n_groups, i2 - n_groups).astype(jnp.int32)
    wt_ref[...] = jnp.where(lane == 0.0, w1, w2)


def moe_router(h, norm_g, w_group, b_group, w_router, b_router, tm=256):
    n, d = h.shape
    n_groups = w_group.shape[1]
    n_experts = w_router.shape[1]
    per_group = n_experts // n_groups
    assert n_groups + n_experts <= LANES
    pad = LANES - n_groups - n_experts
    w = jnp.concatenate([w_group, w_router, jnp.zeros((d, pad), F32)], axis=1).astype(F32)
    bias = jnp.concatenate([b_group, b_router, jnp.zeros((pad,), F32)]).reshape(1, LANES).astype(F32)
    w_hi = w.astype(BF16)
    w_lo = (w - w_hi.astype(F32)).astype(BF16)
    lane = jnp.arange(LANES)
    lane_grp = jnp.where((lane >= n_groups) & (lane < n_groups + n_experts),
                         (lane - n_groups) // per_group, -1).astype(F32).reshape(1, LANES)
    tm = _tile(n, tm)
    ids, wts = pl.pallas_call(
        functools.partial(_router_kernel, n_groups=n_groups),
        grid=(n // tm,),
        in_specs=[pl.BlockSpec((tm, d), lambda i: (i, 0)),
                  pl.BlockSpec((1, d), lambda i: (0, 0)),
                  pl.BlockSpec((d, LANES), lambda i: (0, 0)),
                  pl.BlockSpec((d, LANES), lambda i: (0, 0)),
                  pl.BlockSpec((1, LANES), lambda i: (0, 0)),
                  pl.BlockSpec((1, LANES), lambda i: (0, 0))],
        out_specs=[pl.BlockSpec((tm, LANES), lambda i: (i, 0)),
                   pl.BlockSpec((tm, LANES), lambda i: (i, 0))],
        out_shape=[jax.ShapeDtypeStruct((n, LANES), jnp.int32),
                   jax.ShapeDtypeStruct((n, LANES), F32)],
        compiler_params=pltpu.CompilerParams(
            dimension_semantics=("parallel",),
            vmem_limit_bytes=_vmem_limit([_nbytes((tm, d), F32), _nbytes((d, LANES), F32)],
                                         8 * _nbytes((tm, d), F32))),
        name="moe_router",
    )(h, norm_g.reshape(1, d).astype(F32), w_hi, w_lo, bias, lane_grp)
    return ids[:, :TOP_K], wts


def _row_copy(src_hbm, row, dst, slot, sem):
    return pltpu.make_async_copy(src_hbm.at[pl.ds(row, 1)], dst.at[pl.ds(slot, 1)], sem)


GATHER_UNROLL = 8


DMA_PRIORITIES = 2


def _gather_rows(src_hbm, ids_ref, n_ids, stride, offset, dst, sem, wait, spread=False):
    lanes = DMA_PRIORITIES if spread else 1
    assert n_ids % lanes == 0

    def body(i, c):
        for p in range(lanes):
            r = lanes * i + p
            cp = _row_copy(src_hbm, ids_ref[0, 0, stride * r + offset], dst, r, sem)
            if wait:
                cp.wait()
            else:
                cp.start(priority=p)
        return c

    lax.fori_loop(0, n_ids // lanes, body, 0, unroll=GATHER_UNROLL // lanes)


W_STAGE_ROWS = 512
W_STAGE_SLOTS = 8


class _ExpertWeights:
    def __init__(self, hbm, stage, sems, bufs):
        self.hbm, self.stage, self.sems, self.bufs = hbm, stage, sems, bufs
        d, de = hbm[0].shape[1], hbm[0].shape[2]
        self.de = de
        self.sr = stage.shape[1]
        self.slots = stage.shape[0]
        self.ratio = d // de
        self.rb = self.sr // self.ratio
        self.n_a = d // self.sr
        self.n_chunks = 2 * self.n_a + de // self.rb

    def _each_copy(self, idx, e, fn):
        slot = idx % self.slots
        sr, n_a = self.sr, self.n_a
        for mat in range(2):
            @pl.when((idx >= mat * n_a) & (idx < (mat + 1) * n_a))
            def _(mat=mat):
                r0 = pl.multiple_of((idx - mat * n_a) * sr, sr)
                fn(pltpu.make_async_copy(self.hbm[mat].at[e, pl.ds(r0, sr), :], self.stage.at[slot],
                                         self.sems.at[slot]))

        @pl.when(idx >= 2 * n_a)
        def _():
            r0 = pl.multiple_of((idx - 2 * n_a) * self.rb, self.rb)
            for k in range(self.ratio):
                fn(pltpu.make_async_copy(self.hbm[2].at[e, pl.ds(r0, self.rb), pl.ds(k * self.de, self.de)],
                                         self.stage.at[slot, pl.ds(k * self.rb, self.rb), :],
                                         self.sems.at[slot]))

    def start(self, idx, e):
        self._each_copy(idx, e, lambda cp: cp.start(priority=DMA_PRIORITIES - 1))

    def wait(self, idx, e):
        self._each_copy(idx, e, lambda cp: cp.wait())

    def convert(self, idx, dst):
        slot = idx % self.slots
        sr, n_a = self.sr, self.n_a
        for mat in range(2):
            @pl.when((idx >= mat * n_a) & (idx < (mat + 1) * n_a))
            def _(mat=mat):
                r0 = pl.multiple_of((idx - mat * n_a) * sr, sr)
                self.bufs[mat][dst, pl.ds(r0, sr), :] = self.stage[slot].astype(BF16)

        @pl.when(idx >= 2 * n_a)
        def _():
            r0 = pl.multiple_of((idx - 2 * n_a) * self.rb, self.rb)
            for k in range(self.ratio):
                self.bufs[2][dst, pl.ds(r0, self.rb), pl.ds(k * self.de, self.de)] = (
                    self.stage[slot, pl.ds(k * self.rb, self.rb), :].astype(BF16))

    def start_first(self, e):
        for idx in range(min(self.slots, self.n_chunks)):
            self.start(jnp.int32(idx), e)

    def stream(self, e, dst, first, count):
        def body(c, carry):
            idx = first + c
            self.wait(idx, e)
            self.convert(idx, dst)

            @pl.when(idx + self.slots < self.n_chunks)
            def _():
                self.start(idx + self.slots, e)
            return carry

        lax.fori_loop(0, count, body, 0)


def _moe_expert_kernel(be_ref, nused_ref, pos_ref, nexte_ref, par_ref, c0_ref, cn_ref, tok_ref, tok_next_ref,
                       h_hbm, g_ref, wg_hbm, wu_hbm, wd_hbm, y_ref, x_buf, sems, stage, wsems,
                       wg_buf, wu_buf, wd_buf):
    i = pl.program_id(0)
    n_used = nused_ref[0]
    rows = x_buf.shape[1]
    slot = i % 2
    cur = par_ref[i]
    next_e = nexte_ref[i]
    w = _ExpertWeights((wg_hbm, wu_hbm, wd_hbm), stage, wsems, (wg_buf, wu_buf, wd_buf))

    @pl.when((i == 0) & (n_used > 0))
    def _():
        _gather_rows(h_hbm, tok_ref, rows, 1, 0, x_buf.at[0], sems.at[0], wait=False)
        w.start_first(be_ref[0])
        w.stream(be_ref[0], cur, 0, w.n_chunks)

    @pl.when(i + 1 < n_used)
    def _():
        _gather_rows(h_hbm, tok_next_ref, rows, 1, 0, x_buf.at[1 - slot], sems.at[1 - slot], wait=False)

    @pl.when((i < n_used) & (next_e >= 0))
    def _():
        @pl.when(pos_ref[i] == 0)
        def _():
            w.start_first(next_e)

        w.stream(next_e, 1 - cur, c0_ref[i], cn_ref[i])

    @pl.when(i < n_used)
    def _():
        _gather_rows(h_hbm, tok_ref, rows, 1, 0, x_buf.at[slot], sems.at[slot], wait=True)
        x = x_buf[slot]
        x = (x * lax.rsqrt(jnp.mean(x * x, axis=-1, keepdims=True) + EPS) * g_ref[...]).astype(BF16)
        a = jnp.dot(x, wg_buf[cur], preferred_element_type=F32)
        u = jnp.dot(x, wu_buf[cur], preferred_element_type=F32)
        hmid = (a * jax.nn.sigmoid(a) * u).astype(BF16)
        y_ref[...] = jnp.dot(hmid, wd_buf[cur], preferred_element_type=F32)

    @pl.when(i >= n_used)
    def _():
        y_ref[...] = jnp.zeros(y_ref.shape, y_ref.dtype)


def moe_experts(h, norm_g, tok_buf, sched, w_gate, w_up, w_down):
    n, d = h.shape
    blk_e, n_used, run_pos, run_len, next_e, parity = sched
    n_blocks = blk_e.shape[0]
    de = w_gate.shape[2]
    sr = min(W_STAGE_ROWS, d)
    assert d % sr == 0 and d % de == 0 and sr % (d // de) == 0 and de % (sr // (d // de)) == 0
    n_chunks = 2 * (d // sr) + de // (sr // (d // de))
    spread = jnp.maximum(run_len - 1, 1)
    k = jnp.maximum(run_pos - 1, 0)
    base, extra = n_chunks // spread, n_chunks % spread
    single = run_len == 1
    chunk0 = jnp.where(single, 0, k * base + jnp.minimum(k, extra)).astype(jnp.int32)
    chunk_n = jnp.where(single, n_chunks,
                        jnp.where(run_pos == 0, 0, base + (k < extra))).astype(jnp.int32)
    sched = (blk_e, n_used, run_pos, next_e, parity, chunk0, chunk_n)
    tok_blocks = tok_buf.reshape(n_blocks, 1, MOE_BLOCK)
    n_pre = len(sched)
    grid_spec = pltpu.PrefetchScalarGridSpec(
        num_scalar_prefetch=n_pre,
        grid=(n_blocks,),
        in_specs=[pl.BlockSpec((1, 1, MOE_BLOCK), lambda i, *_: (i, 0, 0), memory_space=pltpu.SMEM),
                  pl.BlockSpec((1, 1, MOE_BLOCK), lambda i, *_: (jnp.minimum(i + 1, n_blocks - 1), 0, 0),
                               memory_space=pltpu.SMEM),
                  pl.BlockSpec(memory_space=pl.ANY),
                  pl.BlockSpec((1, d), lambda i, *_: (0, 0)),
                  pl.BlockSpec(memory_space=pl.ANY),
                  pl.BlockSpec(memory_space=pl.ANY),
                  pl.BlockSpec(memory_space=pl.ANY)],
        out_specs=pl.BlockSpec((MOE_BLOCK, d), lambda i, *_: (i, 0)),
        scratch_shapes=[pltpu.VMEM((2, MOE_BLOCK, d), F32), pltpu.SemaphoreType.DMA((2,)),
                        pltpu.VMEM((W_STAGE_SLOTS, sr, de), F32), pltpu.SemaphoreType.DMA((W_STAGE_SLOTS,)),
                        pltpu.VMEM((2, d, de), BF16), pltpu.VMEM((2, d, de), BF16),
                        pltpu.VMEM((2, de, d), BF16)],
    )
    scratch_bytes = (2 * _nbytes((MOE_BLOCK, d), F32) + W_STAGE_SLOTS * _nbytes((sr, de), F32)
                     + 6 * _nbytes((d, de), BF16))
    return pl.pallas_call(
        _moe_expert_kernel,
        grid_spec=grid_spec,
        out_shape=jax.ShapeDtypeStruct((n_blocks * MOE_BLOCK, d), F32),
        compiler_params=pltpu.CompilerParams(
            dimension_semantics=("arbitrary",),
            vmem_limit_bytes=_vmem_limit([_nbytes((MOE_BLOCK, d), F32)],
                                         scratch_bytes + 4 * _nbytes((MOE_BLOCK, d), F32))),
        name="moe_experts",
    )(*sched, tok_blocks, tok_blocks, h, norm_g.reshape(1, d).astype(F32), w_gate, w_up, w_down)


def _combine_kernel(dest_ref, dest_next_ref, h_ref, w_ref, y_hbm, o_ref, buf, sems):
    i = pl.program_id(0)
    rows = buf.shape[2]
    slot = i % 2

    def gather(ids_ref, s, wait):
        for k in range(TOP_K):
            _gather_rows(y_hbm, ids_ref, rows, TOP_K, k, buf.at[s, k], sems.at[s], wait, spread=True)

    @pl.when(i == 0)
    def _():
        gather(dest_ref, 0, False)

    @pl.when(i + 1 < pl.num_programs(0))
    def _():
        gather(dest_next_ref, 1 - slot, False)

    gather(dest_ref, slot, True)
    w = w_ref[...]
    o_ref[...] = h_ref[...] + (buf[slot, 0] * w[:, 0:1] + buf[slot, 1] * w[:, 1:2])


def moe_combine(h, y_buf, dest, weights, tm=128):
    n, d = h.shape
    tm = _tile(n, tm)
    nt = n // tm
    dest_blocks = dest.reshape(nt, 1, TOP_K * tm)
    return pl.pallas_call(
        _combine_kernel,
        grid=(nt,),
        in_specs=[pl.BlockSpec((1, 1, TOP_K * tm), lambda i: (i, 0, 0), memory_space=pltpu.SMEM),
                  pl.BlockSpec((1, 1, TOP_K * tm), lambda i: (jnp.minimum(i + 1, nt - 1), 0, 0),
                               memory_space=pltpu.SMEM),
                  pl.BlockSpec((tm, d), lambda i: (i, 0)),
                  pl.BlockSpec((tm, LANES), lambda i: (i, 0)),
                  pl.BlockSpec(memory_space=pl.ANY)],
        out_specs=pl.BlockSpec((tm, d), lambda i: (i, 0)),
        out_shape=jax.ShapeDtypeStruct((n, d), F32),
        scratch_shapes=[pltpu.VMEM((2, TOP_K, tm, d), F32), pltpu.SemaphoreType.DMA((2,))],
        compiler_params=pltpu.CompilerParams(
            dimension_semantics=("arbitrary",),
            vmem_limit_bytes=_vmem_limit([2 * _nbytes((tm, d), F32)], 8 * _nbytes((tm, d), F32))),
        name="moe_combine",
    )(dest_blocks, dest_blocks, h, weights, y_buf)


def _moe_dispatch(expert_id, n_experts):
    n = expert_id.shape[0]
    nk = n * TOP_K
    experts = jnp.arange(n_experts, dtype=jnp.int32)

    def lookup(table, idx):
        return jnp.sum(jnp.where(idx[:, None] == experts[None, :], table[None, :], 0), axis=1)

    flat_e = expert_id.reshape(nk).astype(jnp.int32)
    order = jnp.argsort(flat_e).astype(jnp.int32)
    rank = jnp.argsort(order).astype(jnp.int32)
    counts = jnp.sum((flat_e[:, None] == experts[None, :]).astype(jnp.int32), axis=0)
    start = jnp.cumsum(counts) - counts
    padded = (counts + MOE_BLOCK - 1) // MOE_BLOCK * MOE_BLOCK
    pad_end = jnp.cumsum(padded)
    pad_start = pad_end - padded
    n_blocks = (nk + n_experts * (MOE_BLOCK - 1) + MOE_BLOCK - 1) // MOE_BLOCK
    blk = jnp.arange(n_blocks, dtype=jnp.int32)
    blk_e = jnp.minimum(jnp.sum(pad_end[None, :] <= (blk * MOE_BLOCK)[:, None], axis=1),
                        n_experts - 1).astype(jnp.int32)
    dest = (rank + lookup(pad_start - start, flat_e)).astype(jnp.int32)
    blk_pad_start, blk_start, blk_count = (lookup(tbl, blk_e) for tbl in (pad_start, start, counts))
    within = (blk * MOE_BLOCK - blk_pad_start)[:, None] + jnp.arange(MOE_BLOCK, dtype=jnp.int32)[None, :]
    src = jnp.clip(blk_start[:, None] + within, 0, nk - 1)
    tok_buf = jnp.where((within >= 0) & (within < blk_count[:, None]), order[src] // TOP_K, 0)
    tok_buf = tok_buf.reshape(n_blocks * MOE_BLOCK).astype(jnp.int32)
    n_used = (pad_end[-1] // MOE_BLOCK).astype(jnp.int32)
    used = blk < n_used
    is_first = used & (blk_e != jnp.concatenate([jnp.full((1,), -1, jnp.int32), blk_e[:-1]]))
    parity = (jnp.cumsum(is_first.astype(jnp.int32)) - 1) % 2
    run_pos = blk - blk_pad_start // MOE_BLOCK
    run_len = lookup(padded, blk_e) // MOE_BLOCK
    live = jnp.where(counts > 0, experts, n_experts)
    next_live = jnp.concatenate([lax.cummin(live, reverse=True)[1:], jnp.full((1,), n_experts, jnp.int32)])
    blk_next = lookup(next_live, blk_e)
    next_e = jnp.where(blk_next < n_experts, blk_next, -1)
    sched = (blk_e, n_used.reshape(1), run_pos.astype(jnp.int32), run_len.astype(jnp.int32),
             next_e.astype(jnp.int32), parity.astype(jnp.int32))
    return tok_buf, sched, dest


def kernel(x, mem, norm1_g, norm2_g, mem_norm_g, w_in, b_gate, fox_f_bias, fox_q_g, fox_k_g,
           hgrn_lb_logits, hgrn_out_g, mem_q_g, mem_k_g, w_mem_kv, w_branch_fox, w_branch_hgrn,
           w_branch_mem, w_out, w_group, b_group, w_router, b_router, w_exp_gate, w_exp_up,
           w_exp_down):
    batch, seq, d = x.shape
    n_mem = mem.shape[1]
    depth = w_in.shape[0]
    fox_heads, fox_hd = fox_f_bias.shape[1], fox_q_g.shape[1]
    fox_w = fox_heads * fox_hd
    hgrn_w = hgrn_lb_logits.shape[1]
    hgrn_heads = hgrn_w // HGRN_EXPAND
    hgrn_dv = hgrn_out_g.shape[1]
    mem_hd = mem_q_g.shape[1]
    mem_w = w_branch_mem.shape[1]
    mem_heads = mem_w // mem_hd
    n_experts = w_router.shape[2]
    m = batch * seq

    sizes = (fox_w, fox_w, fox_w, fox_heads, hgrn_w, hgrn_w, hgrn_w, hgrn_w, mem_w, 3 * d)
    offs = [0]
    for sz in sizes:
        offs.append(offs[-1] + sz)
    (o_fq, o_fk, o_fv, o_ff, o_hq, o_hf, o_hi, o_hg, o_mq, o_gt, o_end) = offs

    lb_all = jnp.cumsum(jax.nn.softmax(hgrn_lb_logits.astype(F32), axis=0), axis=0)
    mem_n = rmsnorm_rows(mem.reshape(batch * n_mem, d), mem_norm_g, BF16)

    h = x.reshape(m, d)
    for layer in range(depth):
        w_t = jnp.swapaxes(w_in[layer], 0, 1).astype(BF16)
        seg = lambda a, b: ((a, b - a),)
        hn = rmsnorm_rows(h, norm1_g[layer], BF16)

        qk_gain = jnp.concatenate([jnp.tile(fox_q_g[layer] * (LOG2E * fox_hd ** -0.5), fox_heads),
                                   jnp.tile(fox_k_g[layer], fox_heads)])
        qk = matmul(hn, w_t, BF16, functools.partial(_epi_headnorm, head_dim=fox_hd),
                    row_vec=qk_gain, w_rows=seg(o_fq, o_fv), name="proj_fox_qk")
        v_t = matmul_t(hn, w_t, BF16, tm=FOX_BLOCK, w_rows=seg(o_fv, o_ff), name="proj_fox_vt")
        plain = matmul(hn, w_t, BF16, w_rows=seg(o_hq, o_hf) + seg(o_hi, o_mq), name="proj_plain")
        hd_blk = lambda width: width // HGRN_EXPAND
        c_hq = 0
        c_hi = c_hq + hd_blk(hgrn_w)
        c_hg = c_hi + hd_blk(hgrn_w)
        z = matmul(hn, w_t, F32, w_rows=seg(o_hf, o_hi), name="proj_hgrn_f")
        ff = matmul(hn, w_t, F32, tn=LANES, w_rows=seg(o_ff, o_ff + LANES), name="proj_fox_f")
        mq = matmul(hn, w_t, BF16, functools.partial(_epi_headnorm, head_dim=mem_hd),
                    row_vec=jnp.tile(mem_q_g[layer] * mem_hd ** -0.5, mem_heads), tn=mem_hd,
                    w_rows=seg(o_mq, o_gt), name="proj_mem_q")
        gates = matmul(hn, w_t, BF16, _epi_sigmoid_bias, row_vec=b_gate[layer],
                       w_rows=seg(o_gt, o_end), name="proj_gates")

        kbias = fox_key_bias(ff, fox_f_bias[layer], batch, fox_heads, fox_hd)
        o_a = fox_attention(qk, kbias, v_t, batch, fox_heads, fox_hd)

        o_b = hgrn2(plain, c_hq, c_hi, c_hg, z, lb_all[layer], hgrn_out_g[layer], batch, hgrn_heads)

        wkv = w_mem_kv[layer]
        mk = matmul(mem_n, wkv[:, :mem_w].astype(BF16), BF16,
                    functools.partial(_epi_headnorm, head_dim=mem_hd),
                    row_vec=jnp.tile(mem_k_g[layer], mem_heads), tn=mem_hd, name="proj_mem_k")
        mv = matmul(mem_n, wkv[:, mem_w:].astype(BF16), BF16, name="proj_mem_v")
        o_c = mem_attention(mq, mk, mv, batch, mem_heads, mem_hd)

        merged = gated_merge(o_a, o_b, o_c, w_branch_fox[layer].astype(BF16),
                             w_branch_hgrn[layer].astype(BF16), w_branch_mem[layer].astype(BF16), gates)
        h = matmul(merged, w_out[layer].astype(BF16), F32, _epi_residual, residual=h, tn=512,
                   name="proj_out")

        expert_id, weights = moe_router(h, norm2_g[layer], w_group[layer], b_group[layer],
                                        w_router[layer], b_router[layer])
        tok_buf, sched, dest = _moe_dispatch(expert_id, n_experts)
        y_buf = moe_experts(h, norm2_g[layer], tok_buf, sched, w_exp_gate[layer].astype(F32),
                            w_exp_up[layer].astype(F32), w_exp_down[layer].astype(F32))
        h = moe_combine(h, y_buf, dest, weights)
    return h.reshape(batch, seq, d)
```
